```python
import jax, jax.numpy as jnp
from jax import lax
import numpy as np

D_MODEL = 2048
BATCH = 8
SEQ = 8192
DEPTH = 1

GRID_W = 64
CTX_LEN = 256
NH_A = 8
DK_A = 128
DV_A = 256
QK_A = NH_A * DK_A
V_A = NH_A * DV_A
CONV_W = 3
CHUNK = 64
M_INIT = -1e30
NH_B = 16
NKV_B = 4
HD_B = 128
Q_B = NH_B * HD_B
KV_B = NKV_B * HD_B
ROT_HALF = HD_B // 2
ROPE_THETA = 10000.0
Q_BLOCK = 128
EPS = 1e-6
ALPHA = (2 * DEPTH) ** 0.25
BETA = (8 * DEPTH) ** -0.25
KV_WIDTHS = (2 * QK_A, V_A, 4 * NH_A, KV_B, KV_B)
N_KV = 2 * QK_A + V_A + 4 * NH_A + 2 * KV_B
OUT_WIDTHS = (V_A, V_A, Q_B, Q_B, 2 * D_MODEL)
N_IN = N_KV + 2 * V_A + 2 * Q_B + 2 * D_MODEL

kernel_name = "hybrid_mlstm_gqa_dit_block"


def _split(p, widths, start=0):
    outs = []
    off = start
    for w in widths:
        outs.append(p[..., off:off + w])
        off += w
    return outs


def layer_norm(x, w=None, b=None):
    xf = x.astype(jnp.float32)
    mu = xf.mean(-1, keepdims=True)
    var = jnp.mean(jnp.square(xf - mu), -1, keepdims=True)
    y = (xf - mu) * lax.rsqrt(var + EPS)
    if w is not None:
        y = y * w.astype(jnp.float32) + b.astype(jnp.float32)
    return y.astype(x.dtype)


def rms_norm(x, w):
    xf = x.astype(jnp.float32)
    y = xf * lax.rsqrt(jnp.mean(jnp.square(xf), -1, keepdims=True) + EPS)
    return (y * w.astype(jnp.float32)).astype(x.dtype)


def dwconv_centred(x, w, b):
    T = x.shape[1]
    pad = CONV_W // 2
    xp = jnp.pad(x, ((0, 0), (pad, CONV_W - 1 - pad), (0, 0)))
    y = b
    for j in range(CONV_W):
        y = y + xp[:, j:j + T] * w[j]
    return y


def rope_tables(n):
    rows_n = n // GRID_W
    row = jnp.repeat(jnp.arange(rows_n), GRID_W).astype(jnp.float32)
    col = jnp.tile(jnp.arange(GRID_W), rows_n).astype(jnp.float32)
    inv = ROPE_THETA ** (-jnp.arange(0, ROT_HALF, 2, dtype=jnp.float32) / ROT_HALF)
    ang_r = row[:, None] * inv[None]
    ang_c = col[:, None] * inv[None]
    return (jnp.cos(ang_r), jnp.sin(ang_r), jnp.cos(ang_c), jnp.sin(ang_c))


def _rot(xh, cos, sin):
    h = xh.shape[-1] // 2
    x1, x2 = xh[..., :h], xh[..., h:]
    cos = cos[None, :, None, :]
    sin = sin[None, :, None, :]
    return jnp.concatenate([x1 * cos - x2 * sin, x1 * sin + x2 * cos], axis=-1)


def apply_rope_2d(x, rope):
    cr, sr, cc, sc = rope
    xf = x.astype(jnp.float32)
    y = jnp.concatenate([_rot(xf[..., :ROT_HALF], cr, sr),
                         _rot(xf[..., ROT_HALF:], cc, sc)], axis=-1)
    return y.astype(x.dtype)


def zero_state(b):
    return (jnp.zeros((b, NH_A, DV_A, DK_A), jnp.float32),
            jnp.zeros((b, NH_A, DK_A), jnp.float32),
            jnp.full((b, NH_A), M_INIT, jnp.float32))


def mlstm_chunked(q, k, v, log_i, log_f, state):
    B, T, H, _ = q.shape
    nc = T // CHUNK

    def to_chunks(a):
        a = a.reshape((B, nc, CHUNK, H) + a.shape[3:])
        return jnp.moveaxis(a, (1, 3), (0, 2))

    tril = jnp.tril(jnp.ones((CHUNK, CHUNK), bool))

    def step(carry, xs):
        C0, n0, m0 = carry
        qc, kc, vc, ic, fc = xs
        b = jnp.cumsum(fc, axis=-1)
        d = jnp.where(tril, b[..., :, None] - b[..., None, :] + ic[..., None, :], -jnp.inf)
        m_inter = b + m0[..., None]
        m = jnp.maximum(m_inter, d.max(-1))
        w = jnp.exp(d - m[..., None])
        a = jnp.exp(m_inter - m)
        s = jnp.einsum('bhjd,bhsd->bhjs', qc, kc) * w
        num = (a[..., None] * jnp.einsum('bhvd,bhjd->bhjv', C0, qc)
               + jnp.einsum('bhjs,bhsv->bhjv', s, vc))
        den = a * jnp.einsum('bhd,bhjd->bhj', n0, qc) + s.sum(-1)
        h = num / jnp.maximum(jnp.abs(den), jnp.exp(-m))[..., None]
        m_end = m[..., -1]
        w_end = jnp.exp(b[..., -1:] - b + ic - m_end[..., None])
        a_end = a[..., -1]
        C = a_end[..., None, None] * C0 + jnp.einsum('bhs,bhsv,bhsd->bhvd', w_end, vc, kc)
        n = a_end[..., None] * n0 + jnp.einsum('bhs,bhsd->bhd', w_end, kc)
        return (C, n, m_end), h

    xs = tuple(to_chunks(a) for a in (q, k, v, log_i, log_f))
    state, h = lax.scan(step, state, xs)
    h = jnp.moveaxis(h, (0, 2), (1, 3)).reshape(B, T, H, v.shape[-1])
    return h, state


def mlstm_final_state(k, v, log_i, log_f):
    b = jnp.cumsum(log_f, axis=1)
    g = b[:, -1:] - b + log_i
    m = g.max(axis=1)
    w = jnp.exp(g - m[:, None])
    C = jnp.einsum('bth,bthv,bthd->bhvd', w, v, k)
    n = jnp.einsum('bth,bthd->bhd', w, k)
    return (C, n, m)


def _flip(*arrs):
    return [jnp.flip(a, axis=1) for a in arrs]


def mlstm_inputs(qk_pre, v_a, if_a, conv_w, conv_b, b_if):
    B, T = v_a.shape[:2]
    qk = jax.nn.silu(dwconv_centred(qk_pre, conv_w, conv_b)).astype(jnp.float32)
    q = qk[..., :QK_A].reshape(B, T, NH_A, DK_A)
    k = qk[..., QK_A:].reshape(B, T, NH_A, DK_A) * (DK_A ** -0.5)
    v = v_a.astype(jnp.float32).reshape(B, T, NH_A, DV_A)
    gt = (if_a + b_if).astype(jnp.float32).reshape(B, T, 4, NH_A)
    fwd = (gt[:, :, 0], jax.nn.log_sigmoid(gt[:, :, 1]))
    bwd = (gt[:, :, 2], jax.nn.log_sigmoid(gt[:, :, 3]))
    return q, k, v, fwd, bwd


def attn_kv(k_b, v_b, k_norm_w, rope):
    B, T = k_b.shape[:2]
    k = rms_norm(k_b.reshape(B, T, NKV_B, HD_B), k_norm_w)
    if rope is not None:
        k = apply_rope_2d(k, rope)
    return k, v_b.reshape(B, T, NKV_B, HD_B)


def attend_blocks(q, k_all, v_all):
    B, S = q.shape[:2]
    G = NH_B // NKV_B
    nb = S // Q_BLOCK
    qb = q.reshape(B, nb, Q_BLOCK, NKV_B, G, HD_B).transpose(1, 0, 2, 3, 4, 5)
    scale = HD_B ** -0.5

    def one(qblk):
        s = jnp.einsum('bqkgd,btkd->bkgqt', qblk, k_all).astype(jnp.float32) * scale
        p = jax.nn.softmax(s, axis=-1).astype(v_all.dtype)
        return jnp.einsum('bkgqt,btkd->bqkgd', p, v_all)

    o = lax.map(one, qb)
    return o.transpose(1, 0, 2, 3, 4, 5).reshape(B, S, Q_B)


def branch_merge(h_a, o_attn, o_a, z_a, z_b, g_logits, mh_norm_w, w_ba, w_bb, w_out):
    B, T = h_a.shape[:2]
    h = rms_norm(h_a, mh_norm_w.reshape(NH_A, DV_A)).reshape(B, T, V_A)
    y_a = (jax.nn.sigmoid(o_a) * h * jax.nn.silu(z_a)) @ w_ba
    y_b = (o_attn * jax.nn.silu(z_b)) @ w_bb
    g_a, g_b = jnp.split(g_logits, 2, axis=-1)
    return (jax.nn.sigmoid(g_a) * y_a + jax.nn.sigmoid(g_b) * y_b) @ w_out


def trunk_layer(x, ctx, c, c_ctx, rope, w_mod, b_mod, w_in, b_if, conv_w, conv_b,
                mh_norm_w, q_norm_w, k_norm_w, w_ba, w_bb, w_out, ln_w, ln_b, update_ctx):
    B, S = x.shape[:2]
    T_c = ctx.shape[1]
    shift, scale, gate = [m[:, None, :] for m in jnp.split(jax.nn.silu(c) @ w_mod + b_mod, 3, axis=-1)]
    shift_c, scale_c, gate_c = jnp.split(jax.nn.silu(c_ctx) @ w_mod + b_mod, 3, axis=-1)
    u = layer_norm(x) * (1 + scale) + shift
    u_c = layer_norm(ctx) * (1 + scale_c) + shift_c
    p = u @ w_in
    p_c = u_c @ (w_in if update_ctx else w_in[:, :N_KV])

    qk_c, va_c, if_c, kb_c, vb_c = _split(p_c, KV_WIDTHS)
    qc, kc, vc, gf_c, gb_c = mlstm_inputs(qk_c, va_c, if_c, conv_w, conv_b, b_if)
    k_bc, v_bc = attn_kv(kb_c, vb_c, k_norm_w, None)
    if update_ctx:
        h_cf, st_f = mlstm_chunked(qc, kc, vc, *gf_c, zero_state(B))
        h_cb, st_b = mlstm_chunked(*_flip(qc, kc, vc, *gb_c), zero_state(B))
        h_c = (h_cf + jnp.flip(h_cb, axis=1)).astype(ctx.dtype)
        o_ac, z_ac, q_bc, z_bc, g_c = _split(p_c, OUT_WIDTHS, N_KV)
        q_bc = rms_norm(q_bc.reshape(B, T_c, NH_B, HD_B), q_norm_w)
        o_attn_c = attend_blocks(q_bc, k_bc, v_bc)
        out_c = branch_merge(h_c, o_attn_c, o_ac, z_ac, z_bc, g_c, mh_norm_w, w_ba, w_bb, w_out)
        ctx_new = layer_norm(ALPHA * ctx + gate_c * out_c, ln_w, ln_b)
    else:
        st_f = mlstm_final_state(kc, vc, *gf_c)
        st_b = mlstm_final_state(*_flip(kc, vc, *gb_c))
        ctx_new = ctx

    qk_l, va_l, if_l, kb_l, vb_l = _split(p, KV_WIDTHS)
    o_a, z_a, q_b, z_b, g_l = _split(p, OUT_WIDTHS, N_KV)
    ql, kl, vl, gf_l, gb_l = mlstm_inputs(qk_l, va_l, if_l, conv_w, conv_b, b_if)
    h_f, _ = mlstm_chunked(ql, kl, vl, *gf_l, st_f)
    h_b, _ = mlstm_chunked(*_flip(ql, kl, vl, *gb_l), st_b)
    h_l = (h_f + jnp.flip(h_b, axis=1)).astype(x.dtype)

    q_l = apply_rope_2d(rms_norm(q_b.reshape(B, S, NH_B, HD_B), q_norm_w), rope)
    k_bl, v_bl = attn_kv(kb_l, vb_l, k_norm_w, rope)
    k_all = jnp.concatenate([k_bc, k_bl], axis=1)
    v_all = jnp.concatenate([v_bc, v_bl], axis=1)
    o_attn = attend_blocks(q_l, k_all, v_all)

    out = branch_merge(h_l, o_attn, o_a, z_a, z_b, g_l, mh_norm_w, w_ba, w_bb, w_out)
    x_new = layer_norm(ALPHA * x + gate * out, ln_w, ln_b)
    return x_new, ctx_new


def _fwd_setup_inputs(seed: int = 0) -> dict:
    key = jax.random.key(seed)
    ks = jax.random.split(key, 20)
    D = D_MODEL
    nrm = jax.random.normal
    b_if_i = 0.1 * nrm(ks[8], (DEPTH, 2, 1, NH_A))
    b_if_f = 3.0 + 0.5 * nrm(ks[9], (DEPTH, 2, 1, NH_A))
    b_if = jnp.concatenate([b_if_i, b_if_f], axis=2).reshape(DEPTH, 4 * NH_A)
    return {
        "x": nrm(ks[0], (BATCH, SEQ, D), jnp.float32),
        "c": nrm(ks[1], (BATCH, D), jnp.float32),
        "ctx": nrm(ks[2], (BATCH, CTX_LEN, D), jnp.float32),
        "c_ctx": nrm(ks[3], (D,), jnp.float32),
        "w_mod": 0.5 * D ** -0.5 * nrm(ks[4], (DEPTH, D, 3 * D), jnp.float32),
        "b_mod": 0.01 * nrm(ks[5], (DEPTH, 3 * D), jnp.float32),
        "w_in": D ** -0.5 * nrm(ks[6], (DEPTH, D, N_IN), jnp.float32),
        "b_if": b_if.astype(jnp.float32),
        "conv_w": CONV_W ** -0.5 * nrm(ks[7], (DEPTH, CONV_W, 2 * QK_A), jnp.float32),
        "conv_b": 0.01 * nrm(ks[10], (DEPTH, 2 * QK_A), jnp.float32),
        "mh_norm_w": 1.0 + 0.02 * nrm(ks[11], (DEPTH, V_A), jnp.float32),
        "q_norm_w": 1.0 + 0.02 * nrm(ks[12], (DEPTH, HD_B), jnp.float32),
        "k_norm_w": 1.0 + 0.02 * nrm(ks[13], (DEPTH, HD_B), jnp.float32),
        "w_branch_a": BETA * V_A ** -0.5 * nrm(ks[14], (DEPTH, V_A, D), jnp.float32),
        "w_branch_b": BETA * Q_B ** -0.5 * nrm(ks[15], (DEPTH, Q_B, D), jnp.float32),
        "w_out": BETA * D ** -0.5 * nrm(ks[16], (DEPTH, D, D), jnp.float32),
        "ln_w": 1.0 + 0.02 * nrm(ks[17], (DEPTH, D), jnp.float32),
        "ln_b": 0.01 * nrm(ks[18], (DEPTH, D), jnp.float32),
    }


def _fwd_reference(x, c, ctx, c_ctx, w_mod, b_mod, w_in, b_if, conv_w, conv_b, mh_norm_w,
              q_norm_w, k_norm_w, w_branch_a, w_branch_b, w_out, ln_w, ln_b):
    rope = rope_tables(x.shape[1])
    for layer in range(DEPTH):
        x, ctx = trunk_layer(x, ctx, c, c_ctx, rope, w_mod[layer], b_mod[layer], w_in[layer],
                             b_if[layer], conv_w[layer], conv_b[layer], mh_norm_w[layer],
                             q_norm_w[layer], k_norm_w[layer], w_branch_a[layer],
                             w_branch_b[layer], w_out[layer], ln_w[layer], ln_b[layer],
                             layer < DEPTH - 1)
    return x


import jax as _jax
import jax.numpy as _jnp

TWIN_FORMAT = 'train_step'
FWD_PARAMS = ['x', 'c', 'ctx', 'c_ctx', 'w_mod', 'b_mod', 'w_in', 'b_if', 'conv_w', 'conv_b', 'mh_norm_w', 'q_norm_w', 'k_norm_w', 'w_branch_a', 'w_branch_b', 'w_out', 'ln_w', 'ln_b']
TWIN_WEIGHTS = ['c_ctx', 'w_mod', 'b_mod', 'w_in', 'b_if', 'conv_w', 'conv_b', 'mh_norm_w', 'q_norm_w', 'k_norm_w', 'w_branch_a', 'w_branch_b', 'w_out', 'ln_w', 'ln_b']
TWIN_DIFF_INPUT = 'x'
TWIN_INPUTS = ['x', 'c', 'ctx', 'c_ctx', 'w_mod', 'b_mod', 'w_in', 'b_if', 'conv_w', 'conv_b', 'mh_norm_w', 'q_norm_w', 'k_norm_w', 'w_branch_a', 'w_branch_b', 'w_out', 'ln_w', 'ln_b', 'loss_target', 'm_c_ctx', 'm_w_mod', 'm_b_mod', 'm_w_in', 'm_b_if', 'm_conv_w', 'm_conv_b', 'm_mh_norm_w', 'm_q_norm_w', 'm_k_norm_w', 'm_w_branch_a', 'm_w_branch_b', 'm_w_out', 'm_ln_w', 'm_ln_b', 'v_c_ctx', 'v_w_mod', 'v_b_mod', 'v_w_in', 'v_b_if', 'v_conv_w', 'v_conv_b', 'v_mh_norm_w', 'v_q_norm_w', 'v_k_norm_w', 'v_w_branch_a', 'v_w_branch_b', 'v_w_out', 'v_ln_w', 'v_ln_b']
TWIN_OUTPUTS = ['loss', 'grad_x', 'grad_c_ctx', 'grad_w_mod', 'grad_b_mod', 'grad_w_in', 'grad_b_if', 'grad_conv_w', 'grad_conv_b', 'grad_mh_norm_w', 'grad_q_norm_w', 'grad_k_norm_w', 'grad_w_branch_a', 'grad_w_branch_b', 'grad_w_out', 'grad_ln_w', 'grad_ln_b', 'delta_c_ctx', 'delta_w_mod', 'delta_b_mod', 'delta_w_in', 'delta_b_if', 'delta_conv_w', 'delta_conv_b', 'delta_mh_norm_w', 'delta_q_norm_w', 'delta_k_norm_w', 'delta_w_branch_a', 'delta_w_branch_b', 'delta_w_out', 'delta_ln_w', 'delta_ln_b', 'new_m_c_ctx', 'new_m_w_mod', 'new_m_b_mod', 'new_m_w_in', 'new_m_b_if', 'new_m_conv_w', 'new_m_conv_b', 'new_m_mh_norm_w', 'new_m_q_norm_w', 'new_m_k_norm_w', 'new_m_w_branch_a', 'new_m_w_branch_b', 'new_m_w_out', 'new_m_ln_w', 'new_m_ln_b', 'new_v_c_ctx', 'new_v_w_mod', 'new_v_b_mod', 'new_v_w_in', 'new_v_b_if', 'new_v_conv_w', 'new_v_conv_b', 'new_v_mh_norm_w', 'new_v_q_norm_w', 'new_v_k_norm_w', 'new_v_w_branch_a', 'new_v_w_branch_b', 'new_v_w_out', 'new_v_ln_w', 'new_v_ln_b']
TWIN_LEAF_KINDS = {'loss': 'loss', 'grad_x': 'grad_x', 'grad_c_ctx': 'grad_w', 'grad_w_mod': 'grad_w', 'grad_b_mod': 'grad_w', 'grad_w_in': 'grad_w', 'grad_b_if': 'grad_w', 'grad_conv_w': 'grad_w', 'grad_conv_b': 'grad_w', 'grad_mh_norm_w': 'grad_w', 'grad_q_norm_w': 'grad_w', 'grad_k_norm_w': 'grad_w', 'grad_w_branch_a': 'grad_w', 'grad_w_branch_b': 'grad_w', 'grad_w_out': 'grad_w', 'grad_ln_w': 'grad_w', 'grad_ln_b': 'grad_w', 'delta_c_ctx': 'delta_w', 'delta_w_mod': 'delta_w', 'delta_b_mod': 'delta_w', 'delta_w_in': 'delta_w', 'delta_b_if': 'delta_w', 'delta_conv_w': 'delta_w', 'delta_conv_b': 'delta_w', 'delta_mh_norm_w': 'delta_w', 'delta_q_norm_w': 'delta_w', 'delta_k_norm_w': 'delta_w', 'delta_w_branch_a': 'delta_w', 'delta_w_branch_b': 'delta_w', 'delta_w_out': 'delta_w', 'delta_ln_w': 'delta_w', 'delta_ln_b': 'delta_w', 'new_m_c_ctx': 'new_m', 'new_m_w_mod': 'new_m', 'new_m_b_mod': 'new_m', 'new_m_w_in': 'new_m', 'new_m_b_if': 'new_m', 'new_m_conv_w': 'new_m', 'new_m_conv_b': 'new_m', 'new_m_mh_norm_w': 'new_m', 'new_m_q_norm_w': 'new_m', 'new_m_k_norm_w': 'new_m', 'new_m_w_branch_a': 'new_m', 'new_m_w_branch_b': 'new_m', 'new_m_w_out': 'new_m', 'new_m_ln_w': 'new_m', 'new_m_ln_b': 'new_m', 'new_v_c_ctx': 'new_v', 'new_v_w_mod': 'new_v', 'new_v_b_mod': 'new_v', 'new_v_w_in': 'new_v', 'new_v_b_if': 'new_v', 'new_v_conv_w': 'new_v', 'new_v_conv_b': 'new_v', 'new_v_mh_norm_w': 'new_v', 'new_v_q_norm_w': 'new_v', 'new_v_k_norm_w': 'new_v', 'new_v_w_branch_a': 'new_v', 'new_v_w_branch_b': 'new_v', 'new_v_w_out': 'new_v', 'new_v_ln_w': 'new_v', 'new_v_ln_b': 'new_v'}


def _forward(args):
    return _fwd_reference(*[args[k] for k in FWD_PARAMS])


def _output_shape():
    def fwd():
        inp = _fwd_setup_inputs(0)
        return _fwd_reference(*[inp[k] for k in FWD_PARAMS])
    out = _jax.eval_shape(fwd)
    return out.shape, out.dtype

N_MICROBATCH = 1
ADAM_LR = 0.001
ADAM_B1 = 0.9
ADAM_B2 = 0.999
ADAM_EPS = 1e-08
ADAM_WD = 0.01
ADAM_STEP = 10
PER_EXAMPLE_BATCH_AXIS = {'x': 0, 'c': 0, 'ctx': 0, 'loss_target': 0}
SHARED_INPUTS = []
_WEIGHT_DTYPES = {'c_ctx': _jnp.float32, 'w_mod': _jnp.float32, 'b_mod': _jnp.float32, 'w_in': _jnp.float32, 'b_if': _jnp.float32, 'conv_w': _jnp.float32, 'conv_b': _jnp.float32, 'mh_norm_w': _jnp.float32, 'q_norm_w': _jnp.float32, 'k_norm_w': _jnp.float32, 'w_branch_a': _jnp.float32, 'w_branch_b': _jnp.float32, 'w_out': _jnp.float32, 'ln_w': _jnp.float32, 'ln_b': _jnp.float32}
MOMENT_SCALE = {'c_ctx': 6.895273e-04, 'w_mod': 4.183066e-03, 'b_mod': 7.195213e-03, 'w_in': 1.485221e-03, 'b_if': 8.873412e-03, 'conv_w': 1.702238e-03, 'conv_b': 1.593202e-03, 'mh_norm_w': 2.241169e-03, 'q_norm_w': 1.374475e-03, 'k_norm_w': 1.460393e-03, 'w_branch_a': 3.743327e-03, 'w_branch_b': 2.085408e-03, 'w_out': 4.267478e-03, 'ln_w': 3.196856e+01, 'ln_b': 2.592968e-01}


def _to_microbatches(a, axis):
    t = _jnp.moveaxis(a, axis, 0)
    t = t.reshape((N_MICROBATCH, t.shape[0] // N_MICROBATCH) + t.shape[1:])
    return _jnp.moveaxis(t, 1, axis + 1)


def setup_inputs(seed: int = 0) -> dict:
    inp = _fwd_setup_inputs(seed)
    key = _jax.random.fold_in(_jax.random.key(seed), 7919)
    shape, _ = _output_shape()
    out = dict(inp)
    out["loss_target"] = _jax.random.normal(_jax.random.fold_in(key, 0), shape, _jnp.float32)
    for i, name in enumerate(TWIN_WEIGHTS):
        w = inp[name].astype(_jnp.float32)
        if MOMENT_SCALE is None:
            s = _jnp.sqrt(_jnp.mean(_jnp.square(w)) + 1e-30)
        else:
            s = MOMENT_SCALE[name]
        km, kv = _jax.random.split(_jax.random.fold_in(key, i + 1))
        out[name] = w
        out["m_" + name] = s * _jax.random.normal(km, w.shape, _jnp.float32)
        out["v_" + name] = (s * s) * _jax.random.uniform(kv, w.shape, _jnp.float32, 0.5, 1.5)
    if N_MICROBATCH > 1:
        for name, axis in PER_EXAMPLE_BATCH_AXIS.items():
            out[name] = _to_microbatches(out[name], axis)
    return {'x': out['x'], 'c': out['c'], 'ctx': out['ctx'], 'c_ctx': out['c_ctx'], 'w_mod': out['w_mod'], 'b_mod': out['b_mod'], 'w_in': out['w_in'], 'b_if': out['b_if'], 'conv_w': out['conv_w'], 'conv_b': out['conv_b'], 'mh_norm_w': out['mh_norm_w'], 'q_norm_w': out['q_norm_w'], 'k_norm_w': out['k_norm_w'], 'w_branch_a': out['w_branch_a'], 'w_branch_b': out['w_branch_b'], 'w_out': out['w_out'], 'ln_w': out['ln_w'], 'ln_b': out['ln_b'], 'loss_target': out['loss_target'], 'm_c_ctx': out['m_c_ctx'], 'm_w_mod': out['m_w_mod'], 'm_b_mod': out['m_b_mod'], 'm_w_in': out['m_w_in'], 'm_b_if': out['m_b_if'], 'm_conv_w': out['m_conv_w'], 'm_conv_b': out['m_conv_b'], 'm_mh_norm_w': out['m_mh_norm_w'], 'm_q_norm_w': out['m_q_norm_w'], 'm_k_norm_w': out['m_k_norm_w'], 'm_w_branch_a': out['m_w_branch_a'], 'm_w_branch_b': out['m_w_branch_b'], 'm_w_out': out['m_w_out'], 'm_ln_w': out['m_ln_w'], 'm_ln_b': out['m_ln_b'], 'v_c_ctx': out['v_c_ctx'], 'v_w_mod': out['v_w_mod'], 'v_b_mod': out['v_b_mod'], 'v_w_in': out['v_w_in'], 'v_b_if': out['v_b_if'], 'v_conv_w': out['v_conv_w'], 'v_conv_b': out['v_conv_b'], 'v_mh_norm_w': out['v_mh_norm_w'], 'v_q_norm_w': out['v_q_norm_w'], 'v_k_norm_w': out['v_k_norm_w'], 'v_w_branch_a': out['v_w_branch_a'], 'v_w_branch_b': out['v_w_branch_b'], 'v_w_out': out['v_w_out'], 'v_ln_w': out['v_ln_w'], 'v_ln_b': out['v_ln_b']}


def _loss(weights, diff, rest, loss_target):
    with _jax.named_scope("forward"):
        args = {**rest, TWIN_DIFF_INPUT: diff, **{k: w.astype(_WEIGHT_DTYPES[k]) for k, w in weights.items()}}
        y = _forward(args)
    with _jax.named_scope("loss_head"):
        err = _jnp.square(y.astype(_jnp.float32) - loss_target)
        return 0.5 * _jnp.sum(_jnp.mean(err, axis=-1)) if err.ndim else 0.5 * err


def _adamw(w, g, m, v):
    m = ADAM_B1 * m + (1.0 - ADAM_B1) * g
    v = ADAM_B2 * v + (1.0 - ADAM_B2) * _jnp.square(g)
    m_hat = m / (1.0 - ADAM_B1 ** ADAM_STEP)
    v_hat = v / (1.0 - ADAM_B2 ** ADAM_STEP)
    delta = -ADAM_LR * (m_hat / (_jnp.sqrt(v_hat) + ADAM_EPS) + ADAM_WD * w)
    return delta, m, v


def reference(x, c, ctx, c_ctx, w_mod, b_mod, w_in, b_if, conv_w, conv_b, mh_norm_w, q_norm_w, k_norm_w, w_branch_a, w_branch_b, w_out, ln_w, ln_b, loss_target, m_c_ctx, m_w_mod, m_b_mod, m_w_in, m_b_if, m_conv_w, m_conv_b, m_mh_norm_w, m_q_norm_w, m_k_norm_w, m_w_branch_a, m_w_branch_b, m_w_out, m_ln_w, m_ln_b, v_c_ctx, v_w_mod, v_b_mod, v_w_in, v_b_if, v_conv_w, v_conv_b, v_mh_norm_w, v_q_norm_w, v_k_norm_w, v_w_branch_a, v_w_branch_b, v_w_out, v_ln_w, v_ln_b):
    given = dict(x=x, c=c, ctx=ctx, c_ctx=c_ctx, w_mod=w_mod, b_mod=b_mod, w_in=w_in, b_if=b_if, conv_w=conv_w, conv_b=conv_b, mh_norm_w=mh_norm_w, q_norm_w=q_norm_w, k_norm_w=k_norm_w, w_branch_a=w_branch_a, w_branch_b=w_branch_b, w_out=w_out, ln_w=ln_w, ln_b=ln_b, loss_target=loss_target, m_c_ctx=m_c_ctx, m_w_mod=m_w_mod, m_b_mod=m_b_mod, m_w_in=m_w_in, m_b_if=m_b_if, m_conv_w=m_conv_w, m_conv_b=m_conv_b, m_mh_norm_w=m_mh_norm_w, m_q_norm_w=m_q_norm_w, m_k_norm_w=m_k_norm_w, m_w_branch_a=m_w_branch_a, m_w_branch_b=m_w_branch_b, m_w_out=m_w_out, m_ln_w=m_ln_w, m_ln_b=m_ln_b, v_c_ctx=v_c_ctx, v_w_mod=v_w_mod, v_b_mod=v_b_mod, v_w_in=v_w_in, v_b_if=v_b_if, v_conv_w=v_conv_w, v_conv_b=v_conv_b, v_mh_norm_w=v_mh_norm_w, v_q_norm_w=v_q_norm_w, v_k_norm_w=v_k_norm_w, v_w_branch_a=v_w_branch_a, v_w_branch_b=v_w_branch_b, v_w_out=v_w_out, v_ln_w=v_ln_w, v_ln_b=v_ln_b)
    weights = {n: given[n] for n in TWIN_WEIGHTS}
    shared = {n: given[n] for n in SHARED_INPUTS}
    per_example = {n: given[n] for n in ['x', 'c', 'ctx']}
    grad_fn = _jax.value_and_grad(_loss, argnums=(0, 1))

    def one_microbatch(ex, loss_target):
        ex = dict(ex)
        diff = ex.pop(TWIN_DIFF_INPUT)
        return grad_fn(weights, diff, {**shared, **ex}, loss_target)

    if N_MICROBATCH == 1:
        loss, (grad_w, grad_x) = one_microbatch(per_example, given["loss_target"])
    else:
        def body(carry, xs):
            loss_sum, grad_sum = carry
            l_k, (gw_k, gx_k) = one_microbatch(xs[0], xs[1])
            with _jax.named_scope("update"):
                return (loss_sum + l_k, _jax.tree.map(_jnp.add, grad_sum, gw_k)), gx_k

        init = (_jnp.zeros((), _jnp.float32), _jax.tree.map(_jnp.zeros_like, weights))
        (loss, grad_w), grad_x = _jax.lax.scan(body, init, (per_example, given["loss_target"]))
    with _jax.named_scope("update"):
        delta_w, new_m, new_v = {}, {}, {}
        for n in TWIN_WEIGHTS:
            delta_w[n], new_m[n], new_v[n] = _adamw(weights[n], grad_w[n], given["m_" + n], given["v_" + n])
    return (loss, grad_x, *[grad_w[n] for n in TWIN_WEIGHTS], *[delta_w[n] for n in TWIN_WEIGHTS],
            *[new_m[n] for n in TWIN_WEIGHTS], *[new_v[n] for n in TWIN_WEIGHTS])
```

```python
import jax
import jax.numpy as jnp
from jax import lax
from jax.experimental import pallas as pl
from jax.experimental.pallas import tpu as pltpu

F32 = jnp.float32
BF16 = jnp.bfloat16
MESH = pl.DeviceIdType.MESH
N_DEV = 8

GRID_W = 64
NH_A = 8
DK_A = 128
DV_A = 256
CONV_W = 3
CHUNK = 64
M_INIT = -1e30
NH_B = 16
NKV_B = 4
HD_B = 128
ROPE_THETA = 10000.0
EPS = 1e-6
DEPTH = 1
ALPHA = (2 * DEPTH) ** 0.25
ADAM_LR = 0.001
ADAM_B1 = 0.9
ADAM_B2 = 0.999
ADAM_EPS = 1e-08
ADAM_WD = 0.01
ADAM_STEP = 10

LANE = 128
SUBLANE = 8
VMEM_LIMIT = 56 << 20


def _tile(n, target, align):
    best = None
    t = align
    while t <= min(n, target):
        if n % t == 0:
            best = t
        t += align
    return best if best is not None else n


class Dims:
    def __init__(self, S, Tc, D):
        self.S, self.Tc, self.D = S, Tc, D
        self.T = S + Tc
        self.QK = NH_A * DK_A
        self.V = NH_A * DV_A
        self.QB = NH_B * HD_B
        self.KVB = NKV_B * HD_B
        self.G = NH_B // NKV_B
        self.NIF = 4 * NH_A
        self.IFP = 512 if self.KVB % 512 == 0 else LANE
        self.ref_widths = [2 * self.QK, self.V, self.NIF, self.KVB, self.KVB,
                           self.V, self.V, self.QB, self.QB, 2 * D]
        self.ref_names = ["qk", "va", "if", "kb", "vb", "oa", "za", "qb", "zb", "g"]
        self.N_IN = sum(self.ref_widths)
        self.order = ["g", "qk", "va", "oa", "za", "qb", "zb", "kb", "vb", "if"]
        w = dict(zip(self.ref_names, self.ref_widths))
        w["if"] = self.IFP
        self.w = w
        self.off = {}
        o = 0
        for n in self.order:
            assert o % w[n] == 0, (n, o, w[n])
            self.off[n] = o
            o += w[n]
        self.NP = o
        self.TB = min(256, Tc)
        assert Tc % self.TB == 0 and S % self.TB == 0 and self.TB % CHUNK == 0
        self.nctx = Tc // self.TB
        self.nlat = S // self.TB
        self.nblk = self.nctx + self.nlat
        self.PW = max(D, self.V, 2 * self.QK, 3 * LANE)


def _cparams(sem):
    return pltpu.CompilerParams(dimension_semantics=sem, vmem_limit_bytes=VMEM_LIMIT)


def _sigmoid(x):
    return 1.0 / (1.0 + jnp.exp(-x))


def _my_pos():
    return lax.axis_index("x"), lax.axis_index("y"), lax.axis_index("c")


def _all_gather(x, name, big):
    R, C = x.shape
    space = pl.ANY if big else pltpu.VMEM

    def body(x_ref, out_ref, send_sems, recv_sems, local_sem):
        px, py, pc = _my_pos()
        me, sibling = (px, py, pc), (px, py, 1 - pc)
        chips = [(1 - px, py), (px, 1 - py), (1 - px, 1 - py)]

        def slot(bx, by, bc):
            return out_ref.at[4 * bx + 2 * by + bc]

        def copy(k, block, to, src=None):
            return pltpu.make_async_remote_copy(
                src_ref=slot(*block) if src is None else src, dst_ref=slot(*block),
                send_sem=send_sems.at[k], recv_sem=recv_sems.at[k],
                device_id=to, device_id_type=MESH)

        mine = pltpu.make_async_copy(x_ref, slot(*me), local_sem)
        mine.start()
        first = [copy(0, me, sibling, src=x_ref)]
        first += [copy(1 + j, me, (*chip, pc), src=x_ref) for j, chip in enumerate(chips)]
        for cp in first:
            cp.start()
        passed = [copy(4 + j, (*chip, pc), sibling) for j, chip in enumerate(chips)]
        for j, chip in enumerate(chips):
            copy(1 + j, (*chip, pc), me).wait_recv()
            passed[j].start()
        copy(0, sibling, me).wait_recv()
        for j, chip in enumerate(chips):
            copy(4 + j, (*chip, 1 - pc), me).wait_recv()
        for cp in first + passed:
            cp.wait_send()
        mine.wait()

    return pl.pallas_call(
        body, name=name,
        out_shape=jax.ShapeDtypeStruct((N_DEV, R, C), x.dtype),
        in_specs=[pl.BlockSpec(memory_space=space)],
        out_specs=pl.BlockSpec(memory_space=space),
        scratch_shapes=[pltpu.SemaphoreType.DMA((7,)), pltpu.SemaphoreType.DMA((7,)),
                        pltpu.SemaphoreType.DMA],
    )(x)


def _all_to_all(g, name):
    _, R, C = g.shape

    def body(g_ref, out_ref, send_sems, recv_sems, local_sem):
        px, py, pc = _my_pos()
        me = 4 * px + 2 * py + pc
        mine = pltpu.make_async_copy(g_ref.at[me], out_ref.at[me], local_sem)
        mine.start()
        sends, recvs = [], []
        for r in range(1, N_DEV):
            dx, dy, dc = (r >> 2) & 1, (r >> 1) & 1, r & 1
            qx = px if dx == 0 else 1 - px
            qy = py if dy == 0 else 1 - py
            qc = pc if dc == 0 else 1 - pc
            peer = 4 * qx + 2 * qy + qc
            sends.append(pltpu.make_async_remote_copy(
                src_ref=g_ref.at[peer], dst_ref=out_ref.at[me],
                send_sem=send_sems.at[r - 1], recv_sem=recv_sems.at[r - 1],
                device_id=(qx, qy, qc), device_id_type=MESH))
            recvs.append(pltpu.make_async_remote_copy(
                src_ref=g_ref.at[me], dst_ref=out_ref.at[peer],
                send_sem=send_sems.at[r - 1], recv_sem=recv_sems.at[r - 1],
                device_id=(qx, qy, qc), device_id_type=MESH))
        for cp in sends:
            cp.start()
        for cp in recvs:
            cp.wait_recv()
        for cp in sends:
            cp.wait_send()
        mine.wait()

    return pl.pallas_call(
        body, name=name,
        out_shape=jax.ShapeDtypeStruct(g.shape, g.dtype),
        in_specs=[pl.BlockSpec(memory_space=pl.ANY)],
        out_specs=pl.BlockSpec(memory_space=pl.ANY),
        scratch_shapes=[pltpu.SemaphoreType.DMA((7,)), pltpu.SemaphoreType.DMA((7,)),
                        pltpu.SemaphoreType.DMA],
    )(g)


def _sum_slots(a, name):
    _, R, C = a.shape
    tb = _tile(R, max(SUBLANE, (1 << 20) // (4 * C) // SUBLANE * SUBLANE), SUBLANE)

    def kern(a_ref, o_ref):
        acc = a_ref[0].astype(F32)
        for j in range(1, N_DEV):
            acc = acc + a_ref[j].astype(F32)
        o_ref[...] = acc

    return pl.pallas_call(
        kern, name=name, grid=(R // tb,),
        in_specs=[pl.BlockSpec((N_DEV, tb, C), lambda i: (0, i, 0))],
        out_specs=pl.BlockSpec((tb, C), lambda i: (i, 0)),
        out_shape=jax.ShapeDtypeStruct((R, C), F32),
        compiler_params=_cparams(("parallel",)),
    )(a)


def _mm(a, b, name, nt=False, tm=768, tn=1024, tk=2048, out_dtype=F32):
    M, K = a.shape
    N = b.shape[0] if nt else b.shape[1]
    assert (b.shape[1] if nt else b.shape[0]) == K
    tm, tn, tk = _tile(M, tm, 16), _tile(N, tn, LANE), _tile(K, tk, LANE)
    nk = K // tk

    def dot(x, y):
        if nt:
            return lax.dot_general(x, y, (((1,), (1,)), ((), ())), preferred_element_type=F32)
        return jnp.dot(x, y, preferred_element_type=F32)

    def kern(a_ref, b_ref, o_ref, acc_ref):
        k = pl.program_id(2)
        part = dot(a_ref[...], b_ref[...])
        if nk == 1:
            o_ref[...] = part.astype(o_ref.dtype)
        else:
            @pl.when(k == 0)
            def _():
                acc_ref[...] = part

            @pl.when(k > 0)
            def _():
                acc_ref[...] += part

            @pl.when(k == nk - 1)
            def _():
                o_ref[...] = acc_ref[...].astype(o_ref.dtype)

    b_spec = (pl.BlockSpec((tn, tk), lambda i, j, k: (j, k)) if nt
              else pl.BlockSpec((tk, tn), lambda i, j, k: (k, j)))
    return pl.pallas_call(
        kern, name=name, grid=(M // tm, N // tn, nk),
        in_specs=[pl.BlockSpec((tm, tk), lambda i, j, k: (i, k)), b_spec],
        out_specs=pl.BlockSpec((tm, tn), lambda i, j, k: (i, j)),
        out_shape=jax.ShapeDtypeStruct((M, N), out_dtype),
        scratch_shapes=[pltpu.VMEM((tm, tn) if nk > 1 else (SUBLANE, LANE), F32)],
        compiler_params=_cparams(("parallel", "parallel", "arbitrary")),
    )(a, b)


def _ln_stats(x):
    mu = jnp.mean(x, axis=-1, keepdims=True)
    xc = x - mu
    var = jnp.mean(xc * xc, axis=-1, keepdims=True)
    rstd = lax.rsqrt(var + EPS)
    return xc * rstd, rstd


def _ln_mod_fwd(dm, xcat, modv):
    TB, D, nctx = dm.TB, dm.D, dm.nctx

    def kern(x_ref, m_ref, u_ref):
        is_ctx = pl.program_id(0) < nctx
        xh, _ = _ln_stats(x_ref[...])
        shift = jnp.where(is_ctx, m_ref[2:3, :], m_ref[0:1, :])
        scale = jnp.where(is_ctx, m_ref[3:4, :], m_ref[1:2, :])
        u_ref[...] = (xh * (1.0 + scale) + shift).astype(BF16)

    return pl.pallas_call(
        kern, name="ln_mod_fwd", grid=(dm.nblk,),
        in_specs=[pl.BlockSpec((TB, D), lambda i: (i, 0)), pl.BlockSpec((SUBLANE, D), lambda i: (0, 0))],
        out_specs=pl.BlockSpec((TB, D), lambda i: (i, 0)),
        out_shape=jax.ShapeDtypeStruct((dm.T, D), BF16),
        compiler_params=_cparams(("parallel",)),
    )(xcat, modv)


def _conv_specs(dm, W, col):
    TB, T = dm.TB, dm.T
    r8 = TB // SUBLANE
    last8 = T // SUBLANE - 1
    return [pl.BlockSpec((TB, W), lambda i: (i, col)),
            pl.BlockSpec((SUBLANE, W), lambda i: (jnp.maximum(i * r8 - 1, 0), col)),
            pl.BlockSpec((SUBLANE, W), lambda i: (jnp.minimum((i + 1) * r8, last8), col))]


def _shifted(dm, x, prev_ref, next_ref):
    TB, nctx, nblk = dm.TB, dm.nctx, dm.nblk
    i = pl.program_id(0)
    row = lax.broadcasted_iota(jnp.int32, x.shape, 0)
    zero_prev = (i == 0) | (i == nctx)
    zero_next = (i == nctx - 1) | (i == nblk - 1)
    before = jnp.where(zero_prev, 0.0, prev_ref[SUBLANE - 1:SUBLANE, :])
    after = jnp.where(zero_next, 0.0, next_ref[0:1, :])
    xp = jnp.where(row == 0, before, pltpu.roll(x, 1, 0))
    xn = jnp.where(row == TB - 1, after, pltpu.roll(x, TB - 1, 0))
    return xp, xn


def _conv_silu_fwd(dm, P, convp):
    W = 2 * dm.QK
    col = dm.off["qk"] // W
    kscale = DK_A ** -0.5

    def kern(x_ref, p_ref, n_ref, c_ref, o_ref):
        x = x_ref[...]
        xp, xn = _shifted(dm, x, p_ref, n_ref)
        z = c_ref[3:4, :] + xp * c_ref[0:1, :] + x * c_ref[1:2, :] + xn * c_ref[2:3, :]
        lane = lax.broadcasted_iota(jnp.int32, (1, W), 1)
        cs = jnp.where(lane >= dm.QK, kscale, 1.0)
        o_ref[...] = z * _sigmoid(z) * cs

    return pl.pallas_call(
        kern, name="conv_silu_fwd", grid=(dm.nblk,),
        in_specs=_conv_specs(dm, W, col) + [pl.BlockSpec((SUBLANE, W), lambda i: (0, 0))],
        out_specs=pl.BlockSpec((dm.TB, W), lambda i: (i, 0)),
        out_shape=jax.ShapeDtypeStruct((dm.T, W), F32),
        compiler_params=_cparams(("parallel",)),
    )(P, P, P, convp)


def _gates_fwd(dm, P, bif):
    col = dm.off["if"] // LANE

    def kern(x_ref, b_ref, o_ref):
        x = x_ref[...] + b_ref[0:1, :]
        lane = lax.broadcasted_iota(jnp.int32, x.shape, 1)
        is_f = ((lane // NH_A) % 2) == 1
        ls = jnp.minimum(x, 0.0) - jnp.log(1.0 + jnp.exp(-jnp.abs(x)))
        o_ref[...] = jnp.where(lane < dm.NIF, jnp.where(is_f, ls, x), 0.0)

    return pl.pallas_call(
        kern, name="gates_fwd", grid=(dm.nblk,),
        in_specs=[pl.BlockSpec((dm.TB, LANE), lambda i: (i, col)),
                  pl.BlockSpec((SUBLANE, LANE), lambda i: (0, 0))],
        out_specs=pl.BlockSpec((dm.TB, LANE), lambda i: (i, 0)),
        out_shape=jax.ShapeDtypeStruct((dm.T, LANE), F32),
        compiler_params=_cparams(("parallel",)),
    )(P, bif)


def _mlstm_order(dm, reverse, backward):
    nctx, nblk = dm.nctx, dm.nblk

    def idx(i):
        if backward:
            i = nblk - 1 - i
        if not reverse:
            return i
        return jnp.where(i < nctx, nctx - 1 - i, nblk - 1 - (i - nctx))

    return idx


def _chunk_gates(g, ci, cf, mask_f, maskT_f, eye_f):
    lane = lax.broadcasted_iota(jnp.int32, g.shape, 1)
    gi_c = jnp.sum(jnp.where(lane == ci, g, 0.0), axis=1, keepdims=True)
    gf_c = jnp.sum(jnp.where(lane == cf, g, 0.0), axis=1, keepdims=True)
    gi_r = jnp.sum(eye_f * gi_c, axis=0, keepdims=True)
    gf_r = jnp.sum(eye_f * gf_c, axis=0, keepdims=True)
    b_c = jnp.sum(mask_f * gf_r, axis=1, keepdims=True)
    b_r = jnp.sum(maskT_f * gf_c, axis=0, keepdims=True)
    return gi_c, gi_r, b_c, b_r


def _chunk_masks(reverse):
    L = CHUNK
    r = lax.broadcasted_iota(jnp.int32, (L, L), 0)
    c = lax.broadcasted_iota(jnp.int32, (L, L), 1)
    mask = (c >= r) if reverse else (c <= r)
    maskT = (r >= c) if reverse else (r <= c)
    return mask, mask.astype(F32), maskT.astype(F32), (r == c).astype(F32)


def _pick_row(x, e):
    r = lax.broadcasted_iota(jnp.int32, x.shape, 0)
    return jnp.sum(jnp.where(r == e, x, 0.0), axis=0, keepdims=True)


def _dot_nt(a, b):
    return lax.dot_general(a, b, (((1,), (1,)), ((), ())), preferred_element_type=F32)


def _dot(a, b):
    return jnp.dot(a, b, preferred_element_type=F32)


def _chunk_fwd_core(q, k, g, ci, cf, C0, n0, m0, masks, reverse):
    mask, mask_f, maskT_f, eye_f = masks
    gi_c, gi_r, b_c, b_r = _chunk_gates(g, ci, cf, mask_f, maskT_f, eye_f)
    d = jnp.where(mask, b_c - b_r + gi_r, -jnp.inf)
    m_c = jnp.maximum(b_c + m0, jnp.max(d, axis=1, keepdims=True))
    w = jnp.exp(d - m_c)
    a_c = jnp.exp(b_c + m0 - m_c)
    qb, kb = q.astype(BF16), k.astype(BF16)
    s = _dot_nt(qb, kb) * w
    den = a_c * jnp.sum(q * n0, axis=1, keepdims=True) + jnp.sum(s, axis=1, keepdims=True)
    e = 0 if reverse else CHUNK - 1
    m_end, b_end, a_end = _pick_row(m_c, e), _pick_row(b_c, e), _pick_row(a_c, e)
    w_end = jnp.exp(b_end - b_c + gi_c - m_end)
    return qb, kb, s, w, a_c, m_c, den, w_end, a_end, m_end


def _mlstm_fwd(dm, qk_act, P, G, reverse):
    TB, T = dm.TB, dm.T
    NC = TB // CHUNK
    idx = _mlstm_order(dm, reverse, False)
    vcol = dm.off["va"] // DV_A
    base = 2 * NH_A if reverse else 0

    def kern(q_ref, k_ref, v_ref, g_ref, h_ref, cst_ref, nm_ref, C_s, N_s):
        i, h = pl.program_id(0), pl.program_id(1)
        row8 = lax.broadcasted_iota(jnp.int32, (SUBLANE, DK_A), 0)

        @pl.when(i == 0)
        def _():
            C_s[h] = jnp.zeros((DK_A, DV_A), F32)
            N_s[h] = jnp.where(row8 == 1, M_INIT, 0.0)

        masks = _chunk_masks(reverse)
        ci, cf = base + h, base + NH_A + h
        C0 = C_s[h]
        n0, m0 = N_s.at[h][0:1, :], N_s.at[h][1:2, 0:1]
        for c in (range(NC - 1, -1, -1) if reverse else range(NC)):
            rows = pl.ds(c * CHUNK, CHUNK)
            q, k, v, g = q_ref[rows, :], k_ref[rows, :], v_ref[rows, :], g_ref[rows, :]
            cst_ref[0, c] = C0
            nm_ref[0, c] = jnp.where(row8 == 0, n0, jnp.where(row8 == 1, m0, 0.0))
            qb, kb, s, w, a_c, m_c, den, w_end, a_end, m_end = _chunk_fwd_core(
                q, k, g, ci, cf, C0, n0, m0, masks, reverse)
            vb = v.astype(BF16)
            num = a_c * _dot(qb, C0.astype(BF16)) + _dot(s.astype(BF16), vb)
            h_ref[rows, :] = num / jnp.maximum(jnp.abs(den), jnp.exp(-m_c))
            C0 = a_end * C0 + _dot(k.T.astype(BF16), (w_end * v).astype(BF16))
            n0 = a_end * n0 + jnp.sum(w_end * k, axis=0, keepdims=True)
            m0 = m_end
        C_s[h] = C0
        N_s[h] = jnp.where(row8 == 0, n0, jnp.where(row8 == 1, m0, 0.0))

    nch = T // CHUNK
    return pl.pallas_call(
        kern, name="mlstm_fwd_rev" if reverse else "mlstm_fwd", grid=(dm.nblk, NH_A),
        in_specs=[pl.BlockSpec((TB, DK_A), lambda i, h: (idx(i), h)),
                  pl.BlockSpec((TB, DK_A), lambda i, h: (idx(i), NH_A + h)),
                  pl.BlockSpec((TB, DV_A), lambda i, h: (idx(i), vcol + h)),
                  pl.BlockSpec((TB, LANE), lambda i, h: (idx(i), 0))],
        out_specs=[pl.BlockSpec((TB, DV_A), lambda i, h: (idx(i), h)),
                   pl.BlockSpec((1, NC, DK_A, DV_A), lambda i, h: (h, idx(i), 0, 0)),
                   pl.BlockSpec((1, NC, SUBLANE, DK_A), lambda i, h: (h, idx(i), 0, 0))],
        out_shape=[jax.ShapeDtypeStruct((T, dm.V), F32),
                   jax.ShapeDtypeStruct((NH_A, nch, DK_A, DV_A), F32),
                   jax.ShapeDtypeStruct((NH_A, nch, SUBLANE, DK_A), F32)],
        scratch_shapes=[pltpu.VMEM((NH_A, DK_A, DV_A), F32), pltpu.VMEM((NH_A, SUBLANE, DK_A), F32)],
        compiler_params=_cparams(("arbitrary", "arbitrary")),
    )(qk_act, qk_act, P, G)


def _mlstm_bwd(dm, qk_act, P, G, Cst, NM, H, dH, acc, reverse):
    TB, T = dm.TB, dm.T
    NC = TB // CHUNK
    idx = _mlstm_order(dm, reverse, True)
    vcol = dm.off["va"] // DV_A
    base = 2 * NH_A if reverse else 0
    has_acc = acc is not None

    def kern(*refs):
        (q_ref, k_ref, v_ref, g_ref, cst_ref, nm_ref, hh_ref, dh_ref) = refs[:8]
        p = 8
        if has_acc:
            aq_ref, ak_ref, av_ref, ag_ref = refs[p:p + 4]
            p += 4
        dq_ref, dk_ref, dv_ref, dg_ref, R_s, Rn_s = refs[p:p + 6]
        i, h = pl.program_id(0), pl.program_id(1)
        row8 = lax.broadcasted_iota(jnp.int32, (SUBLANE, DK_A), 0)

        @pl.when(i == 0)
        def _():
            R_s[h] = jnp.zeros((DK_A, DV_A), F32)
            Rn_s[h] = jnp.zeros((SUBLANE, DK_A), F32)

        @pl.when(h == 0)
        def _():
            dg_ref[...] = ag_ref[...] if has_acc else jnp.zeros((TB, LANE), F32)

        masks = _chunk_masks(reverse)
        _, mask_f, maskT_f, eye_f = masks
        before_f = mask_f - eye_f
        ci, cf = base + h, base + NH_A + h
        R = R_s[h]
        Rn = Rn_s.at[h][0:1, :]
        lane = lax.broadcasted_iota(jnp.int32, (CHUNK, LANE), 1)

        def as_row(col):
            return jnp.sum(eye_f * col, axis=0, keepdims=True)
        for c in (range(NC) if reverse else range(NC - 1, -1, -1)):
            rows = pl.ds(c * CHUNK, CHUNK)
            q, k, v, g = q_ref[rows, :], k_ref[rows, :], v_ref[rows, :], g_ref[rows, :]
            C0 = cst_ref[0, c]
            n0, m0 = nm_ref.at[0, c][0:1, :], nm_ref.at[0, c][1:2, 0:1]
            qb, kb, s, w, a_c, m_c, den, w_end, a_end, _ = _chunk_fwd_core(
                q, k, g, ci, cf, C0, n0, m0, masks, reverse)
            vb = v.astype(BF16)
            e_m = jnp.exp(-m_c)
            r = 1.0 / jnp.maximum(jnp.abs(den), e_m)
            dh = dh_ref[rows, :]
            dN = dh * r
            dD = jnp.where(jnp.abs(den) > e_m,
                           -jnp.sum(dh * hh_ref[rows, :], axis=1, keepdims=True) * r * jnp.sign(den), 0.0)
            dNb = dN.astype(BF16)
            dS = _dot_nt(dNb, vb) + dD
            dqk = dS * w
            Cb, Rb = C0.astype(BF16), R.astype(BF16)
            dq_in = a_c * (_dot_nt(dNb, Cb) + dD * n0)
            dk_out = w_end * (_dot_nt(vb, Rb) + Rn)
            dq = _dot(dqk.astype(BF16), kb) + dq_in
            dk = _dot(dqk.T.astype(BF16), qb) + dk_out
            dv = _dot(s.T.astype(BF16), dNb) + w_end * _dot(kb, Rb)
            if has_acc:
                dq_ref[rows, :] = aq_ref[rows, :] + dq
                dk_ref[rows, :] = ak_ref[rows, :] + dk
                dv_ref[rows, :] = av_ref[rows, :] + dv
            else:
                dq_ref[rows, :] = dq
                dk_ref[rows, :] = dk
                dv_ref[rows, :] = dv
            gm = dS * s
            g_row = jnp.sum(gm, axis=1, keepdims=True)
            g_col = jnp.sum(eye_f * jnp.sum(gm, axis=0, keepdims=True), axis=1, keepdims=True)
            q_in = jnp.sum(q * dq_in, axis=1, keepdims=True)
            k_out = jnp.sum(k * dk_out, axis=1, keepdims=True)
            through = a_end * (jnp.sum(jnp.sum(R * C0, axis=1, keepdims=True), axis=0, keepdims=True)
                               + jnp.sum(Rn * n0, axis=1, keepdims=True))
            di = g_col + k_out
            df = (jnp.sum(maskT_f * as_row(g_row - g_col + q_in), axis=1, keepdims=True)
                  + jnp.sum(before_f * as_row(k_out), axis=1, keepdims=True) + through)
            dg_ref[rows, :] += jnp.where(lane == ci, di, 0.0) + jnp.where(lane == cf, df, 0.0)
            aq = a_c * q
            R = a_end * R + _dot(aq.T.astype(BF16), dNb)
            Rn = a_end * Rn + jnp.sum(aq * dD, axis=0, keepdims=True)
        R_s[h] = R
        Rn_s[h] = jnp.where(row8 == 0, Rn, 0.0)

    qspec = pl.BlockSpec((TB, DK_A), lambda i, h: (idx(i), h))
    vspec = pl.BlockSpec((TB, DV_A), lambda i, h: (idx(i), h))
    gspec = pl.BlockSpec((TB, LANE), lambda i, h: (idx(i), 0))
    in_specs = [qspec,
                pl.BlockSpec((TB, DK_A), lambda i, h: (idx(i), NH_A + h)),
                pl.BlockSpec((TB, DV_A), lambda i, h: (idx(i), vcol + h)),
                gspec,
                pl.BlockSpec((1, NC, DK_A, DV_A), lambda i, h: (h, idx(i), 0, 0)),
                pl.BlockSpec((1, NC, SUBLANE, DK_A), lambda i, h: (h, idx(i), 0, 0)),
                vspec, vspec]
    args = [qk_act, qk_act, P, G, Cst, NM, H, dH]
    if has_acc:
        in_specs += [qspec, qspec, vspec, gspec]
        args += list(acc)
    return pl.pallas_call(
        kern, name="mlstm_bwd_rev" if reverse else "mlstm_bwd", grid=(dm.nblk, NH_A),
        in_specs=in_specs,
        out_specs=[qspec, qspec, vspec, gspec],
        out_shape=[jax.ShapeDtypeStruct((T, dm.QK), F32), jax.ShapeDtypeStruct((T, dm.QK), F32),
                   jax.ShapeDtypeStruct((T, dm.V), F32), jax.ShapeDtypeStruct((T, LANE), F32)],
        scratch_shapes=[pltpu.VMEM((NH_A, DK_A, DV_A), F32), pltpu.VMEM((NH_A, SUBLANE, DK_A), F32)],
        compiler_params=_cparams(("arbitrary", "arbitrary")),
    )(*args)


def _rms_heads(x, w_row, nh, hd):
    out = []
    for h in range(nh):
        xh = x[:, h * hd:(h + 1) * hd]
        rstd = lax.rsqrt(jnp.mean(xh * xh, axis=1, keepdims=True) + EPS)
        out.append((xh * rstd, rstd))
    return out


def _rope(x, cos, sa, sb):
    return x * cos + pltpu.roll(x, HD_B - HD_B // 4, 1) * sa + pltpu.roll(x, HD_B // 4, 1) * sb


def _rope_t(dy, cos, sa, sb):
    return dy * cos + pltpu.roll(dy * sa, HD_B // 4, 1) + pltpu.roll(dy * sb, HD_B - HD_B // 4, 1)


def _qk_prep(dm, P, rope, qn, kn):
    TB, nctx = dm.TB, dm.nctx
    qcol, kcol, vcol = dm.off["qb"] // dm.QB, dm.off["kb"] // dm.KVB, dm.off["vb"] // dm.KVB

    def kern_q(x_ref, t_ref, w_ref, o_ref):
        cos, sa, sb = t_ref[0], t_ref[1], t_ref[2]
        for h, (xn, _) in enumerate(_rms_heads(x_ref[...], None, NH_B, HD_B)):
            o_ref[:, h * HD_B:(h + 1) * HD_B] = _rope(xn * w_ref[0:1, :], cos, sa, sb).astype(BF16)

    Qr = pl.pallas_call(
        kern_q, name="q_prep", grid=(dm.nlat,),
        in_specs=[pl.BlockSpec((TB, dm.QB), lambda i: (i + nctx, qcol)),
                  pl.BlockSpec((3, TB, HD_B), lambda i: (0, i + nctx, 0)),
                  pl.BlockSpec((SUBLANE, HD_B), lambda i: (0, 0))],
        out_specs=pl.BlockSpec((TB, dm.QB), lambda i: (i, 0)),
        out_shape=jax.ShapeDtypeStruct((dm.S, dm.QB), BF16),
        compiler_params=_cparams(("parallel",)),
    )(P, rope, qn)

    def kern_k(x_ref, v_ref, t_ref, w_ref, o_ref, vo_ref):
        cos, sa, sb = t_ref[0], t_ref[1], t_ref[2]
        for h, (xn, _) in enumerate(_rms_heads(x_ref[...], None, NKV_B, HD_B)):
            o_ref[:, h * HD_B:(h + 1) * HD_B] = _rope(xn * w_ref[0:1, :], cos, sa, sb).astype(BF16)
        vo_ref[...] = v_ref[...].astype(BF16)

    Kr, Vb = pl.pallas_call(
        kern_k, name="k_prep", grid=(dm.nblk,),
        in_specs=[pl.BlockSpec((TB, dm.KVB), lambda i: (i, kcol)),
                  pl.BlockSpec((TB, dm.KVB), lambda i: (i, vcol)),
                  pl.BlockSpec((3, TB, HD_B), lambda i: (0, i, 0)),
                  pl.BlockSpec((SUBLANE, HD_B), lambda i: (0, 0))],
        out_specs=[pl.BlockSpec((TB, dm.KVB), lambda i: (i, 0))] * 2,
        out_shape=[jax.ShapeDtypeStruct((dm.T, dm.KVB), BF16)] * 2,
        compiler_params=_cparams(("parallel",)),
    )(P, P, rope, kn)
    return Qr, Kr, Vb


def _attn_tiles(dm):
    return _tile(dm.S, 512, LANE), _tile(dm.T, 768, LANE)


def _attn_fwd(dm, Qr, Kr, Vb):
    S, T, G = dm.S, dm.T, dm.G
    tq, tk = _attn_tiles(dm)
    nk = T // tk
    scale = HD_B ** -0.5

    def kern(q_ref, k_ref, v_ref, o_ref, l_ref, m_s, l_s, a_s):
        j = pl.program_id(2)

        @pl.when(j == 0)
        def _():
            m_s[...] = jnp.full(m_s.shape, -jnp.inf, F32)
            l_s[...] = jnp.zeros(l_s.shape, F32)
            a_s[...] = jnp.zeros(a_s.shape, F32)

        k, v = k_ref[...], v_ref[...]
        for h in range(G):
            s = _dot_nt(q_ref[:, h * HD_B:(h + 1) * HD_B], k) * scale
            m_old = m_s[h]
            m_new = jnp.maximum(m_old, jnp.max(s, axis=1, keepdims=True))
            p = jnp.exp(s - m_new)
            corr = jnp.exp(m_old - m_new)
            l_s[h] = corr * l_s[h] + jnp.sum(p, axis=1, keepdims=True)
            a_s[h] = corr * a_s[h] + _dot(p.astype(BF16), v)
            m_s[h] = m_new

        @pl.when(j == nk - 1)
        def _():
            for h in range(G):
                o_ref[:, h * HD_B:(h + 1) * HD_B] = a_s[h] / l_s[h]
                l_ref[0, :, h:h + 1] = m_s[h] + jnp.log(l_s[h])

    return pl.pallas_call(
        kern, name="attn_fwd", grid=(NKV_B, S // tq, nk),
        in_specs=[pl.BlockSpec((tq, G * HD_B), lambda g, i, j: (i, g)),
                  pl.BlockSpec((tk, HD_B), lambda g, i, j: (j, g)),
                  pl.BlockSpec((tk, HD_B), lambda g, i, j: (j, g))],
        out_specs=[pl.BlockSpec((tq, G * HD_B), lambda g, i, j: (i, g)),
                   pl.BlockSpec((1, tq, G), lambda g, i, j: (g, i, 0))],
        out_shape=[jax.ShapeDtypeStruct((S, dm.QB), F32), jax.ShapeDtypeStruct((NKV_B, S, G), F32)],
        scratch_shapes=[pltpu.VMEM((G, tq, 1), F32), pltpu.VMEM((G, tq, 1), F32),
                        pltpu.VMEM((G, tq, HD_B), F32)],
        compiler_params=_cparams(("parallel", "parallel", "arbitrary")),
    )(Qr, Kr, Vb)


def _attn_bwd_dq(dm, Qr, Kr, Vb, dO, LSE, DEL):
    S, T, G = dm.S, dm.T, dm.G
    tq, tk = _attn_tiles(dm)
    nk = T // tk
    scale = HD_B ** -0.5

    def kern(q_ref, k_ref, v_ref, do_ref, l_ref, d_ref, dq_ref, a_s):
        j = pl.program_id(2)

        @pl.when(j == 0)
        def _():
            a_s[...] = jnp.zeros(a_s.shape, F32)

        k, v = k_ref[...], v_ref[...]
        for h in range(G):
            cols = slice(h * HD_B, (h + 1) * HD_B)
            s = _dot_nt(q_ref[:, cols], k) * scale
            p = jnp.exp(s - l_ref[0, :, h:h + 1])
            dp = _dot_nt(do_ref[:, cols], v)
            ds = p * (dp - d_ref[0, :, h:h + 1]) * scale
            a_s[h] += _dot(ds.astype(BF16), k)

        @pl.when(j == nk - 1)
        def _():
            for h in range(G):
                dq_ref[:, h * HD_B:(h + 1) * HD_B] = a_s[h]

    qspec = pl.BlockSpec((tq, G * HD_B), lambda g, i, j: (i, g))
    kspec = pl.BlockSpec((tk, HD_B), lambda g, i, j: (j, g))
    lspec = pl.BlockSpec((1, tq, G), lambda g, i, j: (g, i, 0))
    return pl.pallas_call(
        kern, name="attn_bwd_dq", grid=(NKV_B, S // tq, nk),
        in_specs=[qspec, kspec, kspec, qspec, lspec, lspec],
        out_specs=qspec,
        out_shape=jax.ShapeDtypeStruct((S, dm.QB), F32),
        scratch_shapes=[pltpu.VMEM((G, tq, HD_B), F32)],
        compiler_params=_cparams(("parallel", "parallel", "arbitrary")),
    )(Qr, Kr, Vb, dO, LSE, DEL)


def _attn_bwd_dkv(dm, Qr, Kr, Vb, dO, LSE_T, DEL_T):
    S, T, G = dm.S, dm.T, dm.G
    tq, tk = _attn_tiles(dm)
    nq = S // tq
    scale = HD_B ** -0.5

    def kern(q_ref, k_ref, v_ref, do_ref, l_ref, d_ref, dk_ref, dv_ref, ak_s, av_s):
        i = pl.program_id(2)

        @pl.when(i == 0)
        def _():
            ak_s[...] = jnp.zeros(ak_s.shape, F32)
            av_s[...] = jnp.zeros(av_s.shape, F32)

        k, v = k_ref[...], v_ref[...]
        for h in range(G):
            cols = slice(h * HD_B, (h + 1) * HD_B)
            q, do = q_ref[:, cols], do_ref[:, cols]
            sT = _dot_nt(k, q) * scale
            pT = jnp.exp(sT - l_ref[0, h:h + 1, :])
            av_s[...] += _dot(pT.astype(BF16), do)
            dpT = _dot_nt(v, do)
            dsT = pT * (dpT - d_ref[0, h:h + 1, :]) * scale
            ak_s[...] += _dot(dsT.astype(BF16), q)

        @pl.when(i == nq - 1)
        def _():
            dk_ref[...] = ak_s[...]
            dv_ref[...] = av_s[...].astype(BF16)

    qspec = pl.BlockSpec((tq, G * HD_B), lambda g, j, i: (i, g))
    kspec = pl.BlockSpec((tk, HD_B), lambda g, j, i: (j, g))
    lspec = pl.BlockSpec((1, G, tq), lambda g, j, i: (g, 0, i))
    return pl.pallas_call(
        kern, name="attn_bwd_dkv", grid=(NKV_B, T // tk, nq),
        in_specs=[qspec, kspec, kspec, qspec, lspec, lspec],
        out_specs=[kspec, kspec],
        out_shape=[jax.ShapeDtypeStruct((T, dm.KVB), F32), jax.ShapeDtypeStruct((T, dm.KVB), BF16)],
        scratch_shapes=[pltpu.VMEM((tk, HD_B), F32), pltpu.VMEM((tk, HD_B), F32)],
        compiler_params=_cparams(("parallel", "parallel", "arbitrary")),
    )(Qr, Kr, Vb, dO, LSE_T, DEL_T)


def _qk_bwd(dm, dQr, dKr, P, rope, qn, kn):
    TB, nctx = dm.TB, dm.nctx
    qcol, kcol = dm.off["qb"] // dm.QB, dm.off["kb"] // dm.KVB

    def head_bwd(dyr, x, w_row, cos, sa, sb):
        rstd = lax.rsqrt(jnp.mean(x * x, axis=1, keepdims=True) + EPS)
        xn = x * rstd
        dy = _rope_t(dyr, cos, sa, sb)
        dw = jnp.sum(dy * xn, axis=0, keepdims=True)
        dxn = dy * w_row
        dx = rstd * (dxn - xn * jnp.mean(dxn * xn, axis=1, keepdims=True))
        return dx, dw

    def make(nh, ctx_zero):
        def kern(d_ref, x_ref, t_ref, w_ref, o_ref, gw_ref):
            i = pl.program_id(0)

            @pl.when(i == 0)
            def _():
                gw_ref[...] = jnp.zeros(gw_ref.shape, F32)

            def live():
                cos, sa, sb = t_ref[0], t_ref[1], t_ref[2]
                tot = jnp.zeros((1, HD_B), F32)
                for h in range(nh):
                    cols = slice(h * HD_B, (h + 1) * HD_B)
                    dx, dw = head_bwd(d_ref[:, cols], x_ref[:, cols], w_ref[0:1, :], cos, sa, sb)
                    o_ref[:, cols] = dx.astype(BF16)
                    tot = tot + dw
                gw_ref[0:1, :] += tot

            if ctx_zero:
                @pl.when(i < nctx)
                def _():
                    o_ref[...] = jnp.zeros(o_ref.shape, BF16)

                pl.when(i >= nctx)(live)
            else:
                live()
        return kern

    lat = lambda i: jnp.maximum(i - nctx, 0)
    d_qb, gqn = pl.pallas_call(
        make(NH_B, True), name="q_bwd", grid=(dm.nblk,),
        in_specs=[pl.BlockSpec((TB, dm.QB), lambda i: (lat(i), 0)),
                  pl.BlockSpec((TB, dm.QB), lambda i: (i, qcol)),
                  pl.BlockSpec((3, TB, HD_B), lambda i: (0, i, 0)),
                  pl.BlockSpec((SUBLANE, HD_B), lambda i: (0, 0))],
        out_specs=[pl.BlockSpec((TB, dm.QB), lambda i: (i, 0)), pl.BlockSpec((SUBLANE, HD_B), lambda i: (0, 0))],
        out_shape=[jax.ShapeDtypeStruct((dm.T, dm.QB), BF16), jax.ShapeDtypeStruct((SUBLANE, HD_B), F32)],
        compiler_params=_cparams(("arbitrary",)),
    )(dQr, P, rope, qn)
    d_kb, gkn = pl.pallas_call(
        make(NKV_B, False), name="k_bwd", grid=(dm.nblk,),
        in_specs=[pl.BlockSpec((TB, dm.KVB), lambda i: (i, 0)),
                  pl.BlockSpec((TB, dm.KVB), lambda i: (i, kcol)),
                  pl.BlockSpec((3, TB, HD_B), lambda i: (0, i, 0)),
                  pl.BlockSpec((SUBLANE, HD_B), lambda i: (0, 0))],
        out_specs=[pl.BlockSpec((TB, dm.KVB), lambda i: (i, 0)), pl.BlockSpec((SUBLANE, HD_B), lambda i: (0, 0))],
        out_shape=[jax.ShapeDtypeStruct((dm.T, dm.KVB), BF16), jax.ShapeDtypeStruct((SUBLANE, HD_B), F32)],
        compiler_params=_cparams(("arbitrary",)),
    )(dKr, P, rope, kn)
    return d_qb, d_kb, gqn, gkn


def _merge_prep(dm, Hf, Hb, P, O, mhw):
    TB, nctx = dm.TB, dm.nctx
    lat = lambda c: (lambda i: (i + nctx, c))

    def kern(hf_ref, hb_ref, oa_ref, za_ref, zb_ref, o_ref, w_ref, a_ref, b_ref):
        for h in range(NH_A):
            cols = slice(h * DV_A, (h + 1) * DV_A)
            hs = hf_ref[:, cols] + hb_ref[:, cols]
            rstd = lax.rsqrt(jnp.mean(hs * hs, axis=1, keepdims=True) + EPS)
            za = za_ref[:, cols]
            a_ref[:, cols] = (_sigmoid(oa_ref[:, cols]) * (hs * rstd * w_ref[0:1, cols])
                              * (za * _sigmoid(za))).astype(BF16)
        zb = zb_ref[...]
        b_ref[...] = (o_ref[...] * (zb * _sigmoid(zb))).astype(BF16)

    return pl.pallas_call(
        kern, name="merge_prep", grid=(dm.nlat,),
        in_specs=[pl.BlockSpec((TB, dm.V), lat(0)), pl.BlockSpec((TB, dm.V), lat(0)),
                  pl.BlockSpec((TB, dm.V), lat(dm.off["oa"] // dm.V)),
                  pl.BlockSpec((TB, dm.V), lat(dm.off["za"] // dm.V)),
                  pl.BlockSpec((TB, dm.QB), lat(dm.off["zb"] // dm.QB)),
                  pl.BlockSpec((TB, dm.QB), lambda i: (i, 0)),
                  pl.BlockSpec((SUBLANE, dm.V), lambda i: (0, 0))],
        out_specs=[pl.BlockSpec((TB, dm.V), lambda i: (i, 0)), pl.BlockSpec((TB, dm.QB), lambda i: (i, 0))],
        out_shape=[jax.ShapeDtypeStruct((dm.S, dm.V), BF16), jax.ShapeDtypeStruct((dm.S, dm.QB), BF16)],
        compiler_params=_cparams(("parallel",)),
    )(Hf, Hb, P, P, P, O, mhw)


def _gate_merge(dm, ya, yb, P):
    TB, D, nctx = dm.TB, dm.D, dm.nctx
    gcol = dm.off["g"] // D

    def kern(ya_ref, yb_ref, ga_ref, gb_ref, o_ref):
        o_ref[...] = (_sigmoid(ga_ref[...]) * ya_ref[...] + _sigmoid(gb_ref[...]) * yb_ref[...]).astype(BF16)

    row = pl.BlockSpec((TB, D), lambda i: (i, 0))
    return pl.pallas_call(
        kern, name="gate_merge", grid=(dm.nlat,),
        in_specs=[row, row, pl.BlockSpec((TB, D), lambda i: (i + nctx, gcol)),
                  pl.BlockSpec((TB, D), lambda i: (i + nctx, gcol + 1))],
        out_specs=row, out_shape=jax.ShapeDtypeStruct((dm.S, D), BF16),
        compiler_params=_cparams(("parallel",)),
    )(ya, yb, P, P)


def _final(dm, x, out, tgt, modv, lnp):
    TB, D = dm.TB, dm.D

    def kern(x_ref, o_ref, t_ref, m_ref, p_ref, dr_ref, do_ref, cs_ref, ls_ref):
        i = pl.program_id(0)

        @pl.when(i == 0)
        def _():
            cs_ref[...] = jnp.zeros(cs_ref.shape, F32)
            ls_ref[...] = jnp.zeros(ls_ref.shape, F32)

        gate, lnw, lnb = m_ref[4:5, :], p_ref[0:1, :], p_ref[1:2, :]
        out = o_ref[...]
        xh, rstd = _ln_stats(ALPHA * x_ref[...] + gate * out)
        e = xh * lnw + lnb - t_ref[...]
        ls_ref[...] += 0.5 * jnp.sum(jnp.sum(e * e, axis=1, keepdims=True), axis=0, keepdims=True) / D
        dy = e * (1.0 / D)
        dxh = dy * lnw
        dr = rstd * (dxh - jnp.mean(dxh, axis=1, keepdims=True)
                     - xh * jnp.mean(dxh * xh, axis=1, keepdims=True))
        cs_ref[0:1, :] += jnp.sum(dy * xh, axis=0, keepdims=True)
        cs_ref[1:2, :] += jnp.sum(dy, axis=0, keepdims=True)
        cs_ref[2:3, :] += jnp.sum(dr * out, axis=0, keepdims=True)
        dr_ref[...] = ALPHA * dr
        do_ref[...] = (dr * gate).astype(BF16)

    row = pl.BlockSpec((TB, D), lambda i: (i, 0))
    par = pl.BlockSpec((SUBLANE, D), lambda i: (0, 0))
    return pl.pallas_call(
        kern, name="final_norm_loss", grid=(dm.nlat,),
        in_specs=[row, row, row, par, par],
        out_specs=[row, row, par, pl.BlockSpec((SUBLANE, LANE), lambda i: (0, 0))],
        out_shape=[jax.ShapeDtypeStruct((dm.S, D), F32), jax.ShapeDtypeStruct((dm.S, D), BF16),
                   jax.ShapeDtypeStruct((SUBLANE, D), F32), jax.ShapeDtypeStruct((SUBLANE, LANE), F32)],
        compiler_params=_cparams(("arbitrary",)),
    )(x, out, tgt, modv, lnp)


def _merge_bwd(dm, dM, ya, yb, P):
    TB, D, nctx = dm.TB, dm.D, dm.nctx
    gcol = dm.off["g"] // D
    lat = lambda i: jnp.maximum(i - nctx, 0)

    def kern(dm_ref, ya_ref, yb_ref, ga_ref, gb_ref, da_ref, db_ref, dg_ref):
        i = pl.program_id(0)

        @pl.when(i < nctx)
        def _():
            dg_ref[...] = jnp.zeros(dg_ref.shape, BF16)

        @pl.when(i >= nctx)
        def _():
            d = dm_ref[...]
            sa, sb = _sigmoid(ga_ref[...]), _sigmoid(gb_ref[...])
            da_ref[...] = (d * sa).astype(BF16)
            db_ref[...] = (d * sb).astype(BF16)
            dg_ref[:, 0:D] = (d * ya_ref[...] * sa * (1.0 - sa)).astype(BF16)
            dg_ref[:, D:2 * D] = (d * yb_ref[...] * sb * (1.0 - sb)).astype(BF16)

    row = pl.BlockSpec((TB, D), lambda i: (lat(i), 0))
    return pl.pallas_call(
        kern, name="merge_bwd", grid=(dm.nblk,),
        in_specs=[row, row, row, pl.BlockSpec((TB, D), lambda i: (i, gcol)),
                  pl.BlockSpec((TB, D), lambda i: (i, gcol + 1))],
        out_specs=[row, row, pl.BlockSpec((TB, 2 * D), lambda i: (i, 0))],
        out_shape=[jax.ShapeDtypeStruct((dm.S, D), BF16), jax.ShapeDtypeStruct((dm.S, D), BF16),
                   jax.ShapeDtypeStruct((dm.T, 2 * D), BF16)],
        compiler_params=_cparams(("arbitrary",)),
    )(dM, ya, yb, P, P)


def _branch_bwd(dm, dA, dB, Hf, Hb, P, O, mhw):
    TB, nctx, G = dm.TB, dm.nctx, dm.G
    lat = lambda i: jnp.maximum(i - nctx, 0)

    def kern(da_ref, db_ref, hf_ref, hb_ref, oa_ref, za_ref, zb_ref, o_ref, w_ref,
             doa_ref, dza_ref, dzb_ref, dh_ref, do_ref, del_ref, gw_ref):
        i = pl.program_id(0)

        @pl.when(i == 0)
        def _():
            gw_ref[...] = jnp.zeros(gw_ref.shape, F32)

        @pl.when(i < nctx)
        def _():
            doa_ref[...] = jnp.zeros(doa_ref.shape, BF16)
            dza_ref[...] = jnp.zeros(dza_ref.shape, BF16)
            dzb_ref[...] = jnp.zeros(dzb_ref.shape, BF16)
            dh_ref[...] = jnp.zeros(dh_ref.shape, F32)

        @pl.when(i >= nctx)
        def _():
            for h in range(NH_A):
                cols = slice(h * DV_A, (h + 1) * DV_A)
                hs = hf_ref[:, cols] + hb_ref[:, cols]
                rstd = lax.rsqrt(jnp.mean(hs * hs, axis=1, keepdims=True) + EPS)
                xn = hs * rstd
                w = w_ref[0:1, cols]
                hn = xn * w
                so, za = _sigmoid(oa_ref[:, cols]), za_ref[:, cols]
                sz = _sigmoid(za)
                silu = za * sz
                da = da_ref[:, cols]
                doa_ref[:, cols] = (da * hn * silu * so * (1.0 - so)).astype(BF16)
                dza_ref[:, cols] = (da * hn * so * sz * (1.0 + za * (1.0 - sz))).astype(BF16)
                dhn = da * so * silu
                gw_ref[0:1, cols] += jnp.sum(dhn * xn, axis=0, keepdims=True)
                dxn = dhn * w
                dh_ref[:, cols] = rstd * (dxn - xn * jnp.mean(dxn * xn, axis=1, keepdims=True))
            zb = zb_ref[...]
            sz = _sigmoid(zb)
            db, o = db_ref[...], o_ref[...]
            do = db * (zb * sz)
            do_ref[...] = do.astype(BF16)
            dzb_ref[...] = (db * o * sz * (1.0 + zb * (1.0 - sz))).astype(BF16)
            prod = do * o
            for h in range(NH_B):
                del_ref[h // G, :, (h % G):(h % G) + 1] = jnp.sum(
                    prod[:, h * HD_B:(h + 1) * HD_B], axis=1, keepdims=True)

    vlat = pl.BlockSpec((TB, dm.V), lambda i: (lat(i), 0))
    qlat = pl.BlockSpec((TB, dm.QB), lambda i: (lat(i), 0))
    vrow = pl.BlockSpec((TB, dm.V), lambda i: (i, 0))
    qrow = pl.BlockSpec((TB, dm.QB), lambda i: (i, 0))
    pv = lambda n: pl.BlockSpec((TB, dm.V), lambda i: (i, dm.off[n] // dm.V))
    return pl.pallas_call(
        kern, name="branch_bwd", grid=(dm.nblk,),
        in_specs=[vlat, qlat, vrow, vrow, pv("oa"), pv("za"),
                  pl.BlockSpec((TB, dm.QB), lambda i: (i, dm.off["zb"] // dm.QB)), qlat,
                  pl.BlockSpec((SUBLANE, dm.V), lambda i: (0, 0))],
        out_specs=[vrow, vrow, qrow, vrow, qlat,
                   pl.BlockSpec((NKV_B, TB, G), lambda i: (0, lat(i), 0)),
                   pl.BlockSpec((SUBLANE, dm.V), lambda i: (0, 0))],
        out_shape=[jax.ShapeDtypeStruct((dm.T, dm.V), BF16), jax.ShapeDtypeStruct((dm.T, dm.V), BF16),
                   jax.ShapeDtypeStruct((dm.T, dm.QB), BF16), jax.ShapeDtypeStruct((dm.T, dm.V), F32),
                   jax.ShapeDtypeStruct((dm.S, dm.QB), BF16), jax.ShapeDtypeStruct((NKV_B, dm.S, G), F32),
                   jax.ShapeDtypeStruct((SUBLANE, dm.V), F32)],
        compiler_params=_cparams(("arbitrary",)),
    )(dA, dB, Hf, Hb, P, P, P, O, mhw)


def _conv_bwd(dm, dq, dk, P, convp):
    W = 2 * dm.QK
    col = dm.off["qk"] // W
    kscale = DK_A ** -0.5
    TB = dm.TB

    def kern1(dq_ref, dk_ref, x_ref, p_ref, n_ref, c_ref, dz_ref):
        x = x_ref[...]
        xp, xn = _shifted(dm, x, p_ref, n_ref)
        z = c_ref[3:4, :] + xp * c_ref[0:1, :] + x * c_ref[1:2, :] + xn * c_ref[2:3, :]
        sz = _sigmoid(z)
        dact = jnp.concatenate([dq_ref[...], dk_ref[...] * kscale], axis=1)
        dz_ref[...] = dact * sz * (1.0 + z * (1.0 - sz))

    half = pl.BlockSpec((TB, dm.QK), lambda i: (i, 0))
    par = pl.BlockSpec((SUBLANE, W), lambda i: (0, 0))
    dz = pl.pallas_call(
        kern1, name="conv_bwd_act", grid=(dm.nblk,),
        in_specs=[half, half] + _conv_specs(dm, W, col) + [par],
        out_specs=pl.BlockSpec((TB, W), lambda i: (i, 0)),
        out_shape=jax.ShapeDtypeStruct((dm.T, W), F32),
        compiler_params=_cparams(("parallel",)),
    )(dq, dk, P, P, P, convp)

    def kern2(z_ref, zp_ref, zn_ref, x_ref, p_ref, n_ref, c_ref, dx_ref, cs_ref):
        @pl.when(pl.program_id(0) == 0)
        def _():
            cs_ref[...] = jnp.zeros(cs_ref.shape, F32)

        dz = z_ref[...]
        dzp, dzn = _shifted(dm, dz, zp_ref, zn_ref)
        dx_ref[...] = (dzn * c_ref[0:1, :] + dz * c_ref[1:2, :] + dzp * c_ref[2:3, :]).astype(BF16)
        x = x_ref[...]
        xp, xn = _shifted(dm, x, p_ref, n_ref)
        cs_ref[0:1, :] += jnp.sum(dz * xp, axis=0, keepdims=True)
        cs_ref[1:2, :] += jnp.sum(dz * x, axis=0, keepdims=True)
        cs_ref[2:3, :] += jnp.sum(dz * xn, axis=0, keepdims=True)
        cs_ref[3:4, :] += jnp.sum(dz, axis=0, keepdims=True)

    return pl.pallas_call(
        kern2, name="conv_bwd_taps", grid=(dm.nblk,),
        in_specs=_conv_specs(dm, W, 0) + _conv_specs(dm, W, col) + [par],
        out_specs=[pl.BlockSpec((TB, W), lambda i: (i, 0)), par],
        out_shape=[jax.ShapeDtypeStruct((dm.T, W), BF16), jax.ShapeDtypeStruct((SUBLANE, W), F32)],
        compiler_params=_cparams(("arbitrary",)),
    )(dz, dz, dz, P, P, P, convp)


def _gates_bwd(dm, dG, G):
    TB = dm.TB

    def kern(d_ref, g_ref, o_ref, cs_ref):
        @pl.when(pl.program_id(0) == 0)
        def _():
            cs_ref[...] = jnp.zeros(cs_ref.shape, F32)

        lane = lax.broadcasted_iota(jnp.int32, (TB, LANE), 1)
        is_f = ((lane // NH_A) % 2) == 1
        d = d_ref[...]
        dpre = jnp.where(lane < dm.NIF, jnp.where(is_f, d * (1.0 - jnp.exp(g_ref[...])), d), 0.0)
        cs_ref[0:1, :] += jnp.sum(dpre, axis=0, keepdims=True)
        if dm.IFP > LANE:
            o_ref[:, LANE:] = jnp.zeros((TB, dm.IFP - LANE), BF16)
        o_ref[:, 0:LANE] = dpre.astype(BF16)

    row = pl.BlockSpec((TB, LANE), lambda i: (i, 0))
    return pl.pallas_call(
        kern, name="gates_bwd", grid=(dm.nblk,),
        in_specs=[row, row],
        out_specs=[pl.BlockSpec((TB, dm.IFP), lambda i: (i, 0)), pl.BlockSpec((SUBLANE, LANE), lambda i: (0, 0))],
        out_shape=[jax.ShapeDtypeStruct((dm.T, dm.IFP), BF16), jax.ShapeDtypeStruct((SUBLANE, LANE), F32)],
        compiler_params=_cparams(("arbitrary",)),
    )(dG, G)


def _ln_mod_bwd(dm, dU, xcat, modv, dr_a):
    TB, D, nctx = dm.TB, dm.D, dm.nctx
    lat = lambda i: jnp.maximum(i - nctx, 0)

    def kern(du_ref, x_ref, m_ref, dr_ref, gx_ref, cs_ref):
        i = pl.program_id(0)
        is_ctx = i < nctx

        @pl.when(i == 0)
        def _():
            cs_ref[...] = jnp.zeros(cs_ref.shape, F32)

        xh, rstd = _ln_stats(x_ref[...])
        du = du_ref[...]
        s_shift = jnp.sum(du, axis=0, keepdims=True)
        s_scale = jnp.sum(du * xh, axis=0, keepdims=True)
        cs_ref[0:1, :] += jnp.where(is_ctx, 0.0, s_shift)
        cs_ref[1:2, :] += jnp.where(is_ctx, 0.0, s_scale)
        cs_ref[2:3, :] += jnp.where(is_ctx, s_shift, 0.0)
        cs_ref[3:4, :] += jnp.where(is_ctx, s_scale, 0.0)

        @pl.when(i >= nctx)
        def _():
            dxh = du * (1.0 + m_ref[1:2, :])
            gx_ref[...] = dr_ref[...] + rstd * (dxh - jnp.mean(dxh, axis=1, keepdims=True)
                                                - xh * jnp.mean(dxh * xh, axis=1, keepdims=True))

    row = pl.BlockSpec((TB, D), lambda i: (i, 0))
    lrow = pl.BlockSpec((TB, D), lambda i: (lat(i), 0))
    par = pl.BlockSpec((SUBLANE, D), lambda i: (0, 0))
    return pl.pallas_call(
        kern, name="ln_mod_bwd", grid=(dm.nblk,),
        in_specs=[row, row, par, lrow],
        out_specs=[lrow, par],
        out_shape=[jax.ShapeDtypeStruct((dm.S, D), F32), jax.ShapeDtypeStruct((SUBLANE, D), F32)],
        compiler_params=_cparams(("arbitrary",)),
    )(dU, xcat, modv, dr_a)


def _mod_fwd(craw, w_loc, b_loc):
    R, D = craw.shape
    n = w_loc.shape[1]

    def kern(c_ref, w_ref, b_ref, o_ref):
        c = c_ref[...]
        o_ref[...] = _dot((c * _sigmoid(c)).astype(BF16), w_ref[...].astype(BF16)) + b_ref[0:1, :]

    return pl.pallas_call(
        kern, name="mod_fwd", out_shape=jax.ShapeDtypeStruct((R, n), F32),
        compiler_params=pltpu.CompilerParams(vmem_limit_bytes=VMEM_LIMIT),
    )(craw, w_loc, b_loc)


def _mod_bwd(crawT, dmod, w_loc):
    D, R = crawT.shape
    n = w_loc.shape[1]

    def kern(c_ref, d_ref, w_ref, gw_ref, dc_ref):
        c = c_ref[...]
        d = d_ref[...].astype(BF16)
        gw_ref[...] = _dot((c * _sigmoid(c)).astype(BF16), d)
        dc_ref[...] = _dot_nt(d, w_ref[...].astype(BF16))

    return pl.pallas_call(
        kern, name="mod_bwd",
        out_shape=[jax.ShapeDtypeStruct((D, n), F32), jax.ShapeDtypeStruct((R, D), F32)],
        compiler_params=pltpu.CompilerParams(vmem_limit_bytes=VMEM_LIMIT),
    )(crawT, dmod, w_loc)


def _cctx_grad(dsilu, c_ctx):
    def kern(p_ref, c_ref, o_ref):
        c = c_ref[...]
        sc = _sigmoid(c)
        o_ref[...] = p_ref[...] * (sc * (1.0 + c * (1.0 - sc)))

    return pl.pallas_call(kern, name="cctx_grad", out_shape=jax.ShapeDtypeStruct(c_ctx.shape, F32))(dsilu, c_ctx)


def _adamw(w, g, m, v, name):
    R, C = w.shape
    tb = _tile(R, max(SUBLANE, (1 << 19) // (4 * C) // SUBLANE * SUBLANE), SUBLANE)
    c1 = 1.0 / (1.0 - ADAM_B1 ** ADAM_STEP)
    c2 = 1.0 / (1.0 - ADAM_B2 ** ADAM_STEP)

    def kern(w_ref, g_ref, m_ref, v_ref, d_ref, nm_ref, nv_ref):
        g = g_ref[...]
        nm = ADAM_B1 * m_ref[...] + (1.0 - ADAM_B1) * g
        nv = ADAM_B2 * v_ref[...] + (1.0 - ADAM_B2) * (g * g)
        nm_ref[...] = nm
        nv_ref[...] = nv
        d_ref[...] = -ADAM_LR * ((nm * c1) / (jnp.sqrt(nv * c2) + ADAM_EPS) + ADAM_WD * w_ref[...])

    spec = pl.BlockSpec((tb, C), lambda i: (i, 0))
    return pl.pallas_call(
        kern, name=name, grid=(R // tb,), in_specs=[spec] * 4, out_specs=[spec] * 3,
        out_shape=[jax.ShapeDtypeStruct((R, C), F32)] * 3,
        compiler_params=_cparams(("parallel",)),
    )(w, g, m, v)


def _rope_tables(dm):
    half = HD_B // 2
    rows_n = dm.S // GRID_W
    row = jnp.repeat(jnp.arange(rows_n), GRID_W).astype(F32)
    col = jnp.tile(jnp.arange(GRID_W), rows_n).astype(F32)
    inv = ROPE_THETA ** (-jnp.arange(0, half, 2, dtype=F32) / half)
    ar, ac = row[:, None] * inv[None], col[:, None] * inv[None]
    cos = jnp.concatenate([jnp.cos(ar), jnp.cos(ar), jnp.cos(ac), jnp.cos(ac)], axis=1)
    zr = jnp.zeros_like(ar)
    sa = jnp.concatenate([-jnp.sin(ar), zr, -jnp.sin(ac), zr], axis=1)
    sb = jnp.concatenate([zr, jnp.sin(ar), zr, jnp.sin(ac)], axis=1)
    ctx = jnp.stack([jnp.ones((dm.Tc, HD_B), F32), jnp.zeros((dm.Tc, HD_B), F32), jnp.zeros((dm.Tc, HD_B), F32)])
    return jnp.concatenate([ctx, jnp.stack([cos, sa, sb])], axis=1)


def _rows8(*rows, width):
    out = [jnp.pad(r.reshape(-1).astype(F32), (0, width - r.size)) for r in rows]
    n = -(-len(out) // SUBLANE) * SUBLANE
    out += [jnp.zeros((width,), F32)] * (n - len(out))
    return jnp.stack(out)


def _to_padded(dm, w_full):
    cols, o = {}, 0
    for n, wd in zip(dm.ref_names, dm.ref_widths):
        cols[n] = w_full[:, o:o + wd]
        o += wd
    cols["if"] = jnp.pad(cols["if"], ((0, 0), (0, dm.IFP - dm.NIF)))
    return jnp.concatenate([cols[n] for n in dm.order], axis=1)


def _from_padded(dm, w_pad):
    return jnp.concatenate(
        [w_pad[:, dm.off[n]:dm.off[n] + wd] for n, wd in zip(dm.ref_names, dm.ref_widths)], axis=1)


def kernel(x, c, ctx, c_ctx, w_mod, b_mod, w_in, b_if, conv_w, conv_b, mh_norm_w, q_norm_w, k_norm_w, w_branch_a, w_branch_b, w_out, ln_w, ln_b, loss_target, m_c_ctx, m_w_mod, m_b_mod, m_w_in, m_b_if, m_conv_w, m_conv_b, m_mh_norm_w, m_q_norm_w, m_k_norm_w, m_w_branch_a, m_w_branch_b, m_w_out, m_ln_w, m_ln_b, v_c_ctx, v_w_mod, v_b_mod, v_w_in, v_b_if, v_conv_w, v_conv_b, v_mh_norm_w, v_q_norm_w, v_k_norm_w, v_w_branch_a, v_w_branch_b, v_w_out, v_ln_w, v_ln_b):
    S, D = x.shape[1], x.shape[2]
    Tc = ctx.shape[1]
    dm = Dims(S, Tc, D)
    T, QK2 = dm.T, 2 * dm.QK
    me = 4 * lax.axis_index("x") + 2 * lax.axis_index("y") + lax.axis_index("c")
    n_mod = w_mod.shape[2]
    n_in = w_in.shape[2]
    n_cv = conv_w.shape[2]
    rb = w_out.shape[1]

    pack0 = _all_gather(_rows8(c[0], conv_w[0, 0], conv_w[0, 1], conv_w[0, 2], width=D), "ag_cond", False)
    c_all = pack0[:, 0, :]
    conv_full = jnp.transpose(pack0[:, 1:4, :n_cv], (1, 0, 2)).reshape(CONV_W, QK2)
    convp = _rows8(conv_full[0], conv_full[1], conv_full[2], conv_b[0], width=QK2)

    w_in_all = _all_gather(w_in[0].astype(BF16), "ag_w_in", True)
    Wp = _to_padded(dm, jnp.transpose(w_in_all, (1, 0, 2)).reshape(D, dm.N_IN))
    wsq = jnp.concatenate([w_branch_a[0], w_branch_b[0], w_out[0]], axis=0).astype(BF16)
    wsq_all = _all_gather(wsq, "ag_w_sq", True)
    Wba = wsq_all[:, 0:rb, :].reshape(dm.V, D)
    Wbb = wsq_all[:, rb:2 * rb, :].reshape(dm.QB, D)
    Wout = wsq_all[:, 2 * rb:3 * rb, :].reshape(D, D)

    craw = _rows8(*[c_all[j] for j in range(N_DEV)], c_ctx, width=D)
    b_loc = _rows8(lax.dynamic_slice(b_mod[0], (me * n_mod,), (n_mod,)), width=n_mod)
    mod_all = _all_gather(_mod_fwd(craw, w_mod[0], b_loc), "ag_mod", False)
    mod_rows = jnp.transpose(mod_all, (1, 0, 2)).reshape(2 * SUBLANE, 3 * D)
    mod_me = lax.dynamic_slice(mod_rows, (me, 0), (1, 3 * D))[0]
    mod_cx = mod_rows[N_DEV]
    modv = _rows8(mod_me[0:D], mod_me[D:2 * D], mod_cx[0:D], mod_cx[D:2 * D], mod_me[2 * D:3 * D], width=D)

    xcat = jnp.concatenate([ctx[0], x[0]], axis=0)
    U = _ln_mod_fwd(dm, xcat, modv)
    P = _mm(U, Wp, "mm_in_proj", tn=896)
    qk_act = _conv_silu_fwd(dm, P, convp)
    G = _gates_fwd(dm, P, _rows8(b_if[0], width=LANE))
    Hf, Cf, NMf = _mlstm_fwd(dm, qk_act, P, G, False)
    Hb, Cb, NMb = _mlstm_fwd(dm, qk_act, P, G, True)
    rope = _rope_tables(dm)
    qn, kn = _rows8(q_norm_w[0], width=HD_B), _rows8(k_norm_w[0], width=HD_B)
    Qr, Kr, Vb = _qk_prep(dm, P, rope, qn, kn)
    O, LSE = _attn_fwd(dm, Qr, Kr, Vb)
    mhw = _rows8(mh_norm_w[0], width=dm.V)
    A_in, B_in = _merge_prep(dm, Hf, Hb, P, O, mhw)
    ya = _mm(A_in, Wba, "mm_branch_a")
    yb = _mm(B_in, Wbb, "mm_branch_b")
    M_in = _gate_merge(dm, ya, yb, P)
    out = _mm(M_in, Wout, "mm_out")
    lnp = _rows8(ln_w[0], ln_b[0], width=D)
    dr_a, d_out, cs_fin, loss_p = _final(dm, x[0], out, loss_target[0], modv, lnp)
    loss = lax.psum(loss_p[0, 0], ("x", "y", "c"))

    dM = _mm(d_out, Wout, "mm_d_merge", nt=True)
    gWout = _mm(M_in.T, d_out, "mm_g_w_out", tk=2048)
    d_ya, d_yb, d_g = _merge_bwd(dm, dM, ya, yb, P)
    dA = _mm(d_ya, Wba, "mm_d_a", nt=True)
    gWba = _mm(A_in.T, d_ya, "mm_g_w_ba", tk=2048)
    dB = _mm(d_yb, Wbb, "mm_d_b", nt=True)
    gWbb = _mm(B_in.T, d_yb, "mm_g_w_bb", tk=2048)
    d_oa, d_za, d_zb, dH, dO, DEL, gmh = _branch_bwd(dm, dA, dB, Hf, Hb, P, O, mhw)
    dQr = _attn_bwd_dq(dm, Qr, Kr, Vb, dO, LSE, DEL)
    dKr, d_vb = _attn_bwd_dkv(dm, Qr, Kr, Vb, dO, jnp.transpose(LSE, (0, 2, 1)), jnp.transpose(DEL, (0, 2, 1)))
    d_qb, d_kb, gqn, gkn = _qk_bwd(dm, dQr, dKr, P, rope, qn, kn)
    acc = _mlstm_bwd(dm, qk_act, P, G, Cf, NMf, Hf, dH, None, False)
    dq, dk, dv, dG = _mlstm_bwd(dm, qk_act, P, G, Cb, NMb, Hb, dH, acc, True)
    d_qk, cs_conv = _conv_bwd(dm, dq, dk, P, convp)
    d_if, gbif = _gates_bwd(dm, dG, G)
    parts = {"g": d_g, "qk": d_qk, "va": dv.astype(BF16), "oa": d_oa, "za": d_za, "qb": d_qb,
             "zb": d_zb, "kb": d_kb, "vb": d_vb, "if": d_if}
    dP = jnp.concatenate([parts[n] for n in dm.order], axis=1)
    dU = _mm(dP, Wp, "mm_d_u", nt=True, tk=1792)
    gWp = _mm(U.T, dP, "mm_g_w_in", tm=1024, tn=896, tk=2816)
    grad_x, cs_ln = _ln_mod_bwd(dm, dU, xcat, modv, dr_a)

    dmod_me = _rows8(jnp.concatenate([cs_ln[0], cs_ln[1], cs_fin[2]]),
                     jnp.concatenate([cs_ln[2], cs_ln[3], jnp.zeros((D,), F32)]), width=3 * D)
    dmod_all = _all_gather(dmod_me, "ag_dmod", False)
    dmod_loc = lax.dynamic_slice(dmod_all, (0, 0, me * n_mod), (N_DEV, 2, n_mod))
    dmod_rows = _rows8(*[dmod_loc[j, 0] for j in range(N_DEV)], jnp.sum(dmod_loc[:, 1, :], axis=0), width=n_mod)
    g_w_mod, dc_part = _mod_bwd(craw.T, dmod_rows, w_mod[0])

    PW = dm.PW
    small = _rows8(cs_fin[0], cs_fin[1], gmh[0], cs_conv[3], cs_conv[0], cs_conv[1], cs_conv[2],
                   dmod_me[0, 0:D], dmod_me[0, D:2 * D], dmod_me[0, 2 * D:3 * D],
                   dmod_me[1, 0:D], dmod_me[1, D:2 * D],
                   jnp.concatenate([gqn[0], gkn[0], gbif[0]]), dc_part[N_DEV], width=PW)
    tot = _sum_slots(_all_gather(small, "ag_small", False), "sum_small")
    g_ln_w, g_ln_b, g_mh, g_conv_b = tot[0, :D], tot[1, :D], tot[2, :dm.V], tot[3, :QK2]
    g_conv_full = tot[4:7, :QK2]
    g_b_mod = jnp.concatenate([tot[7, :D] + tot[10, :D], tot[8, :D] + tot[11, :D], tot[9, :D]])
    g_qn, g_kn, g_bif = tot[12, 0:HD_B], tot[12, HD_B:2 * HD_B], tot[12, 2 * HD_B:2 * HD_B + dm.NIF]
    g_c_ctx = _cctx_grad(tot[13:14, :D], c_ctx.reshape(1, D))[0]
    g_conv_w = lax.dynamic_slice(g_conv_full, (0, me * n_cv), (CONV_W, n_cv))

    gW = _from_padded(dm, gWp).astype(BF16)
    gW = jnp.transpose(gW.reshape(D, N_DEV, n_in), (1, 0, 2))
    g_w_in = _sum_slots(_all_to_all(gW, "a2a_g_w_in"), "sum_g_w_in")
    gsq = jnp.concatenate([gWba.reshape(N_DEV, rb, D), gWbb.reshape(N_DEV, rb, D),
                           gWout.reshape(N_DEV, rb, D)], axis=1).astype(BF16)
    g_sq = _sum_slots(_all_to_all(gsq, "a2a_g_w_sq"), "sum_g_w_sq")

    d_in, nm_in, nv_in = _adamw(w_in[0], g_w_in, m_w_in[0], v_w_in[0], "adam_w_in")
    d_md, nm_md, nv_md = _adamw(w_mod[0], g_w_mod, m_w_mod[0], v_w_mod[0], "adam_w_mod")
    cat3 = lambda a, b, cc: jnp.concatenate([a[0], b[0], cc[0]], axis=0)
    d_sq, nm_sq, nv_sq = _adamw(cat3(w_branch_a, w_branch_b, w_out), g_sq,
                                cat3(m_w_branch_a, m_w_branch_b, m_w_out),
                                cat3(v_w_branch_a, v_w_branch_b, v_w_out), "adam_w_sq")
    names = ["c_ctx", "b_mod", "b_if", "conv_w", "conv_b", "mh", "qn", "kn", "ln_w", "ln_b"]
    ws = [c_ctx, b_mod, b_if, conv_w, conv_b, mh_norm_w, q_norm_w, k_norm_w, ln_w, ln_b]
    ms = [m_c_ctx, m_b_mod, m_b_if, m_conv_w, m_conv_b, m_mh_norm_w, m_q_norm_w, m_k_norm_w, m_ln_w, m_ln_b]
    vs = [v_c_ctx, v_b_mod, v_b_if, v_conv_w, v_conv_b, v_mh_norm_w, v_q_norm_w, v_k_norm_w, v_ln_w, v_ln_b]
    gs = [g_c_ctx, g_b_mod, g_bif, g_conv_w, g_conv_b, g_mh, g_qn, g_kn, g_ln_w, g_ln_b]
    sizes = [a.size for a in ws]
    tot_n = sum(sizes)
    padn = -(-tot_n // LANE) * LANE
    flat = lambda arrs: jnp.pad(jnp.concatenate([a.reshape(-1) for a in arrs]), (0, padn - tot_n)).reshape(1, padn)
    d_s, nm_s, nv_s = _adamw(flat(ws), flat(gs), flat(ms), flat(vs), "adam_small")

    def split(a):
        res, o = {}, 0
        for n, wv, sz in zip(names, ws, sizes):
            res[n] = a[0, o:o + sz].reshape(wv.shape)
            o += sz
        return res

    def assemble(small_d, big_in, big_md, big_sq):
        s = small_d
        return [s["c_ctx"], big_md[None], s["b_mod"], big_in[None], s["b_if"], s["conv_w"], s["conv_b"],
                s["mh"], s["qn"], s["kn"], big_sq[None, 0:rb], big_sq[None, rb:2 * rb], big_sq[None, 2 * rb:3 * rb],
                s["ln_w"], s["ln_b"]]

    g_small = {n: g.reshape(wv.shape) for n, g, wv in zip(names, gs, ws)}
    grads = assemble(g_small, g_w_in, g_w_mod, g_sq)
    deltas = assemble(split(d_s), d_in, d_md, d_sq)
    new_m = assemble(split(nm_s), nm_in, nm_md, nm_sq)
    new_v = assemble(split(nv_s), nv_in, nv_md, nv_sq)
    return (loss, grad_x[None], *grads, *deltas, *new_m, *new_v)
```

```python
import jax
import jax.numpy as jnp
from jax import lax
from jax.experimental import pallas as pl
from jax.experimental.pallas import tpu as pltpu

F32 = jnp.float32
BF16 = jnp.bfloat16
MESH = pl.DeviceIdType.MESH
N_DEV = 8

GRID_W = 64
NH_A = 8
DK_A = 128
DV_A = 256
CONV_W = 3
CHUNK = 64
M_INIT = -1e30
NH_B = 16
NKV_B = 4
HD_B = 128
ROPE_THETA = 10000.0
EPS = 1e-6
DEPTH = 1
ALPHA = (2 * DEPTH) ** 0.25
ADAM_LR = 0.001
ADAM_B1 = 0.9
ADAM_B2 = 0.999
ADAM_EPS = 1e-08
ADAM_WD = 0.01
ADAM_STEP = 10

LANE = 128
SUBLANE = 8
VMEM_LIMIT = 56 << 20


def _tile(n, target, align):
    best = None
    t = align
    while t <= min(n, target):
        if n % t == 0:
            best = t
        t += align
    return best if best is not None else n


class Dims:
    def __init__(self, S, Tc, D):
        self.S, self.Tc, self.D = S, Tc, D
        self.T = S + Tc
        self.QK = NH_A * DK_A
        self.V = NH_A * DV_A
        self.QB = NH_B * HD_B
        self.KVB = NKV_B * HD_B
        self.G = NH_B // NKV_B
        self.NIF = 4 * NH_A
        self.IFP = 512 if self.KVB % 512 == 0 else LANE
        self.ref_widths = [2 * self.QK, self.V, self.NIF, self.KVB, self.KVB,
                           self.V, self.V, self.QB, self.QB, 2 * D]
        self.ref_names = ["qk", "va", "if", "kb", "vb", "oa", "za", "qb", "zb", "g"]
        self.N_IN = sum(self.ref_widths)
        self.order = ["g", "qk", "va", "oa", "za", "qb", "zb", "kb", "vb", "if"]
        w = dict(zip(self.ref_names, self.ref_widths))
        w["if"] = self.IFP
        self.w = w
        self.off = {}
        o = 0
        for n in self.order:
            assert o % w[n] == 0, (n, o, w[n])
            self.off[n] = o
            o += w[n]
        self.NP = o
        self.TB = min(256, Tc)
        assert Tc % self.TB == 0 and S % self.TB == 0 and self.TB % CHUNK == 0
        self.nctx = Tc // self.TB
        self.nlat = S // self.TB
        self.nblk = self.nctx + self.nlat
        self.PW = max(D, self.V, 2 * self.QK, 3 * LANE)


def _cparams(sem):
    return pltpu.CompilerParams(dimension_semantics=sem, vmem_limit_bytes=VMEM_LIMIT)


def _sigmoid(x):
    return 1.0 / (1.0 + jnp.exp(-x))


def _my_pos():
    return lax.axis_index("x"), lax.axis_index("y"), lax.axis_index("c")


def _all_gather(x, name, big):
    R, C = x.shape
    space = pl.ANY if big else pltpu.VMEM

    def body(x_ref, out_ref, send_sems, recv_sems, local_sem):
        px, py, pc = _my_pos()
        me, sibling = (px, py, pc), (px, py, 1 - pc)
        chips = [(1 - px, py), (px, 1 - py), (1 - px, 1 - py)]

        def slot(bx, by, bc):
            return out_ref.at[4 * bx + 2 * by + bc]

        def copy(k, block, to, src=None):
            return pltpu.make_async_remote_copy(
                src_ref=slot(*block) if src is None else src, dst_ref=slot(*block),
                send_sem=send_sems.at[k], recv_sem=recv_sems.at[k],
                device_id=to, device_id_type=MESH)

        mine = pltpu.make_async_copy(x_ref, slot(*me), local_sem)
        mine.start()
        first = [copy(0, me, sibling, src=x_ref)]
        first += [copy(1 + j, me, (*chip, pc), src=x_ref) for j, chip in enumerate(chips)]
        for cp in first:
            cp.start()
        passed = [copy(4 + j, (*chip, pc), sibling) for j, chip in enumerate(chips)]
        for j, chip in enumerate(chips):
            copy(1 + j, (*chip, pc), me).wait_recv()
            passed[j].start()
        copy(0, sibling, me).wait_recv()
        for j, chip in enumerate(chips):
            copy(4 + j, (*chip, 1 - pc), me).wait_recv()
        for cp in first + passed:
            cp.wait_send()
        mine.wait()

    return pl.pallas_call(
        body, name=name,
        out_shape=jax.ShapeDtypeStruct((N_DEV, R, C), x.dtype),
        in_specs=[pl.BlockSpec(memory_space=space)],
        out_specs=pl.BlockSpec(memory_space=space),
        scratch_shapes=[pltpu.SemaphoreType.DMA((7,)), pltpu.SemaphoreType.DMA((7,)),
                        pltpu.SemaphoreType.DMA],
    )(x)


EXCHANGE_SEMS = [pltpu.SemaphoreType.DMA((N_DEV - 1,)), pltpu.SemaphoreType.DMA((N_DEV - 1,)),
                 pltpu.SemaphoreType.DMA]


def _exchange(kind, src_ref, land_ref, send_sems, recv_sems, local_sem):
    def copies():
        px, py, pc = _my_pos()
        me = 4 * px + 2 * py + pc
        own = src_ref if kind == "ag" else src_ref.at[me]
        local = pltpu.make_async_copy(own, land_ref.at[me], local_sem)
        sends, recvs = [], []
        for r in range(1, N_DEV):
            dx, dy, dc = (r >> 2) & 1, (r >> 1) & 1, r & 1
            qx = px if dx == 0 else 1 - px
            qy = py if dy == 0 else 1 - py
            qc = pc if dc == 0 else 1 - pc
            peer = 4 * qx + 2 * qy + qc
            sems = dict(send_sem=send_sems.at[r - 1], recv_sem=recv_sems.at[r - 1],
                        device_id=(qx, qy, qc), device_id_type=MESH)
            sends.append(pltpu.make_async_remote_copy(
                src_ref=src_ref if kind == "ag" else src_ref.at[peer], dst_ref=land_ref.at[me], **sems))
            recvs.append(pltpu.make_async_remote_copy(src_ref=own, dst_ref=land_ref.at[peer], **sems))
        return local, sends, recvs

    def start():
        local, sends, _ = copies()
        local.start()
        for cp in sends:
            cp.start()

    def wait():
        local, sends, recvs = copies()
        for cp in recvs:
            cp.wait_recv()
        for cp in sends:
            cp.wait_send()
        local.wait()

    return start, wait


def _land_shape(kind, src):
    return jax.ShapeDtypeStruct(src.shape if kind == "a2a" else (N_DEV,) + src.shape, src.dtype)


def _sum_slots(a, name):
    _, R, C = a.shape
    tb = _tile(R, max(SUBLANE, (1 << 20) // (4 * C) // SUBLANE * SUBLANE), SUBLANE)

    def kern(a_ref, o_ref):
        acc = a_ref[0].astype(F32)
        for j in range(1, N_DEV):
            acc = acc + a_ref[j].astype(F32)
        o_ref[...] = acc

    return pl.pallas_call(
        kern, name=name, grid=(R // tb,),
        in_specs=[pl.BlockSpec((N_DEV, tb, C), lambda i: (0, i, 0))],
        out_specs=pl.BlockSpec((tb, C), lambda i: (i, 0)),
        out_shape=jax.ShapeDtypeStruct((R, C), F32),
        compiler_params=_cparams(("parallel",)),
    )(a)


def _mm(a, b, name, nt=False, tm=768, tn=1024, tk=2048, out_dtype=F32, exchange=None):
    M, K = a.shape
    N = b.shape[0] if nt else b.shape[1]
    assert (b.shape[1] if nt else b.shape[0]) == K
    tm, tn, tk = _tile(M, tm, 16), _tile(N, tn, LANE), _tile(K, tk, LANE)
    ni, nj, nk = M // tm, N // tn, K // tk

    def dot(x, y):
        if nt:
            return lax.dot_general(x, y, (((1,), (1,)), ((), ())), preferred_element_type=F32)
        return jnp.dot(x, y, preferred_element_type=F32)

    def kern(a_ref, b_ref, *rest):
        if exchange is not None:
            src_ref, o_ref, land_ref, acc_ref, send_sems, recv_sems, local_sem = rest
            start, wait = _exchange(exchange[0], src_ref, land_ref, send_sems, recv_sems, local_sem)
            i, j, kk = pl.program_id(0), pl.program_id(1), pl.program_id(2)
            pl.when((i == 0) & (j == 0) & (kk == 0))(start)
        else:
            o_ref, acc_ref = rest
        k = pl.program_id(2)
        part = dot(a_ref[...], b_ref[...])
        if nk == 1:
            o_ref[...] = part.astype(o_ref.dtype)
        else:
            @pl.when(k == 0)
            def _():
                acc_ref[...] = part

            @pl.when(k > 0)
            def _():
                acc_ref[...] += part

            @pl.when(k == nk - 1)
            def _():
                o_ref[...] = acc_ref[...].astype(o_ref.dtype)

        if exchange is not None:
            pl.when((i == ni - 1) & (j == nj - 1) & (kk == nk - 1))(wait)

    b_spec = (pl.BlockSpec((tn, tk), lambda i, j, k: (j, k)) if nt
              else pl.BlockSpec((tk, tn), lambda i, j, k: (k, j)))
    in_specs = [pl.BlockSpec((tm, tk), lambda i, j, k: (i, k)), b_spec]
    out_specs = pl.BlockSpec((tm, tn), lambda i, j, k: (i, j))
    out_shape = jax.ShapeDtypeStruct((M, N), out_dtype)
    scratch = [pltpu.VMEM((tm, tn) if nk > 1 else (SUBLANE, LANE), F32)]
    args = (a, b)
    sem = ("parallel", "parallel", "arbitrary")
    if exchange is not None:
        in_specs = in_specs + [pl.BlockSpec(memory_space=pl.ANY)]
        out_specs = [out_specs, pl.BlockSpec(memory_space=pl.ANY)]
        out_shape = [out_shape, _land_shape(*exchange)]
        scratch = scratch + EXCHANGE_SEMS
        args = (a, b, exchange[1])
        sem = ("arbitrary", "arbitrary", "arbitrary")
    return pl.pallas_call(
        kern, name=name, grid=(ni, nj, nk), in_specs=in_specs, out_specs=out_specs,
        out_shape=out_shape, scratch_shapes=scratch, compiler_params=_cparams(sem),
    )(*args)


def _ln_stats(x):
    mu = jnp.mean(x, axis=-1, keepdims=True)
    xc = x - mu
    var = jnp.mean(xc * xc, axis=-1, keepdims=True)
    rstd = lax.rsqrt(var + EPS)
    return xc * rstd, rstd


def _ln_mod_fwd(dm, xcat, modv):
    TB, D, nctx = dm.TB, dm.D, dm.nctx

    def kern(x_ref, m_ref, u_ref):
        is_ctx = pl.program_id(0) < nctx
        xh, _ = _ln_stats(x_ref[...])
        shift = jnp.where(is_ctx, m_ref[2:3, :], m_ref[0:1, :])
        scale = jnp.where(is_ctx, m_ref[3:4, :], m_ref[1:2, :])
        u_ref[...] = (xh * (1.0 + scale) + shift).astype(BF16)

    return pl.pallas_call(
        kern, name="ln_mod_fwd", grid=(dm.nblk,),
        in_specs=[pl.BlockSpec((TB, D), lambda i: (i, 0)), pl.BlockSpec((SUBLANE, D), lambda i: (0, 0))],
        out_specs=pl.BlockSpec((TB, D), lambda i: (i, 0)),
        out_shape=jax.ShapeDtypeStruct((dm.T, D), BF16),
        compiler_params=_cparams(("parallel",)),
    )(xcat, modv)


def _conv_specs(dm, W, col):
    TB, T = dm.TB, dm.T
    r8 = TB // SUBLANE
    last8 = T // SUBLANE - 1
    return [pl.BlockSpec((TB, W), lambda i: (i, col)),
            pl.BlockSpec((SUBLANE, W), lambda i: (jnp.maximum(i * r8 - 1, 0), col)),
            pl.BlockSpec((SUBLANE, W), lambda i: (jnp.minimum((i + 1) * r8, last8), col))]


def _shifted(dm, x, prev_ref, next_ref):
    TB, nctx, nblk = dm.TB, dm.nctx, dm.nblk
    i = pl.program_id(0)
    row = lax.broadcasted_iota(jnp.int32, x.shape, 0)
    zero_prev = (i == 0) | (i == nctx)
    zero_next = (i == nctx - 1) | (i == nblk - 1)
    before = jnp.where(zero_prev, 0.0, prev_ref[SUBLANE - 1:SUBLANE, :])
    after = jnp.where(zero_next, 0.0, next_ref[0:1, :])
    xp = jnp.where(row == 0, before, pltpu.roll(x, 1, 0))
    xn = jnp.where(row == TB - 1, after, pltpu.roll(x, TB - 1, 0))
    return xp, xn


def _conv_silu_fwd(dm, P, convp):
    W = 2 * dm.QK
    col = dm.off["qk"] // W
    kscale = DK_A ** -0.5

    def kern(x_ref, p_ref, n_ref, c_ref, o_ref):
        x = x_ref[...]
        xp, xn = _shifted(dm, x, p_ref, n_ref)
        z = c_ref[3:4, :] + xp * c_ref[0:1, :] + x * c_ref[1:2, :] + xn * c_ref[2:3, :]
        lane = lax.broadcasted_iota(jnp.int32, (1, W), 1)
        cs = jnp.where(lane >= dm.QK, kscale, 1.0)
        o_ref[...] = z * _sigmoid(z) * cs

    return pl.pallas_call(
        kern, name="conv_silu_fwd", grid=(dm.nblk,),
        in_specs=_conv_specs(dm, W, col) + [pl.BlockSpec((SUBLANE, W), lambda i: (0, 0))],
        out_specs=pl.BlockSpec((dm.TB, W), lambda i: (i, 0)),
        out_shape=jax.ShapeDtypeStruct((dm.T, W), F32),
        compiler_params=_cparams(("parallel",)),
    )(P, P, P, convp)


def _gates_fwd(dm, P, bif):
    col = dm.off["if"] // LANE

    def kern(x_ref, b_ref, o_ref):
        x = x_ref[...] + b_ref[0:1, :]
        lane = lax.broadcasted_iota(jnp.int32, x.shape, 1)
        is_f = ((lane // NH_A) % 2) == 1
        ls = jnp.minimum(x, 0.0) - jnp.log(1.0 + jnp.exp(-jnp.abs(x)))
        o_ref[...] = jnp.where(lane < dm.NIF, jnp.where(is_f, ls, x), 0.0)

    return pl.pallas_call(
        kern, name="gates_fwd", grid=(dm.nblk,),
        in_specs=[pl.BlockSpec((dm.TB, LANE), lambda i: (i, col)),
                  pl.BlockSpec((SUBLANE, LANE), lambda i: (0, 0))],
        out_specs=pl.BlockSpec((dm.TB, LANE), lambda i: (i, 0)),
        out_shape=jax.ShapeDtypeStruct((dm.T, LANE), F32),
        compiler_params=_cparams(("parallel",)),
    )(P, bif)


def _mlstm_order(dm, reverse, backward):
    nctx, nblk = dm.nctx, dm.nblk

    def idx(i):
        if backward:
            i = nblk - 1 - i
        if not reverse:
            return i
        return jnp.where(i < nctx, nctx - 1 - i, nblk - 1 - (i - nctx))

    return idx


def _chunk_gates(g, ci, cf, mask_f, maskT_f, eye_f):
    lane = lax.broadcasted_iota(jnp.int32, g.shape, 1)
    gi_c = jnp.sum(jnp.where(lane == ci, g, 0.0), axis=1, keepdims=True)
    gf_c = jnp.sum(jnp.where(lane == cf, g, 0.0), axis=1, keepdims=True)
    gi_r = jnp.sum(eye_f * gi_c, axis=0, keepdims=True)
    gf_r = jnp.sum(eye_f * gf_c, axis=0, keepdims=True)
    b_c = jnp.sum(mask_f * gf_r, axis=1, keepdims=True)
    b_r = jnp.sum(maskT_f * gf_c, axis=0, keepdims=True)
    return gi_c, gi_r, b_c, b_r


def _chunk_masks(reverse):
    L = CHUNK
    r = lax.broadcasted_iota(jnp.int32, (L, L), 0)
    c = lax.broadcasted_iota(jnp.int32, (L, L), 1)
    mask = (c >= r) if reverse else (c <= r)
    maskT = (r >= c) if reverse else (r <= c)
    return mask, mask.astype(F32), maskT.astype(F32), (r == c).astype(F32)


def _pick_row(x, e):
    r = lax.broadcasted_iota(jnp.int32, x.shape, 0)
    return jnp.sum(jnp.where(r == e, x, 0.0), axis=0, keepdims=True)


def _dot_nt(a, b):
    return lax.dot_general(a, b, (((1,), (1,)), ((), ())), preferred_element_type=F32)


def _dot(a, b):
    return jnp.dot(a, b, preferred_element_type=F32)


def _chunk_fwd_core(q, k, g, ci, cf, C0, n0, m0, masks, reverse):
    mask, mask_f, maskT_f, eye_f = masks
    gi_c, gi_r, b_c, b_r = _chunk_gates(g, ci, cf, mask_f, maskT_f, eye_f)
    d = jnp.where(mask, b_c - b_r + gi_r, -jnp.inf)
    m_c = jnp.maximum(b_c + m0, jnp.max(d, axis=1, keepdims=True))
    w = jnp.exp(d - m_c)
    a_c = jnp.exp(b_c + m0 - m_c)
    qb, kb = q.astype(BF16), k.astype(BF16)
    s = _dot_nt(qb, kb) * w
    den = a_c * jnp.sum(q * n0, axis=1, keepdims=True) + jnp.sum(s, axis=1, keepdims=True)
    e = 0 if reverse else CHUNK - 1
    m_end, b_end, a_end = _pick_row(m_c, e), _pick_row(b_c, e), _pick_row(a_c, e)
    w_end = jnp.exp(b_end - b_c + gi_c - m_end)
    return qb, kb, s, w, a_c, m_c, den, w_end, a_end, m_end


def _mlstm_fwd(dm, qk_act, P, G, reverse):
    TB, T = dm.TB, dm.T
    NC = TB // CHUNK
    idx = _mlstm_order(dm, reverse, False)
    vcol = dm.off["va"] // DV_A
    base = 2 * NH_A if reverse else 0

    def kern(q_ref, k_ref, v_ref, g_ref, h_ref, cst_ref, nm_ref, C_s, N_s):
        i, h = pl.program_id(0), pl.program_id(1)
        row8 = lax.broadcasted_iota(jnp.int32, (SUBLANE, DK_A), 0)

        @pl.when(i == 0)
        def _():
            C_s[h] = jnp.zeros((DK_A, DV_A), F32)
            N_s[h] = jnp.where(row8 == 1, M_INIT, 0.0)

        masks = _chunk_masks(reverse)
        ci, cf = base + h, base + NH_A + h
        C0 = C_s[h]
        n0, m0 = N_s.at[h][0:1, :], N_s.at[h][1:2, 0:1]
        for c in (range(NC - 1, -1, -1) if reverse else range(NC)):
            rows = pl.ds(c * CHUNK, CHUNK)
            q, k, v, g = q_ref[rows, :], k_ref[rows, :], v_ref[rows, :], g_ref[rows, :]
            cst_ref[0, c] = C0
            nm_ref[0, c] = jnp.where(row8 == 0, n0, jnp.where(row8 == 1, m0, 0.0))
            qb, kb, s, w, a_c, m_c, den, w_end, a_end, m_end = _chunk_fwd_core(
                q, k, g, ci, cf, C0, n0, m0, masks, reverse)
            vb = v.astype(BF16)
            num = a_c * _dot(qb, C0.astype(BF16)) + _dot(s.astype(BF16), vb)
            h_ref[rows, :] = num / jnp.maximum(jnp.abs(den), jnp.exp(-m_c))
            C0 = a_end * C0 + _dot(k.T.astype(BF16), (w_end * v).astype(BF16))
            n0 = a_end * n0 + jnp.sum(w_end * k, axis=0, keepdims=True)
            m0 = m_end
        C_s[h] = C0
        N_s[h] = jnp.where(row8 == 0, n0, jnp.where(row8 == 1, m0, 0.0))

    nch = T // CHUNK
    return pl.pallas_call(
        kern, name="mlstm_fwd_rev" if reverse else "mlstm_fwd", grid=(dm.nblk, NH_A),
        in_specs=[pl.BlockSpec((TB, DK_A), lambda i, h: (idx(i), h)),
                  pl.BlockSpec((TB, DK_A), lambda i, h: (idx(i), NH_A + h)),
                  pl.BlockSpec((TB, DV_A), lambda i, h: (idx(i), vcol + h)),
                  pl.BlockSpec((TB, LANE), lambda i, h: (idx(i), 0))],
        out_specs=[pl.BlockSpec((TB, DV_A), lambda i, h: (idx(i), h)),
                   pl.BlockSpec((1, NC, DK_A, DV_A), lambda i, h: (h, idx(i), 0, 0)),
                   pl.BlockSpec((1, NC, SUBLANE, DK_A), lambda i, h: (h, idx(i), 0, 0))],
        out_shape=[jax.ShapeDtypeStruct((T, dm.V), F32),
                   jax.ShapeDtypeStruct((NH_A, nch, DK_A, DV_A), F32),
                   jax.ShapeDtypeStruct((NH_A, nch, SUBLANE, DK_A), F32)],
        scratch_shapes=[pltpu.VMEM((NH_A, DK_A, DV_A), F32), pltpu.VMEM((NH_A, SUBLANE, DK_A), F32)],
        compiler_params=_cparams(("arbitrary", "arbitrary")),
    )(qk_act, qk_act, P, G)


def _mlstm_bwd(dm, qk_act, P, G, Cst, NM, H, dH, acc, reverse):
    TB, T = dm.TB, dm.T
    NC = TB // CHUNK
    idx = _mlstm_order(dm, reverse, True)
    vcol = dm.off["va"] // DV_A
    base = 2 * NH_A if reverse else 0
    has_acc = acc is not None

    def kern(*refs):
        (q_ref, k_ref, v_ref, g_ref, cst_ref, nm_ref, hh_ref, dh_ref) = refs[:8]
        p = 8
        if has_acc:
            aq_ref, ak_ref, av_ref, ag_ref = refs[p:p + 4]
            p += 4
        dq_ref, dk_ref, dv_ref, dg_ref, R_s, Rn_s = refs[p:p + 6]
        i, h = pl.program_id(0), pl.program_id(1)
        row8 = lax.broadcasted_iota(jnp.int32, (SUBLANE, DK_A), 0)

        @pl.when(i == 0)
        def _():
            R_s[h] = jnp.zeros((DK_A, DV_A), F32)
            Rn_s[h] = jnp.zeros((SUBLANE, DK_A), F32)

        @pl.when(h == 0)
        def _():
            dg_ref[...] = ag_ref[...] if has_acc else jnp.zeros((TB, LANE), F32)

        masks = _chunk_masks(reverse)
        _, mask_f, maskT_f, eye_f = masks
        before_f = mask_f - eye_f
        ci, cf = base + h, base + NH_A + h
        R = R_s[h]
        Rn = Rn_s.at[h][0:1, :]
        lane = lax.broadcasted_iota(jnp.int32, (CHUNK, LANE), 1)

        def as_row(col):
            return jnp.sum(eye_f * col, axis=0, keepdims=True)
        for c in (range(NC) if reverse else range(NC - 1, -1, -1)):
            rows = pl.ds(c * CHUNK, CHUNK)
            q, k, v, g = q_ref[rows, :], k_ref[rows, :], v_ref[rows, :], g_ref[rows, :]
            C0 = cst_ref[0, c]
            n0, m0 = nm_ref.at[0, c][0:1, :], nm_ref.at[0, c][1:2, 0:1]
            qb, kb, s, w, a_c, m_c, den, w_end, a_end, _ = _chunk_fwd_core(
                q, k, g, ci, cf, C0, n0, m0, masks, reverse)
            vb = v.astype(BF16)
            e_m = jnp.exp(-m_c)
            r = 1.0 / jnp.maximum(jnp.abs(den), e_m)
            dh = dh_ref[rows, :]
            dN = dh * r
            dD = jnp.where(jnp.abs(den) > e_m,
                           -jnp.sum(dh * hh_ref[rows, :], axis=1, keepdims=True) * r * jnp.sign(den), 0.0)
            dNb = dN.astype(BF16)
            dS = _dot_nt(dNb, vb) + dD
            dqk = dS * w
            Cb, Rb = C0.astype(BF16), R.astype(BF16)
            dq_in = a_c * (_dot_nt(dNb, Cb) + dD * n0)
            dk_out = w_end * (_dot_nt(vb, Rb) + Rn)
            dq = _dot(dqk.astype(BF16), kb) + dq_in
            dk = _dot(dqk.T.astype(BF16), qb) + dk_out
            dv = _dot(s.T.astype(BF16), dNb) + w_end * _dot(kb, Rb)
            if has_acc:
                dq_ref[rows, :] = aq_ref[rows, :] + dq
                dk_ref[rows, :] = ak_ref[rows, :] + dk
                dv_ref[rows, :] = av_ref[rows, :] + dv
            else:
                dq_ref[rows, :] = dq
                dk_ref[rows, :] = dk
                dv_ref[rows, :] = dv
            gm = dS * s
            g_row = jnp.sum(gm, axis=1, keepdims=True)
            g_col = jnp.sum(eye_f * jnp.sum(gm, axis=0, keepdims=True), axis=1, keepdims=True)
            q_in = jnp.sum(q * dq_in, axis=1, keepdims=True)
            k_out = jnp.sum(k * dk_out, axis=1, keepdims=True)
            through = a_end * (jnp.sum(jnp.sum(R * C0, axis=1, keepdims=True), axis=0, keepdims=True)
                               + jnp.sum(Rn * n0, axis=1, keepdims=True))
            di = g_col + k_out
            df = (jnp.sum(maskT_f * as_row(g_row - g_col + q_in), axis=1, keepdims=True)
                  + jnp.sum(before_f * as_row(k_out), axis=1, keepdims=True) + through)
            dg_ref[rows, :] += jnp.where(lane == ci, di, 0.0) + jnp.where(lane == cf, df, 0.0)
            aq = a_c * q
            R = a_end * R + _dot(aq.T.astype(BF16), dNb)
            Rn = a_end * Rn + jnp.sum(aq * dD, axis=0, keepdims=True)
        R_s[h] = R
        Rn_s[h] = jnp.where(row8 == 0, Rn, 0.0)

    qspec = pl.BlockSpec((TB, DK_A), lambda i, h: (idx(i), h))
    vspec = pl.BlockSpec((TB, DV_A), lambda i, h: (idx(i), h))
    gspec = pl.BlockSpec((TB, LANE), lambda i, h: (idx(i), 0))
    in_specs = [qspec,
                pl.BlockSpec((TB, DK_A), lambda i, h: (idx(i), NH_A + h)),
                pl.BlockSpec((TB, DV_A), lambda i, h: (idx(i), vcol + h)),
                gspec,
                pl.BlockSpec((1, NC, DK_A, DV_A), lambda i, h: (h, idx(i), 0, 0)),
                pl.BlockSpec((1, NC, SUBLANE, DK_A), lambda i, h: (h, idx(i), 0, 0)),
                vspec, vspec]
    args = [qk_act, qk_act, P, G, Cst, NM, H, dH]
    if has_acc:
        in_specs += [qspec, qspec, vspec, gspec]
        args += list(acc)
    return pl.pallas_call(
        kern, name="mlstm_bwd_rev" if reverse else "mlstm_bwd", grid=(dm.nblk, NH_A),
        in_specs=in_specs,
        out_specs=[qspec, qspec, vspec, gspec],
        out_shape=[jax.ShapeDtypeStruct((T, dm.QK), F32), jax.ShapeDtypeStruct((T, dm.QK), F32),
                   jax.ShapeDtypeStruct((T, dm.V), F32), jax.ShapeDtypeStruct((T, LANE), F32)],
        scratch_shapes=[pltpu.VMEM((NH_A, DK_A, DV_A), F32), pltpu.VMEM((NH_A, SUBLANE, DK_A), F32)],
        compiler_params=_cparams(("arbitrary", "arbitrary")),
    )(*args)


def _rms_heads(x, w_row, nh, hd):
    out = []
    for h in range(nh):
        xh = x[:, h * hd:(h + 1) * hd]
        rstd = lax.rsqrt(jnp.mean(xh * xh, axis=1, keepdims=True) + EPS)
        out.append((xh * rstd, rstd))
    return out


def _rope(x, cos, sa, sb):
    return x * cos + pltpu.roll(x, HD_B - HD_B // 4, 1) * sa + pltpu.roll(x, HD_B // 4, 1) * sb


def _rope_t(dy, cos, sa, sb):
    return dy * cos + pltpu.roll(dy * sa, HD_B // 4, 1) + pltpu.roll(dy * sb, HD_B - HD_B // 4, 1)


ATT_SCALE = HD_B ** -0.5
LOG2E = 1.4426950408889634
LN2 = 0.6931471805599453
QSCALE = ATT_SCALE * LOG2E


def _qk_prep(dm, P, rope, qn, kn):
    TB, nctx = dm.TB, dm.nctx
    qcol, kcol, vcol = dm.off["qb"] // dm.QB, dm.off["kb"] // dm.KVB, dm.off["vb"] // dm.KVB

    def kern_q(x_ref, t_ref, w_ref, o_ref):
        cos, sa, sb = t_ref[0], t_ref[1], t_ref[2]
        for h, (xn, _) in enumerate(_rms_heads(x_ref[...], None, NH_B, HD_B)):
            o_ref[h] = (_rope(xn * w_ref[0:1, :], cos, sa, sb) * QSCALE).astype(BF16)

    Qr = pl.pallas_call(
        kern_q, name="q_prep", grid=(dm.nlat,),
        in_specs=[pl.BlockSpec((TB, dm.QB), lambda i: (i + nctx, qcol)),
                  pl.BlockSpec((3, TB, HD_B), lambda i: (0, i + nctx, 0)),
                  pl.BlockSpec((SUBLANE, HD_B), lambda i: (0, 0))],
        out_specs=pl.BlockSpec((NH_B, TB, HD_B), lambda i: (0, i, 0)),
        out_shape=jax.ShapeDtypeStruct((NH_B, dm.S, HD_B), BF16),
        compiler_params=_cparams(("parallel",)),
    )(P, rope, qn)

    def kern_k(x_ref, v_ref, t_ref, w_ref, o_ref, vo_ref):
        cos, sa, sb = t_ref[0], t_ref[1], t_ref[2]
        for h, (xn, _) in enumerate(_rms_heads(x_ref[...], None, NKV_B, HD_B)):
            o_ref[:, h * HD_B:(h + 1) * HD_B] = _rope(xn * w_ref[0:1, :], cos, sa, sb).astype(BF16)
        vo_ref[...] = v_ref[...].astype(BF16)

    Kr, Vb = pl.pallas_call(
        kern_k, name="k_prep", grid=(dm.nblk,),
        in_specs=[pl.BlockSpec((TB, dm.KVB), lambda i: (i, kcol)),
                  pl.BlockSpec((TB, dm.KVB), lambda i: (i, vcol)),
                  pl.BlockSpec((3, TB, HD_B), lambda i: (0, i, 0)),
                  pl.BlockSpec((SUBLANE, HD_B), lambda i: (0, 0))],
        out_specs=[pl.BlockSpec((TB, dm.KVB), lambda i: (i, 0))] * 2,
        out_shape=[jax.ShapeDtypeStruct((dm.T, dm.KVB), BF16)] * 2,
        compiler_params=_cparams(("parallel",)),
    )(P, P, rope, kn)
    return Qr, Kr, Vb


def _attn_tiles(dm):
    return _tile(dm.S, 512, LANE), _tile(dm.T, 768, LANE)


def _attn_fwd(dm, Qr, Kr, Vb):
    S, T, G = dm.S, dm.T, dm.G
    tq, tk = _attn_tiles(dm)
    nk = T // tk

    def kern(q_ref, k_ref, v_ref, o_ref, l_ref, m_s, l_s, a_s):
        j = pl.program_id(2)

        @pl.when(j == 0)
        def _():
            m_s[...] = jnp.full(m_s.shape, -jnp.inf, F32)
            l_s[...] = jnp.zeros(l_s.shape, F32)
            a_s[...] = jnp.zeros(a_s.shape, F32)

        k, v = k_ref[...], v_ref[...]
        for h in range(G):
            s = _dot_nt(q_ref[h], k)
            m_old = m_s[h]
            m_new = jnp.maximum(m_old, jnp.max(s, axis=1, keepdims=True))
            p = jnp.exp2(s - m_new)
            corr = jnp.exp2(m_old - m_new)
            l_s[h] = corr * l_s[h] + jnp.sum(p, axis=1, keepdims=True)
            m_s[h] = m_new
            a_s[h] = corr * a_s[h] + _dot(p.astype(BF16), v)

        @pl.when(j == nk - 1)
        def _():
            for h in range(G):
                o_ref[h] = a_s[h] / l_s[h]
                l_ref[0, :, h:h + 1] = m_s[h] + jnp.log(l_s[h]) * LOG2E

    qspec = pl.BlockSpec((G, tq, HD_B), lambda g, i, j: (g, i, 0))
    return pl.pallas_call(
        kern, name="attn_fwd", grid=(NKV_B, S // tq, nk),
        in_specs=[qspec,
                  pl.BlockSpec((tk, HD_B), lambda g, i, j: (j, g)),
                  pl.BlockSpec((tk, HD_B), lambda g, i, j: (j, g))],
        out_specs=[qspec, pl.BlockSpec((1, tq, G), lambda g, i, j: (g, i, 0))],
        out_shape=[jax.ShapeDtypeStruct((NH_B, S, HD_B), F32), jax.ShapeDtypeStruct((NKV_B, S, G), F32)],
        scratch_shapes=[pltpu.VMEM((G, tq, 1), F32), pltpu.VMEM((G, tq, 1), F32),
                        pltpu.VMEM((G, tq, HD_B), F32)],
        compiler_params=_cparams(("parallel", "parallel", "arbitrary")),
    )(Qr, Kr, Vb)


def _attn_bwd_dq(dm, Qr, Kr, Vb, dO, LSE, DEL):
    S, T, G = dm.S, dm.T, dm.G
    tq, tk = _attn_tiles(dm)
    nk = T // tk

    def kern(q_ref, k_ref, v_ref, do_ref, l_ref, d_ref, dq_ref, a_s):
        j = pl.program_id(2)

        @pl.when(j == 0)
        def _():
            a_s[...] = jnp.zeros(a_s.shape, F32)

        k, v = k_ref[...], v_ref[...]
        for h in range(G):
            p = jnp.exp2(_dot_nt(q_ref[h], k) - l_ref[0, :, h:h + 1])
            dp = _dot_nt(do_ref[h], v)
            a_s[h] += _dot((p * (dp - d_ref[0, :, h:h + 1])).astype(BF16), k)

        @pl.when(j == nk - 1)
        def _():
            for h in range(G):
                dq_ref[h] = a_s[h] * ATT_SCALE

    qspec = pl.BlockSpec((G, tq, HD_B), lambda g, i, j: (g, i, 0))
    kspec = pl.BlockSpec((tk, HD_B), lambda g, i, j: (j, g))
    lspec = pl.BlockSpec((1, tq, G), lambda g, i, j: (g, i, 0))
    return pl.pallas_call(
        kern, name="attn_bwd_dq", grid=(NKV_B, S // tq, nk),
        in_specs=[qspec, kspec, kspec, qspec, lspec, lspec],
        out_specs=qspec,
        out_shape=jax.ShapeDtypeStruct((NH_B, S, HD_B), F32),
        scratch_shapes=[pltpu.VMEM((G, tq, HD_B), F32)],
        compiler_params=_cparams(("parallel", "parallel", "arbitrary")),
    )(Qr, Kr, Vb, dO, LSE, DEL)


def _attn_bwd_dkv(dm, Qr, Kr, Vb, dO, LSE_T, DEL_T):
    S, T, G = dm.S, dm.T, dm.G
    tq, tk = _attn_tiles(dm)
    nq = S // tq

    def kern(q_ref, k_ref, v_ref, do_ref, l_ref, d_ref, dk_ref, dv_ref, ak_s, av_s):
        i = pl.program_id(2)

        @pl.when(i == 0)
        def _():
            ak_s[...] = jnp.zeros(ak_s.shape, F32)
            av_s[...] = jnp.zeros(av_s.shape, F32)

        k, v = k_ref[...], v_ref[...]
        for h in range(G):
            q, do = q_ref[h], do_ref[h]
            pT = jnp.exp2(_dot_nt(k, q) - l_ref[0, h:h + 1, :])
            dpT = _dot_nt(v, do)
            av_s[...] += _dot(pT.astype(BF16), do)
            ak_s[...] += _dot((pT * (dpT - d_ref[0, h:h + 1, :])).astype(BF16), q)

        @pl.when(i == nq - 1)
        def _():
            dk_ref[...] = ak_s[...] * LN2
            dv_ref[...] = av_s[...].astype(BF16)

    qspec = pl.BlockSpec((G, tq, HD_B), lambda g, j, i: (g, i, 0))
    kspec = pl.BlockSpec((tk, HD_B), lambda g, j, i: (j, g))
    lspec = pl.BlockSpec((1, G, tq), lambda g, j, i: (g, 0, i))
    return pl.pallas_call(
        kern, name="attn_bwd_dkv", grid=(NKV_B, T // tk, nq),
        in_specs=[qspec, kspec, kspec, qspec, lspec, lspec],
        out_specs=[kspec, kspec],
        out_shape=[jax.ShapeDtypeStruct((T, dm.KVB), F32), jax.ShapeDtypeStruct((T, dm.KVB), BF16)],
        scratch_shapes=[pltpu.VMEM((tk, HD_B), F32), pltpu.VMEM((tk, HD_B), F32)],
        compiler_params=_cparams(("parallel", "parallel", "arbitrary")),
    )(Qr, Kr, Vb, dO, LSE_T, DEL_T)


def _qk_bwd(dm, dQr, dKr, P, rope, qn, kn):
    TB, nctx = dm.TB, dm.nctx
    qcol, kcol = dm.off["qb"] // dm.QB, dm.off["kb"] // dm.KVB

    def head_bwd(dyr, x, w_row, cos, sa, sb):
        rstd = lax.rsqrt(jnp.mean(x * x, axis=1, keepdims=True) + EPS)
        xn = x * rstd
        dy = _rope_t(dyr, cos, sa, sb)
        dw = jnp.sum(dy * xn, axis=0, keepdims=True)
        dxn = dy * w_row
        dx = rstd * (dxn - xn * jnp.mean(dxn * xn, axis=1, keepdims=True))
        return dx, dw

    def make(nh, ctx_zero):
        def kern(d_ref, x_ref, t_ref, w_ref, o_ref, gw_ref):
            i = pl.program_id(0)

            @pl.when(i == 0)
            def _():
                gw_ref[...] = jnp.zeros(gw_ref.shape, F32)

            def live():
                cos, sa, sb = t_ref[0], t_ref[1], t_ref[2]
                tot = jnp.zeros((1, HD_B), F32)
                for h in range(nh):
                    cols = slice(h * HD_B, (h + 1) * HD_B)
                    dyr = d_ref[h] if ctx_zero else d_ref[:, cols]
                    dx, dw = head_bwd(dyr, x_ref[:, cols], w_ref[0:1, :], cos, sa, sb)
                    o_ref[:, cols] = dx.astype(BF16)
                    tot = tot + dw
                gw_ref[0:1, :] += tot

            if ctx_zero:
                @pl.when(i < nctx)
                def _():
                    o_ref[...] = jnp.zeros(o_ref.shape, BF16)

                pl.when(i >= nctx)(live)
            else:
                live()
        return kern

    lat = lambda i: jnp.maximum(i - nctx, 0)
    d_qb, gqn = pl.pallas_call(
        make(NH_B, True), name="q_bwd", grid=(dm.nblk,),
        in_specs=[pl.BlockSpec((NH_B, TB, HD_B), lambda i: (0, lat(i), 0)),
                  pl.BlockSpec((TB, dm.QB), lambda i: (i, qcol)),
                  pl.BlockSpec((3, TB, HD_B), lambda i: (0, i, 0)),
                  pl.BlockSpec((SUBLANE, HD_B), lambda i: (0, 0))],
        out_specs=[pl.BlockSpec((TB, dm.QB), lambda i: (i, 0)), pl.BlockSpec((SUBLANE, HD_B), lambda i: (0, 0))],
        out_shape=[jax.ShapeDtypeStruct((dm.T, dm.QB), BF16), jax.ShapeDtypeStruct((SUBLANE, HD_B), F32)],
        compiler_params=_cparams(("arbitrary",)),
    )(dQr, P, rope, qn)
    d_kb, gkn = pl.pallas_call(
        make(NKV_B, False), name="k_bwd", grid=(dm.nblk,),
        in_specs=[pl.BlockSpec((TB, dm.KVB), lambda i: (i, 0)),
                  pl.BlockSpec((TB, dm.KVB), lambda i: (i, kcol)),
                  pl.BlockSpec((3, TB, HD_B), lambda i: (0, i, 0)),
                  pl.BlockSpec((SUBLANE, HD_B), lambda i: (0, 0))],
        out_specs=[pl.BlockSpec((TB, dm.KVB), lambda i: (i, 0)), pl.BlockSpec((SUBLANE, HD_B), lambda i: (0, 0))],
        out_shape=[jax.ShapeDtypeStruct((dm.T, dm.KVB), BF16), jax.ShapeDtypeStruct((SUBLANE, HD_B), F32)],
        compiler_params=_cparams(("arbitrary",)),
    )(dKr, P, rope, kn)
    return d_qb, d_kb, gqn, gkn


def _merge_prep(dm, Hf, Hb, P, O, mhw):
    TB, nctx = dm.TB, dm.nctx
    lat = lambda c: (lambda i: (i + nctx, c))

    def kern(hf_ref, hb_ref, oa_ref, za_ref, zb_ref, o_ref, w_ref, a_ref, b_ref):
        for h in range(NH_A):
            cols = slice(h * DV_A, (h + 1) * DV_A)
            hs = hf_ref[:, cols] + hb_ref[:, cols]
            rstd = lax.rsqrt(jnp.mean(hs * hs, axis=1, keepdims=True) + EPS)
            za = za_ref[:, cols]
            a_ref[:, cols] = (_sigmoid(oa_ref[:, cols]) * (hs * rstd * w_ref[0:1, cols])
                              * (za * _sigmoid(za))).astype(BF16)
        for h in range(NH_B):
            cols = slice(h * HD_B, (h + 1) * HD_B)
            zb = zb_ref[:, cols]
            b_ref[:, cols] = (o_ref[h] * (zb * _sigmoid(zb))).astype(BF16)

    return pl.pallas_call(
        kern, name="merge_prep", grid=(dm.nlat,),
        in_specs=[pl.BlockSpec((TB, dm.V), lat(0)), pl.BlockSpec((TB, dm.V), lat(0)),
                  pl.BlockSpec((TB, dm.V), lat(dm.off["oa"] // dm.V)),
                  pl.BlockSpec((TB, dm.V), lat(dm.off["za"] // dm.V)),
                  pl.BlockSpec((TB, dm.QB), lat(dm.off["zb"] // dm.QB)),
                  pl.BlockSpec((NH_B, TB, HD_B), lambda i: (0, i, 0)),
                  pl.BlockSpec((SUBLANE, dm.V), lambda i: (0, 0))],
        out_specs=[pl.BlockSpec((TB, dm.V), lambda i: (i, 0)), pl.BlockSpec((TB, dm.QB), lambda i: (i, 0))],
        out_shape=[jax.ShapeDtypeStruct((dm.S, dm.V), BF16), jax.ShapeDtypeStruct((dm.S, dm.QB), BF16)],
        compiler_params=_cparams(("parallel",)),
    )(Hf, Hb, P, P, P, O, mhw)


def _gate_merge(dm, ya, yb, P):
    TB, D, nctx = dm.TB, dm.D, dm.nctx
    gcol = dm.off["g"] // D

    def kern(ya_ref, yb_ref, ga_ref, gb_ref, o_ref):
        o_ref[...] = (_sigmoid(ga_ref[...]) * ya_ref[...] + _sigmoid(gb_ref[...]) * yb_ref[...]).astype(BF16)

    row = pl.BlockSpec((TB, D), lambda i: (i, 0))
    return pl.pallas_call(
        kern, name="gate_merge", grid=(dm.nlat,),
        in_specs=[row, row, pl.BlockSpec((TB, D), lambda i: (i + nctx, gcol)),
                  pl.BlockSpec((TB, D), lambda i: (i + nctx, gcol + 1))],
        out_specs=row, out_shape=jax.ShapeDtypeStruct((dm.S, D), BF16),
        compiler_params=_cparams(("parallel",)),
    )(ya, yb, P, P)


def _final(dm, x, out, tgt, modv, lnp):
    TB, D = dm.TB, dm.D

    def kern(x_ref, o_ref, t_ref, m_ref, p_ref, dr_ref, do_ref, cs_ref, ls_ref):
        i = pl.program_id(0)

        @pl.when(i == 0)
        def _():
            cs_ref[...] = jnp.zeros(cs_ref.shape, F32)
            ls_ref[...] = jnp.zeros(ls_ref.shape, F32)

        gate, lnw, lnb = m_ref[4:5, :], p_ref[0:1, :], p_ref[1:2, :]
        out = o_ref[...]
        xh, rstd = _ln_stats(ALPHA * x_ref[...] + gate * out)
        e = xh * lnw + lnb - t_ref[...]
        ls_ref[...] += 0.5 * jnp.sum(jnp.sum(e * e, axis=1, keepdims=True), axis=0, keepdims=True) / D
        dy = e * (1.0 / D)
        dxh = dy * lnw
        dr = rstd * (dxh - jnp.mean(dxh, axis=1, keepdims=True)
                     - xh * jnp.mean(dxh * xh, axis=1, keepdims=True))
        cs_ref[0:1, :] += jnp.sum(dy * xh, axis=0, keepdims=True)
        cs_ref[1:2, :] += jnp.sum(dy, axis=0, keepdims=True)
        cs_ref[2:3, :] += jnp.sum(dr * out, axis=0, keepdims=True)
        dr_ref[...] = ALPHA * dr
        do_ref[...] = (dr * gate).astype(BF16)

    row = pl.BlockSpec((TB, D), lambda i: (i, 0))
    par = pl.BlockSpec((SUBLANE, D), lambda i: (0, 0))
    return pl.pallas_call(
        kern, name="final_norm_loss", grid=(dm.nlat,),
        in_specs=[row, row, row, par, par],
        out_specs=[row, row, par, pl.BlockSpec((SUBLANE, LANE), lambda i: (0, 0))],
        out_shape=[jax.ShapeDtypeStruct((dm.S, D), F32), jax.ShapeDtypeStruct((dm.S, D), BF16),
                   jax.ShapeDtypeStruct((SUBLANE, D), F32), jax.ShapeDtypeStruct((SUBLANE, LANE), F32)],
        compiler_params=_cparams(("arbitrary",)),
    )(x, out, tgt, modv, lnp)


def _merge_bwd(dm, dM, ya, yb, P):
    TB, D, nctx = dm.TB, dm.D, dm.nctx
    gcol = dm.off["g"] // D
    lat = lambda i: jnp.maximum(i - nctx, 0)

    def kern(dm_ref, ya_ref, yb_ref, ga_ref, gb_ref, da_ref, db_ref, dg_ref):
        i = pl.program_id(0)

        @pl.when(i < nctx)
        def _():
            dg_ref[...] = jnp.zeros(dg_ref.shape, BF16)

        @pl.when(i >= nctx)
        def _():
            d = dm_ref[...]
            sa, sb = _sigmoid(ga_ref[...]), _sigmoid(gb_ref[...])
            da_ref[...] = (d * sa).astype(BF16)
            db_ref[...] = (d * sb).astype(BF16)
            dg_ref[:, 0:D] = (d * ya_ref[...] * sa * (1.0 - sa)).astype(BF16)
            dg_ref[:, D:2 * D] = (d * yb_ref[...] * sb * (1.0 - sb)).astype(BF16)

    row = pl.BlockSpec((TB, D), lambda i: (lat(i), 0))
    return pl.pallas_call(
        kern, name="merge_bwd", grid=(dm.nblk,),
        in_specs=[row, row, row, pl.BlockSpec((TB, D), lambda i: (i, gcol)),
                  pl.BlockSpec((TB, D), lambda i: (i, gcol + 1))],
        out_specs=[row, row, pl.BlockSpec((TB, 2 * D), lambda i: (i, 0))],
        out_shape=[jax.ShapeDtypeStruct((dm.S, D), BF16), jax.ShapeDtypeStruct((dm.S, D), BF16),
                   jax.ShapeDtypeStruct((dm.T, 2 * D), BF16)],
        compiler_params=_cparams(("arbitrary",)),
    )(dM, ya, yb, P, P)


def _branch_bwd(dm, dA, dB, Hf, Hb, P, O, mhw):
    TB, nctx, G = dm.TB, dm.nctx, dm.G
    lat = lambda i: jnp.maximum(i - nctx, 0)

    def kern(da_ref, db_ref, hf_ref, hb_ref, oa_ref, za_ref, zb_ref, o_ref, w_ref,
             doa_ref, dza_ref, dzb_ref, dh_ref, do_ref, del_ref, gw_ref):
        i = pl.program_id(0)

        @pl.when(i == 0)
        def _():
            gw_ref[...] = jnp.zeros(gw_ref.shape, F32)

        @pl.when(i < nctx)
        def _():
            doa_ref[...] = jnp.zeros(doa_ref.shape, BF16)
            dza_ref[...] = jnp.zeros(dza_ref.shape, BF16)
            dzb_ref[...] = jnp.zeros(dzb_ref.shape, BF16)
            dh_ref[...] = jnp.zeros(dh_ref.shape, F32)

        @pl.when(i >= nctx)
        def _():
            for h in range(NH_A):
                cols = slice(h * DV_A, (h + 1) * DV_A)
                hs = hf_ref[:, cols] + hb_ref[:, cols]
                rstd = lax.rsqrt(jnp.mean(hs * hs, axis=1, keepdims=True) + EPS)
                xn = hs * rstd
                w = w_ref[0:1, cols]
                hn = xn * w
                so, za = _sigmoid(oa_ref[:, cols]), za_ref[:, cols]
                sz = _sigmoid(za)
                silu = za * sz
                da = da_ref[:, cols]
                doa_ref[:, cols] = (da * hn * silu * so * (1.0 - so)).astype(BF16)
                dza_ref[:, cols] = (da * hn * so * sz * (1.0 + za * (1.0 - sz))).astype(BF16)
                dhn = da * so * silu
                gw_ref[0:1, cols] += jnp.sum(dhn * xn, axis=0, keepdims=True)
                dxn = dhn * w
                dh_ref[:, cols] = rstd * (dxn - xn * jnp.mean(dxn * xn, axis=1, keepdims=True))
            for h in range(NH_B):
                cols = slice(h * HD_B, (h + 1) * HD_B)
                zb = zb_ref[:, cols]
                sz = _sigmoid(zb)
                db, o = db_ref[:, cols], o_ref[h]
                do = db * (zb * sz)
                do_ref[h] = do.astype(BF16)
                dzb_ref[:, cols] = (db * o * sz * (1.0 + zb * (1.0 - sz))).astype(BF16)
                del_ref[h // G, :, (h % G):(h % G) + 1] = jnp.sum(do * o, axis=1, keepdims=True)

    vlat = pl.BlockSpec((TB, dm.V), lambda i: (lat(i), 0))
    qlat = pl.BlockSpec((TB, dm.QB), lambda i: (lat(i), 0))
    hlat = pl.BlockSpec((NH_B, TB, HD_B), lambda i: (0, lat(i), 0))
    vrow = pl.BlockSpec((TB, dm.V), lambda i: (i, 0))
    qrow = pl.BlockSpec((TB, dm.QB), lambda i: (i, 0))
    pv = lambda n: pl.BlockSpec((TB, dm.V), lambda i: (i, dm.off[n] // dm.V))
    return pl.pallas_call(
        kern, name="branch_bwd", grid=(dm.nblk,),
        in_specs=[vlat, qlat, vrow, vrow, pv("oa"), pv("za"),
                  pl.BlockSpec((TB, dm.QB), lambda i: (i, dm.off["zb"] // dm.QB)), hlat,
                  pl.BlockSpec((SUBLANE, dm.V), lambda i: (0, 0))],
        out_specs=[vrow, vrow, qrow, vrow, hlat,
                   pl.BlockSpec((NKV_B, TB, G), lambda i: (0, lat(i), 0)),
                   pl.BlockSpec((SUBLANE, dm.V), lambda i: (0, 0))],
        out_shape=[jax.ShapeDtypeStruct((dm.T, dm.V), BF16), jax.ShapeDtypeStruct((dm.T, dm.V), BF16),
                   jax.ShapeDtypeStruct((dm.T, dm.QB), BF16), jax.ShapeDtypeStruct((dm.T, dm.V), F32),
                   jax.ShapeDtypeStruct((NH_B, dm.S, HD_B), BF16), jax.ShapeDtypeStruct((NKV_B, dm.S, G), F32),
                   jax.ShapeDtypeStruct((SUBLANE, dm.V), F32)],
        compiler_params=_cparams(("arbitrary",)),
    )(dA, dB, Hf, Hb, P, P, P, O, mhw)


def _conv_bwd(dm, dq, dk, P, convp):
    W = 2 * dm.QK
    col = dm.off["qk"] // W
    kscale = DK_A ** -0.5
    TB = dm.TB

    def kern1(dq_ref, dk_ref, x_ref, p_ref, n_ref, c_ref, dz_ref):
        x = x_ref[...]
        xp, xn = _shifted(dm, x, p_ref, n_ref)
        z = c_ref[3:4, :] + xp * c_ref[0:1, :] + x * c_ref[1:2, :] + xn * c_ref[2:3, :]
        sz = _sigmoid(z)
        dact = jnp.concatenate([dq_ref[...], dk_ref[...] * kscale], axis=1)
        dz_ref[...] = dact * sz * (1.0 + z * (1.0 - sz))

    half = pl.BlockSpec((TB, dm.QK), lambda i: (i, 0))
    par = pl.BlockSpec((SUBLANE, W), lambda i: (0, 0))
    dz = pl.pallas_call(
        kern1, name="conv_bwd_act", grid=(dm.nblk,),
        in_specs=[half, half] + _conv_specs(dm, W, col) + [par],
        out_specs=pl.BlockSpec((TB, W), lambda i: (i, 0)),
        out_shape=jax.ShapeDtypeStruct((dm.T, W), F32),
        compiler_params=_cparams(("parallel",)),
    )(dq, dk, P, P, P, convp)

    def kern2(z_ref, zp_ref, zn_ref, x_ref, p_ref, n_ref, c_ref, dx_ref, cs_ref):
        @pl.when(pl.program_id(0) == 0)
        def _():
            cs_ref[...] = jnp.zeros(cs_ref.shape, F32)

        dz = z_ref[...]
        dzp, dzn = _shifted(dm, dz, zp_ref, zn_ref)
        dx_ref[...] = (dzn * c_ref[0:1, :] + dz * c_ref[1:2, :] + dzp * c_ref[2:3, :]).astype(BF16)
        x = x_ref[...]
        xp, xn = _shifted(dm, x, p_ref, n_ref)
        cs_ref[0:1, :] += jnp.sum(dz * xp, axis=0, keepdims=True)
        cs_ref[1:2, :] += jnp.sum(dz * x, axis=0, keepdims=True)
        cs_ref[2:3, :] += jnp.sum(dz * xn, axis=0, keepdims=True)
        cs_ref[3:4, :] += jnp.sum(dz, axis=0, keepdims=True)

    return pl.pallas_call(
        kern2, name="conv_bwd_taps", grid=(dm.nblk,),
        in_specs=_conv_specs(dm, W, 0) + _conv_specs(dm, W, col) + [par],
        out_specs=[pl.BlockSpec((TB, W), lambda i: (i, 0)), par],
        out_shape=[jax.ShapeDtypeStruct((dm.T, W), BF16), jax.ShapeDtypeStruct((SUBLANE, W), F32)],
        compiler_params=_cparams(("arbitrary",)),
    )(dz, dz, dz, P, P, P, convp)


def _gates_bwd(dm, dG, G):
    TB = dm.TB

    def kern(d_ref, g_ref, o_ref, cs_ref):
        @pl.when(pl.program_id(0) == 0)
        def _():
            cs_ref[...] = jnp.zeros(cs_ref.shape, F32)

        lane = lax.broadcasted_iota(jnp.int32, (TB, LANE), 1)
        is_f = ((lane // NH_A) % 2) == 1
        d = d_ref[...]
        dpre = jnp.where(lane < dm.NIF, jnp.where(is_f, d * (1.0 - jnp.exp(g_ref[...])), d), 0.0)
        cs_ref[0:1, :] += jnp.sum(dpre, axis=0, keepdims=True)
        if dm.IFP > LANE:
            o_ref[:, LANE:] = jnp.zeros((TB, dm.IFP - LANE), BF16)
        o_ref[:, 0:LANE] = dpre.astype(BF16)

    row = pl.BlockSpec((TB, LANE), lambda i: (i, 0))
    return pl.pallas_call(
        kern, name="gates_bwd", grid=(dm.nblk,),
        in_specs=[row, row],
        out_specs=[pl.BlockSpec((TB, dm.IFP), lambda i: (i, 0)), pl.BlockSpec((SUBLANE, LANE), lambda i: (0, 0))],
        out_shape=[jax.ShapeDtypeStruct((dm.T, dm.IFP), BF16), jax.ShapeDtypeStruct((SUBLANE, LANE), F32)],
        compiler_params=_cparams(("arbitrary",)),
    )(dG, G)


def _ln_mod_bwd(dm, dU, xcat, modv, dr_a):
    TB, D, nctx = dm.TB, dm.D, dm.nctx
    lat = lambda i: jnp.maximum(i - nctx, 0)

    def kern(du_ref, x_ref, m_ref, dr_ref, gx_ref, cs_ref):
        i = pl.program_id(0)
        is_ctx = i < nctx

        @pl.when(i == 0)
        def _():
            cs_ref[...] = jnp.zeros(cs_ref.shape, F32)

        xh, rstd = _ln_stats(x_ref[...])
        du = du_ref[...]
        s_shift = jnp.sum(du, axis=0, keepdims=True)
        s_scale = jnp.sum(du * xh, axis=0, keepdims=True)
        cs_ref[0:1, :] += jnp.where(is_ctx, 0.0, s_shift)
        cs_ref[1:2, :] += jnp.where(is_ctx, 0.0, s_scale)
        cs_ref[2:3, :] += jnp.where(is_ctx, s_shift, 0.0)
        cs_ref[3:4, :] += jnp.where(is_ctx, s_scale, 0.0)

        @pl.when(i >= nctx)
        def _():
            dxh = du * (1.0 + m_ref[1:2, :])
            gx_ref[...] = dr_ref[...] + rstd * (dxh - jnp.mean(dxh, axis=1, keepdims=True)
                                                - xh * jnp.mean(dxh * xh, axis=1, keepdims=True))

    row = pl.BlockSpec((TB, D), lambda i: (i, 0))
    lrow = pl.BlockSpec((TB, D), lambda i: (lat(i), 0))
    par = pl.BlockSpec((SUBLANE, D), lambda i: (0, 0))
    return pl.pallas_call(
        kern, name="ln_mod_bwd", grid=(dm.nblk,),
        in_specs=[row, row, par, lrow],
        out_specs=[lrow, par],
        out_shape=[jax.ShapeDtypeStruct((dm.S, D), F32), jax.ShapeDtypeStruct((SUBLANE, D), F32)],
        compiler_params=_cparams(("arbitrary",)),
    )(dU, xcat, modv, dr_a)


def _mod_fwd(craw, w_loc, b_loc):
    R, D = craw.shape
    n = w_loc.shape[1]

    def kern(c_ref, w_ref, b_ref, o_ref):
        c = c_ref[...]
        o_ref[...] = _dot((c * _sigmoid(c)).astype(BF16), w_ref[...].astype(BF16)) + b_ref[0:1, :]

    return pl.pallas_call(
        kern, name="mod_fwd", out_shape=jax.ShapeDtypeStruct((R, n), F32),
        compiler_params=pltpu.CompilerParams(vmem_limit_bytes=VMEM_LIMIT),
    )(craw, w_loc, b_loc)


def _mod_bwd(crawT, dmod, w_loc):
    D, R = crawT.shape
    n = w_loc.shape[1]

    def kern(c_ref, d_ref, w_ref, gw_ref, dc_ref):
        c = c_ref[...]
        d = d_ref[...].astype(BF16)
        gw_ref[...] = _dot((c * _sigmoid(c)).astype(BF16), d)
        dc_ref[...] = _dot_nt(d, w_ref[...].astype(BF16))

    return pl.pallas_call(
        kern, name="mod_bwd",
        out_shape=[jax.ShapeDtypeStruct((D, n), F32), jax.ShapeDtypeStruct((R, D), F32)],
        compiler_params=pltpu.CompilerParams(vmem_limit_bytes=VMEM_LIMIT),
    )(crawT, dmod, w_loc)


def _cctx_grad(dsilu, c_ctx):
    def kern(p_ref, c_ref, o_ref):
        c = c_ref[...]
        sc = _sigmoid(c)
        o_ref[...] = p_ref[...] * (sc * (1.0 + c * (1.0 - sc)))

    return pl.pallas_call(kern, name="cctx_grad", out_shape=jax.ShapeDtypeStruct(c_ctx.shape, F32))(dsilu, c_ctx)


def _adamw(w, g, m, v, name):
    R, C = w.shape
    tb = _tile(R, max(SUBLANE, (1 << 19) // (4 * C) // SUBLANE * SUBLANE), SUBLANE)
    c1 = 1.0 / (1.0 - ADAM_B1 ** ADAM_STEP)
    c2 = 1.0 / (1.0 - ADAM_B2 ** ADAM_STEP)

    def kern(w_ref, g_ref, m_ref, v_ref, d_ref, nm_ref, nv_ref):
        g = g_ref[...]
        nm = ADAM_B1 * m_ref[...] + (1.0 - ADAM_B1) * g
        nv = ADAM_B2 * v_ref[...] + (1.0 - ADAM_B2) * (g * g)
        nm_ref[...] = nm
        nv_ref[...] = nv
        d_ref[...] = -ADAM_LR * ((nm * c1) / (jnp.sqrt(nv * c2) + ADAM_EPS) + ADAM_WD * w_ref[...])

    spec = pl.BlockSpec((tb, C), lambda i: (i, 0))
    return pl.pallas_call(
        kern, name=name, grid=(R // tb,), in_specs=[spec] * 4, out_specs=[spec] * 3,
        out_shape=[jax.ShapeDtypeStruct((R, C), F32)] * 3,
        compiler_params=_cparams(("parallel",)),
    )(w, g, m, v)


def _rope_tables(dm):
    half = HD_B // 2
    rows_n = dm.S // GRID_W
    row = jnp.repeat(jnp.arange(rows_n), GRID_W).astype(F32)
    col = jnp.tile(jnp.arange(GRID_W), rows_n).astype(F32)
    inv = ROPE_THETA ** (-jnp.arange(0, half, 2, dtype=F32) / half)
    ar, ac = row[:, None] * inv[None], col[:, None] * inv[None]
    cos = jnp.concatenate([jnp.cos(ar), jnp.cos(ar), jnp.cos(ac), jnp.cos(ac)], axis=1)
    zr = jnp.zeros_like(ar)
    sa = jnp.concatenate([-jnp.sin(ar), zr, -jnp.sin(ac), zr], axis=1)
    sb = jnp.concatenate([zr, jnp.sin(ar), zr, jnp.sin(ac)], axis=1)
    ctx = jnp.stack([jnp.ones((dm.Tc, HD_B), F32), jnp.zeros((dm.Tc, HD_B), F32), jnp.zeros((dm.Tc, HD_B), F32)])
    return jnp.concatenate([ctx, jnp.stack([cos, sa, sb])], axis=1)


def _rows8(*rows, width):
    out = [jnp.pad(r.reshape(-1).astype(F32), (0, width - r.size)) for r in rows]
    n = -(-len(out) // SUBLANE) * SUBLANE
    out += [jnp.zeros((width,), F32)] * (n - len(out))
    return jnp.stack(out)


def _to_padded(dm, w_full):
    cols, o = {}, 0
    for n, wd in zip(dm.ref_names, dm.ref_widths):
        cols[n] = w_full[:, o:o + wd]
        o += wd
    cols["if"] = jnp.pad(cols["if"], ((0, 0), (0, dm.IFP - dm.NIF)))
    return jnp.concatenate([cols[n] for n in dm.order], axis=1)


def _from_padded(dm, w_pad):
    return jnp.concatenate(
        [w_pad[:, dm.off[n]:dm.off[n] + wd] for n, wd in zip(dm.ref_names, dm.ref_widths)], axis=1)


def kernel(x, c, ctx, c_ctx, w_mod, b_mod, w_in, b_if, conv_w, conv_b, mh_norm_w, q_norm_w, k_norm_w, w_branch_a, w_branch_b, w_out, ln_w, ln_b, loss_target, m_c_ctx, m_w_mod, m_b_mod, m_w_in, m_b_if, m_conv_w, m_conv_b, m_mh_norm_w, m_q_norm_w, m_k_norm_w, m_w_branch_a, m_w_branch_b, m_w_out, m_ln_w, m_ln_b, v_c_ctx, v_w_mod, v_b_mod, v_w_in, v_b_if, v_conv_w, v_conv_b, v_mh_norm_w, v_q_norm_w, v_k_norm_w, v_w_branch_a, v_w_branch_b, v_w_out, v_ln_w, v_ln_b):
    S, D = x.shape[1], x.shape[2]
    Tc = ctx.shape[1]
    dm = Dims(S, Tc, D)
    T, QK2 = dm.T, 2 * dm.QK
    me = 4 * lax.axis_index("x") + 2 * lax.axis_index("y") + lax.axis_index("c")
    n_mod = w_mod.shape[2]
    n_in = w_in.shape[2]
    n_cv = conv_w.shape[2]
    rb = w_out.shape[1]

    pack0 = _all_gather(_rows8(c[0], conv_w[0, 0], conv_w[0, 1], conv_w[0, 2], width=D), "ag_cond", False)
    c_all = pack0[:, 0, :]
    conv_full = jnp.transpose(pack0[:, 1:4, :n_cv], (1, 0, 2)).reshape(CONV_W, QK2)
    convp = _rows8(conv_full[0], conv_full[1], conv_full[2], conv_b[0], width=QK2)

    w_in_all = _all_gather(w_in[0].astype(BF16), "ag_w_in", True)
    Wp = _to_padded(dm, jnp.transpose(w_in_all, (1, 0, 2)).reshape(D, dm.N_IN))
    wsq = jnp.concatenate([w_branch_a[0], w_branch_b[0], w_out[0]], axis=0).astype(BF16)

    craw = _rows8(*[c_all[j] for j in range(N_DEV)], c_ctx, width=D)
    b_loc = _rows8(lax.dynamic_slice(b_mod[0], (me * n_mod,), (n_mod,)), width=n_mod)
    mod_all = _all_gather(_mod_fwd(craw, w_mod[0], b_loc), "ag_mod", False)
    mod_rows = jnp.transpose(mod_all, (1, 0, 2)).reshape(2 * SUBLANE, 3 * D)
    mod_me = lax.dynamic_slice(mod_rows, (me, 0), (1, 3 * D))[0]
    mod_cx = mod_rows[N_DEV]
    modv = _rows8(mod_me[0:D], mod_me[D:2 * D], mod_cx[0:D], mod_cx[D:2 * D], mod_me[2 * D:3 * D], width=D)

    xcat = jnp.concatenate([ctx[0], x[0]], axis=0)
    U = _ln_mod_fwd(dm, xcat, modv)
    P, wsq_all = _mm(U, Wp, "mm_in_proj", tn=896, exchange=("ag", wsq))
    Wba = wsq_all[:, 0:rb, :].reshape(dm.V, D)
    Wbb = wsq_all[:, rb:2 * rb, :].reshape(dm.QB, D)
    Wout = wsq_all[:, 2 * rb:3 * rb, :].reshape(D, D)
    qk_act = _conv_silu_fwd(dm, P, convp)
    G = _gates_fwd(dm, P, _rows8(b_if[0], width=LANE))
    Hf, Cf, NMf = _mlstm_fwd(dm, qk_act, P, G, False)
    Hb, Cb, NMb = _mlstm_fwd(dm, qk_act, P, G, True)
    rope = _rope_tables(dm)
    qn, kn = _rows8(q_norm_w[0], width=HD_B), _rows8(k_norm_w[0], width=HD_B)
    Qr, Kr, Vb = _qk_prep(dm, P, rope, qn, kn)
    O, LSE = _attn_fwd(dm, Qr, Kr, Vb)
    mhw = _rows8(mh_norm_w[0], width=dm.V)
    A_in, B_in = _merge_prep(dm, Hf, Hb, P, O, mhw)
    ya = _mm(A_in, Wba, "mm_branch_a")
    yb = _mm(B_in, Wbb, "mm_branch_b")
    M_in = _gate_merge(dm, ya, yb, P)
    out = _mm(M_in, Wout, "mm_out")
    lnp = _rows8(ln_w[0], ln_b[0], width=D)
    dr_a, d_out, cs_fin, loss_p = _final(dm, x[0], out, loss_target[0], modv, lnp)
    loss = lax.psum(loss_p[0, 0], ("x", "y", "c"))

    dM = _mm(d_out, Wout, "mm_d_merge", nt=True)
    gWout = _mm(M_in.T, d_out, "mm_g_w_out", tk=2048)
    d_ya, d_yb, d_g = _merge_bwd(dm, dM, ya, yb, P)
    dA = _mm(d_ya, Wba, "mm_d_a", nt=True)
    gWba = _mm(A_in.T, d_ya, "mm_g_w_ba", tk=2048)
    dB = _mm(d_yb, Wbb, "mm_d_b", nt=True)
    gWbb = _mm(B_in.T, d_yb, "mm_g_w_bb", tk=2048)
    d_oa, d_za, d_zb, dH, dO, DEL, gmh = _branch_bwd(dm, dA, dB, Hf, Hb, P, O, mhw)
    dQr = _attn_bwd_dq(dm, Qr, Kr, Vb, dO, LSE, DEL)
    dKr, d_vb = _attn_bwd_dkv(dm, Qr, Kr, Vb, dO, jnp.transpose(LSE, (0, 2, 1)), jnp.transpose(DEL, (0, 2, 1)))
    d_qb, d_kb, gqn, gkn = _qk_bwd(dm, dQr, dKr, P, rope, qn, kn)
    acc = _mlstm_bwd(dm, qk_act, P, G, Cf, NMf, Hf, dH, None, False)
    dq, dk, dv, dG = _mlstm_bwd(dm, qk_act, P, G, Cb, NMb, Hb, dH, acc, True)
    d_qk, cs_conv = _conv_bwd(dm, dq, dk, P, convp)
    d_if, gbif = _gates_bwd(dm, dG, G)
    parts = {"g": d_g, "qk": d_qk, "va": dv.astype(BF16), "oa": d_oa, "za": d_za, "qb": d_qb,
             "zb": d_zb, "kb": d_kb, "vb": d_vb, "if": d_if}
    dP = jnp.concatenate([parts[n] for n in dm.order], axis=1)
    gsq = jnp.concatenate([gWba.reshape(N_DEV, rb, D), gWbb.reshape(N_DEV, rb, D),
                           gWout.reshape(N_DEV, rb, D)], axis=1).astype(BF16)
    gWp, gsq_all = _mm(U.T, dP, "mm_g_w_in", tm=1024, tn=896, tk=2816, exchange=("a2a", gsq))
    gW = _from_padded(dm, gWp).astype(BF16)
    gW = jnp.transpose(gW.reshape(D, N_DEV, n_in), (1, 0, 2))
    dU, gW_all = _mm(dP, Wp, "mm_d_u", nt=True, tk=1792, exchange=("a2a", gW))
    grad_x, cs_ln = _ln_mod_bwd(dm, dU, xcat, modv, dr_a)

    dmod_me = _rows8(jnp.concatenate([cs_ln[0], cs_ln[1], cs_fin[2]]),
                     jnp.concatenate([cs_ln[2], cs_ln[3], jnp.zeros((D,), F32)]), width=3 * D)
    dmod_all = _all_gather(dmod_me, "ag_dmod", False)
    dmod_loc = lax.dynamic_slice(dmod_all, (0, 0, me * n_mod), (N_DEV, 2, n_mod))
    dmod_rows = _rows8(*[dmod_loc[j, 0] for j in range(N_DEV)], jnp.sum(dmod_loc[:, 1, :], axis=0), width=n_mod)
    g_w_mod, dc_part = _mod_bwd(craw.T, dmod_rows, w_mod[0])

    PW = dm.PW
    small = _rows8(cs_fin[0], cs_fin[1], gmh[0], cs_conv[3], cs_conv[0], cs_conv[1], cs_conv[2],
                   dmod_me[0, 0:D], dmod_me[0, D:2 * D], dmod_me[0, 2 * D:3 * D],
                   dmod_me[1, 0:D], dmod_me[1, D:2 * D],
                   jnp.concatenate([gqn[0], gkn[0], gbif[0]]), dc_part[N_DEV], width=PW)
    tot = _sum_slots(_all_gather(small, "ag_small", False), "sum_small")
    g_ln_w, g_ln_b, g_mh, g_conv_b = tot[0, :D], tot[1, :D], tot[2, :dm.V], tot[3, :QK2]
    g_conv_full = tot[4:7, :QK2]
    g_b_mod = jnp.concatenate([tot[7, :D] + tot[10, :D], tot[8, :D] + tot[11, :D], tot[9, :D]])
    g_qn, g_kn, g_bif = tot[12, 0:HD_B], tot[12, HD_B:2 * HD_B], tot[12, 2 * HD_B:2 * HD_B + dm.NIF]
    g_c_ctx = _cctx_grad(tot[13:14, :D], c_ctx.reshape(1, D))[0]
    g_conv_w = lax.dynamic_slice(g_conv_full, (0, me * n_cv), (CONV_W, n_cv))

    g_w_in = _sum_slots(gW_all, "sum_g_w_in")
    g_sq = _sum_slots(gsq_all, "sum_g_w_sq")

    d_in, nm_in, nv_in = _adamw(w_in[0], g_w_in, m_w_in[0], v_w_in[0], "adam_w_in")
    d_md, nm_md, nv_md = _adamw(w_mod[0], g_w_mod, m_w_mod[0], v_w_mod[0], "adam_w_mod")
    cat3 = lambda a, b, cc: jnp.concatenate([a[0], b[0], cc[0]], axis=0)
    d_sq, nm_sq, nv_sq = _adamw(cat3(w_branch_a, w_branch_b, w_out), g_sq,
                                cat3(m_w_branch_a, m_w_branch_b, m_w_out),
                                cat3(v_w_branch_a, v_w_branch_b, v_w_out), "adam_w_sq")
    names = ["c_ctx", "b_mod", "b_if", "conv_w", "conv_b", "mh", "qn", "kn", "ln_w", "ln_b"]
    ws = [c_ctx, b_mod, b_if, conv_w, conv_b, mh_norm_w, q_norm_w, k_norm_w, ln_w, ln_b]
    ms = [m_c_ctx, m_b_mod, m_b_if, m_conv_w, m_conv_b, m_mh_norm_w, m_q_norm_w, m_k_norm_w, m_ln_w, m_ln_b]
    vs = [v_c_ctx, v_b_mod, v_b_if, v_conv_w, v_conv_b, v_mh_norm_w, v_q_norm_w, v_k_norm_w, v_ln_w, v_ln_b]
    gs = [g_c_ctx, g_b_mod, g_bif, g_conv_w, g_conv_b, g_mh, g_qn, g_kn, g_ln_w, g_ln_b]
    sizes = [a.size for a in ws]
    tot_n = sum(sizes)
    padn = -(-tot_n // LANE) * LANE
    flat = lambda arrs: jnp.pad(jnp.concatenate([a.reshape(-1) for a in arrs]), (0, padn - tot_n)).reshape(1, padn)
    d_s, nm_s, nv_s = _adamw(flat(ws), flat(gs), flat(ms), flat(vs), "adam_small")

    def split(a):
        res, o = {}, 0
        for n, wv, sz in zip(names, ws, sizes):
            res[n] = a[0, o:o + sz].reshape(wv.shape)
            o += sz
        return res

    def assemble(small_d, big_in, big_md, big_sq):
        s = small_d
        return [s["c_ctx"], big_md[None], s["b_mod"], big_in[None], s["b_if"], s["conv_w"], s["conv_b"],
                s["mh"], s["qn"], s["kn"], big_sq[None, 0:rb], big_sq[None, rb:2 * rb], big_sq[None, 2 * rb:3 * rb],
                s["ln_w"], s["ln_b"]]

    g_small = {n: g.reshape(wv.shape) for n, g, wv in zip(names, gs, ws)}
    grads = assemble(g_small, g_w_in, g_w_mod, g_sq)
    deltas = assemble(split(d_s), d_in, d_md, d_sq)
    new_m = assemble(split(nm_s), nm_in, nm_md, nm_sq)
    new_v = assemble(split(nv_s), nv_in, nv_md, nv_sq)
    return (loss, grad_x[None], *grads, *deltas, *new_m, *new_v)
```

```python
import jax
import jax.numpy as jnp
from jax import lax
from jax.experimental import pallas as pl
from jax.experimental.pallas import tpu as pltpu

F32 = jnp.float32
BF16 = jnp.bfloat16
MESH = pl.DeviceIdType.MESH
N_DEV = 8

GRID_W = 64
NH_A = 8
DK_A = 128
DV_A = 256
CONV_W = 3
CHUNK = 64
M_INIT = -1e30
NH_B = 16
NKV_B = 4
HD_B = 128
ROPE_THETA = 10000.0
EPS = 1e-6
DEPTH = 1
ALPHA = (2 * DEPTH) ** 0.25
ADAM_LR = 0.001
ADAM_B1 = 0.9
ADAM_B2 = 0.999
ADAM_EPS = 1e-08
ADAM_WD = 0.01
ADAM_STEP = 10

LANE = 128
SUBLANE = 8
VMEM_LIMIT = 56 << 20


def _tile(n, target, align):
    best = None
    t = align
    while t <= min(n, target):
        if n % t == 0:
            best = t
        t += align
    return best if best is not None else n


class Dims:
    def __init__(self, S, Tc, D):
        self.S, self.Tc, self.D = S, Tc, D
        self.T = S + Tc
        self.QK = NH_A * DK_A
        self.V = NH_A * DV_A
        self.QB = NH_B * HD_B
        self.KVB = NKV_B * HD_B
        self.G = NH_B // NKV_B
        self.NIF = 4 * NH_A
        self.IFP = 512 if self.KVB % 512 == 0 else LANE
        self.ref_widths = [2 * self.QK, self.V, self.NIF, self.KVB, self.KVB,
                           self.V, self.V, self.QB, self.QB, 2 * D]
        self.ref_names = ["qk", "va", "if", "kb", "vb", "oa", "za", "qb", "zb", "g"]
        self.N_IN = sum(self.ref_widths)
        self.order = ["g", "qk", "va", "oa", "za", "qb", "zb", "kb", "vb", "if"]
        w = dict(zip(self.ref_names, self.ref_widths))
        w["if"] = self.IFP
        self.w = w
        self.off = {}
        o = 0
        for n in self.order:
            assert o % w[n] == 0, (n, o, w[n])
            self.off[n] = o
            o += w[n]
        self.NP = o
        self.TB = min(256, Tc)
        assert Tc % self.TB == 0 and S % self.TB == 0 and self.TB % CHUNK == 0
        self.nctx = Tc // self.TB
        self.nlat = S // self.TB
        self.nblk = self.nctx + self.nlat
        self.PW = max(D, self.V, 2 * self.QK, 3 * LANE)


def _cparams(sem):
    return pltpu.CompilerParams(dimension_semantics=sem, vmem_limit_bytes=VMEM_LIMIT)


def _sigmoid(x):
    return 1.0 / (1.0 + jnp.exp(-x))


def _my_pos():
    return lax.axis_index("x"), lax.axis_index("y"), lax.axis_index("c")


def _all_gather(x, name, big):
    R, C = x.shape
    space = pl.ANY if big else pltpu.VMEM

    def body(x_ref, out_ref, send_sems, recv_sems, local_sem):
        px, py, pc = _my_pos()
        me, sibling = (px, py, pc), (px, py, 1 - pc)
        chips = [(1 - px, py), (px, 1 - py), (1 - px, 1 - py)]

        def slot(bx, by, bc):
            return out_ref.at[4 * bx + 2 * by + bc]

        def copy(k, block, to, src=None):
            return pltpu.make_async_remote_copy(
                src_ref=slot(*block) if src is None else src, dst_ref=slot(*block),
                send_sem=send_sems.at[k], recv_sem=recv_sems.at[k],
                device_id=to, device_id_type=MESH)

        mine = pltpu.make_async_copy(x_ref, slot(*me), local_sem)
        mine.start()
        first = [copy(0, me, sibling, src=x_ref)]
        first += [copy(1 + j, me, (*chip, pc), src=x_ref) for j, chip in enumerate(chips)]
        for cp in first:
            cp.start()
        passed = [copy(4 + j, (*chip, pc), sibling) for j, chip in enumerate(chips)]
        for j, chip in enumerate(chips):
            copy(1 + j, (*chip, pc), me).wait_recv()
            passed[j].start()
        copy(0, sibling, me).wait_recv()
        for j, chip in enumerate(chips):
            copy(4 + j, (*chip, 1 - pc), me).wait_recv()
        for cp in first + passed:
            cp.wait_send()
        mine.wait()

    return pl.pallas_call(
        body, name=name,
        out_shape=jax.ShapeDtypeStruct((N_DEV, R, C), x.dtype),
        in_specs=[pl.BlockSpec(memory_space=space)],
        out_specs=pl.BlockSpec(memory_space=space),
        scratch_shapes=[pltpu.SemaphoreType.DMA((7,)), pltpu.SemaphoreType.DMA((7,)),
                        pltpu.SemaphoreType.DMA],
    )(x)


EXCHANGE_SEMS = [pltpu.SemaphoreType.DMA((N_DEV - 1,)), pltpu.SemaphoreType.DMA((N_DEV - 1,)),
                 pltpu.SemaphoreType.DMA]


def _exchange(kind, src_ref, land_ref, send_sems, recv_sems, local_sem):
    def copies():
        px, py, pc = _my_pos()
        me = 4 * px + 2 * py + pc
        own = src_ref if kind == "ag" else src_ref.at[me]
        local = pltpu.make_async_copy(own, land_ref.at[me], local_sem)
        sends, recvs = [], []
        for r in range(1, N_DEV):
            dx, dy, dc = (r >> 2) & 1, (r >> 1) & 1, r & 1
            qx = px if dx == 0 else 1 - px
            qy = py if dy == 0 else 1 - py
            qc = pc if dc == 0 else 1 - pc
            peer = 4 * qx + 2 * qy + qc
            sems = dict(send_sem=send_sems.at[r - 1], recv_sem=recv_sems.at[r - 1],
                        device_id=(qx, qy, qc), device_id_type=MESH)
            sends.append(pltpu.make_async_remote_copy(
                src_ref=src_ref if kind == "ag" else src_ref.at[peer], dst_ref=land_ref.at[me], **sems))
            recvs.append(pltpu.make_async_remote_copy(src_ref=own, dst_ref=land_ref.at[peer], **sems))
        return local, sends, recvs

    def start():
        local, sends, _ = copies()
        local.start()
        for cp in sends:
            cp.start()

    def wait():
        local, sends, recvs = copies()
        for cp in recvs:
            cp.wait_recv()
        for cp in sends:
            cp.wait_send()
        local.wait()

    return start, wait


def _land_shape(kind, src):
    return jax.ShapeDtypeStruct(src.shape if kind == "a2a" else (N_DEV,) + src.shape, src.dtype)


def _sum_slots(a, name):
    _, R, C = a.shape
    tb = _tile(R, max(SUBLANE, (1 << 20) // (4 * C) // SUBLANE * SUBLANE), SUBLANE)

    def kern(a_ref, o_ref):
        acc = a_ref[0].astype(F32)
        for j in range(1, N_DEV):
            acc = acc + a_ref[j].astype(F32)
        o_ref[...] = acc

    return pl.pallas_call(
        kern, name=name, grid=(R // tb,),
        in_specs=[pl.BlockSpec((N_DEV, tb, C), lambda i: (0, i, 0))],
        out_specs=pl.BlockSpec((tb, C), lambda i: (i, 0)),
        out_shape=jax.ShapeDtypeStruct((R, C), F32),
        compiler_params=_cparams(("parallel",)),
    )(a)


def _mm(a, b, name, nt=False, tm=768, tn=1024, tk=2048, out_dtype=F32, exchange=None):
    M, K = a.shape
    N = b.shape[0] if nt else b.shape[1]
    assert (b.shape[1] if nt else b.shape[0]) == K
    tm, tn, tk = _tile(M, tm, 16), _tile(N, tn, LANE), _tile(K, tk, LANE)
    ni, nj, nk = M // tm, N // tn, K // tk

    def dot(x, y):
        if nt:
            return lax.dot_general(x, y, (((1,), (1,)), ((), ())), preferred_element_type=F32)
        return jnp.dot(x, y, preferred_element_type=F32)

    def kern(a_ref, b_ref, *rest):
        if exchange is not None:
            src_ref, o_ref, land_ref, acc_ref, send_sems, recv_sems, local_sem = rest
            start, wait = _exchange(exchange[0], src_ref, land_ref, send_sems, recv_sems, local_sem)
            i, j, kk = pl.program_id(0), pl.program_id(1), pl.program_id(2)
            pl.when((i == 0) & (j == 0) & (kk == 0))(start)
        else:
            o_ref, acc_ref = rest
        k = pl.program_id(2)
        part = dot(a_ref[...], b_ref[...])
        if nk == 1:
            o_ref[...] = part.astype(o_ref.dtype)
        else:
            @pl.when(k == 0)
            def _():
                acc_ref[...] = part

            @pl.when(k > 0)
            def _():
                acc_ref[...] += part

            @pl.when(k == nk - 1)
            def _():
                o_ref[...] = acc_ref[...].astype(o_ref.dtype)

        if exchange is not None:
            pl.when((i == ni - 1) & (j == nj - 1) & (kk == nk - 1))(wait)

    b_spec = (pl.BlockSpec((tn, tk), lambda i, j, k: (j, k)) if nt
              else pl.BlockSpec((tk, tn), lambda i, j, k: (k, j)))
    in_specs = [pl.BlockSpec((tm, tk), lambda i, j, k: (i, k)), b_spec]
    out_specs = pl.BlockSpec((tm, tn), lambda i, j, k: (i, j))
    out_shape = jax.ShapeDtypeStruct((M, N), out_dtype)
    scratch = [pltpu.VMEM((tm, tn) if nk > 1 else (SUBLANE, LANE), F32)]
    args = (a, b)
    sem = ("parallel", "parallel", "arbitrary")
    if exchange is not None:
        in_specs = in_specs + [pl.BlockSpec(memory_space=pl.ANY)]
        out_specs = [out_specs, pl.BlockSpec(memory_space=pl.ANY)]
        out_shape = [out_shape, _land_shape(*exchange)]
        scratch = scratch + EXCHANGE_SEMS
        args = (a, b, exchange[1])
        sem = ("arbitrary", "arbitrary", "arbitrary")
    return pl.pallas_call(
        kern, name=name, grid=(ni, nj, nk), in_specs=in_specs, out_specs=out_specs,
        out_shape=out_shape, scratch_shapes=scratch, compiler_params=_cparams(sem),
    )(*args)


def _ln_stats(x):
    mu = jnp.mean(x, axis=-1, keepdims=True)
    xc = x - mu
    var = jnp.mean(xc * xc, axis=-1, keepdims=True)
    rstd = lax.rsqrt(var + EPS)
    return xc * rstd, rstd


def _ln_mod_fwd(dm, xcat, modv):
    TB, D, nctx = dm.TB, dm.D, dm.nctx

    def kern(x_ref, m_ref, u_ref):
        is_ctx = pl.program_id(0) < nctx
        xh, _ = _ln_stats(x_ref[...])
        shift = jnp.where(is_ctx, m_ref[2:3, :], m_ref[0:1, :])
        scale = jnp.where(is_ctx, m_ref[3:4, :], m_ref[1:2, :])
        u_ref[...] = (xh * (1.0 + scale) + shift).astype(BF16)

    return pl.pallas_call(
        kern, name="ln_mod_fwd", grid=(dm.nblk,),
        in_specs=[pl.BlockSpec((TB, D), lambda i: (i, 0)), pl.BlockSpec((SUBLANE, D), lambda i: (0, 0))],
        out_specs=pl.BlockSpec((TB, D), lambda i: (i, 0)),
        out_shape=jax.ShapeDtypeStruct((dm.T, D), BF16),
        compiler_params=_cparams(("parallel",)),
    )(xcat, modv)


def _conv_specs(dm, W, col):
    TB, T = dm.TB, dm.T
    r8 = TB // SUBLANE
    last8 = T // SUBLANE - 1
    return [pl.BlockSpec((TB, W), lambda i: (i, col)),
            pl.BlockSpec((SUBLANE, W), lambda i: (jnp.maximum(i * r8 - 1, 0), col)),
            pl.BlockSpec((SUBLANE, W), lambda i: (jnp.minimum((i + 1) * r8, last8), col))]


def _shifted(dm, x, prev_ref, next_ref):
    TB, nctx, nblk = dm.TB, dm.nctx, dm.nblk
    i = pl.program_id(0)
    row = lax.broadcasted_iota(jnp.int32, x.shape, 0)
    zero_prev = (i == 0) | (i == nctx)
    zero_next = (i == nctx - 1) | (i == nblk - 1)
    before = jnp.where(zero_prev, 0.0, prev_ref[SUBLANE - 1:SUBLANE, :])
    after = jnp.where(zero_next, 0.0, next_ref[0:1, :])
    xp = jnp.where(row == 0, before, pltpu.roll(x, 1, 0))
    xn = jnp.where(row == TB - 1, after, pltpu.roll(x, TB - 1, 0))
    return xp, xn


def _conv_silu_fwd(dm, P, convp):
    W = 2 * dm.QK
    col = dm.off["qk"] // W
    kscale = DK_A ** -0.5

    def kern(x_ref, p_ref, n_ref, c_ref, o_ref):
        x = x_ref[...]
        xp, xn = _shifted(dm, x, p_ref, n_ref)
        z = c_ref[3:4, :] + xp * c_ref[0:1, :] + x * c_ref[1:2, :] + xn * c_ref[2:3, :]
        lane = lax.broadcasted_iota(jnp.int32, (1, W), 1)
        cs = jnp.where(lane >= dm.QK, kscale, 1.0)
        o_ref[...] = z * _sigmoid(z) * cs

    return pl.pallas_call(
        kern, name="conv_silu_fwd", grid=(dm.nblk,),
        in_specs=_conv_specs(dm, W, col) + [pl.BlockSpec((SUBLANE, W), lambda i: (0, 0))],
        out_specs=pl.BlockSpec((dm.TB, W), lambda i: (i, 0)),
        out_shape=jax.ShapeDtypeStruct((dm.T, W), F32),
        compiler_params=_cparams(("parallel",)),
    )(P, P, P, convp)


def _gates_fwd(dm, P, bif):
    col = dm.off["if"] // LANE

    def kern(x_ref, b_ref, o_ref):
        x = x_ref[...] + b_ref[0:1, :]
        lane = lax.broadcasted_iota(jnp.int32, x.shape, 1)
        is_f = ((lane // NH_A) % 2) == 1
        ls = jnp.minimum(x, 0.0) - jnp.log(1.0 + jnp.exp(-jnp.abs(x)))
        o_ref[...] = jnp.where(lane < dm.NIF, jnp.where(is_f, ls, x), 0.0)

    return pl.pallas_call(
        kern, name="gates_fwd", grid=(dm.nblk,),
        in_specs=[pl.BlockSpec((dm.TB, LANE), lambda i: (i, col)),
                  pl.BlockSpec((SUBLANE, LANE), lambda i: (0, 0))],
        out_specs=pl.BlockSpec((dm.TB, LANE), lambda i: (i, 0)),
        out_shape=jax.ShapeDtypeStruct((dm.T, LANE), F32),
        compiler_params=_cparams(("parallel",)),
    )(P, bif)


def _mlstm_order(dm, reverse, backward):
    nctx, nblk = dm.nctx, dm.nblk

    def idx(i):
        if backward:
            i = nblk - 1 - i
        if not reverse:
            return i
        return jnp.where(i < nctx, nctx - 1 - i, nblk - 1 - (i - nctx))

    return idx


def _chunk_gates(g, ci, cf, mask_f, maskT_f, eye_f):
    lane = lax.broadcasted_iota(jnp.int32, g.shape, 1)
    gi_c = jnp.sum(jnp.where(lane == ci, g, 0.0), axis=1, keepdims=True)
    gf_c = jnp.sum(jnp.where(lane == cf, g, 0.0), axis=1, keepdims=True)
    gi_r = jnp.sum(eye_f * gi_c, axis=0, keepdims=True)
    gf_r = jnp.sum(eye_f * gf_c, axis=0, keepdims=True)
    b_c = jnp.sum(mask_f * gf_r, axis=1, keepdims=True)
    b_r = jnp.sum(maskT_f * gf_c, axis=0, keepdims=True)
    return gi_c, gi_r, b_c, b_r


def _chunk_masks(reverse):
    L = CHUNK
    r = lax.broadcasted_iota(jnp.int32, (L, L), 0)
    c = lax.broadcasted_iota(jnp.int32, (L, L), 1)
    mask = (c >= r) if reverse else (c <= r)
    maskT = (r >= c) if reverse else (r <= c)
    return mask, mask.astype(F32), maskT.astype(F32), (r == c).astype(F32)


def _pick_row(x, e):
    r = lax.broadcasted_iota(jnp.int32, x.shape, 0)
    return jnp.sum(jnp.where(r == e, x, 0.0), axis=0, keepdims=True)


def _dot_nt(a, b):
    return lax.dot_general(a, b, (((1,), (1,)), ((), ())), preferred_element_type=F32)


def _dot(a, b):
    return jnp.dot(a, b, preferred_element_type=F32)


def _chunk_fwd_core(q, k, g, ci, cf, C0, n0, m0, masks, reverse):
    mask, mask_f, maskT_f, eye_f = masks
    gi_c, gi_r, b_c, b_r = _chunk_gates(g, ci, cf, mask_f, maskT_f, eye_f)
    d = jnp.where(mask, b_c - b_r + gi_r, -jnp.inf)
    m_c = jnp.maximum(b_c + m0, jnp.max(d, axis=1, keepdims=True))
    w = jnp.exp(d - m_c)
    a_c = jnp.exp(b_c + m0 - m_c)
    qb, kb = q.astype(BF16), k.astype(BF16)
    s = _dot_nt(qb, kb) * w
    den = a_c * jnp.sum(q * n0, axis=1, keepdims=True) + jnp.sum(s, axis=1, keepdims=True)
    e = 0 if reverse else CHUNK - 1
    m_end, b_end, a_end = _pick_row(m_c, e), _pick_row(b_c, e), _pick_row(a_c, e)
    w_end = jnp.exp(b_end - b_c + gi_c - m_end)
    return qb, kb, s, w, a_c, m_c, den, w_end, a_end, m_end


MLSTM_HEADS_PER_STEP = 4


def _heads_per_step():
    return MLSTM_HEADS_PER_STEP if NH_A % MLSTM_HEADS_PER_STEP == 0 else 1


def _mlstm_fwd(dm, qk_act, P, G, reverse):
    TB, T = dm.TB, dm.T
    NC = TB // CHUNK
    idx = _mlstm_order(dm, reverse, False)
    vcol = dm.off["va"] // DV_A
    base = 2 * NH_A if reverse else 0

    HP = _heads_per_step()
    hcols = lambda hh, w: slice(hh * w, (hh + 1) * w)

    def kern(q_ref, k_ref, v_ref, g_ref, h_ref, cst_ref, nm_ref, C_s, N_s):
        i, hp = pl.program_id(0), pl.program_id(1)
        row8 = lax.broadcasted_iota(jnp.int32, (SUBLANE, DK_A), 0)
        heads = [hp * HP + hh for hh in range(HP)]

        @pl.when(i == 0)
        def _():
            for h in heads:
                C_s[h] = jnp.zeros((DK_A, DV_A), F32)
                N_s[h] = jnp.where(row8 == 1, M_INIT, 0.0)

        masks = _chunk_masks(reverse)
        state = [(C_s[h], N_s.at[h][0:1, :], N_s.at[h][1:2, 0:1]) for h in heads]
        for c in (range(NC - 1, -1, -1) if reverse else range(NC)):
            rows = pl.ds(c * CHUNK, CHUNK)
            g = g_ref[rows, :]
            for hh, h in enumerate(heads):
                C0, n0, m0 = state[hh]
                q, k, v = q_ref[rows, hcols(hh, DK_A)], k_ref[rows, hcols(hh, DK_A)], v_ref[rows, hcols(hh, DV_A)]
                cst_ref[hh, c] = C0
                nm_ref[hh, c] = jnp.where(row8 == 0, n0, jnp.where(row8 == 1, m0, 0.0))
                qb, kb, s, w, a_c, m_c, den, w_end, a_end, m_end = _chunk_fwd_core(
                    q, k, g, base + h, base + NH_A + h, C0, n0, m0, masks, reverse)
                vb = v.astype(BF16)
                num = a_c * _dot(qb, C0.astype(BF16)) + _dot(s.astype(BF16), vb)
                h_ref[rows, hcols(hh, DV_A)] = num / jnp.maximum(jnp.abs(den), jnp.exp(-m_c))
                state[hh] = (a_end * C0 + _dot(k.T.astype(BF16), (w_end * v).astype(BF16)),
                             a_end * n0 + jnp.sum(w_end * k, axis=0, keepdims=True), m_end)
        for hh, h in enumerate(heads):
            C0, n0, m0 = state[hh]
            C_s[h] = C0
            N_s[h] = jnp.where(row8 == 0, n0, jnp.where(row8 == 1, m0, 0.0))

    nch = T // CHUNK
    NG = NH_A // HP
    return pl.pallas_call(
        kern, name="mlstm_fwd_rev" if reverse else "mlstm_fwd", grid=(dm.nblk, NG),
        in_specs=[pl.BlockSpec((TB, HP * DK_A), lambda i, h: (idx(i), h)),
                  pl.BlockSpec((TB, HP * DK_A), lambda i, h: (idx(i), NG + h)),
                  pl.BlockSpec((TB, HP * DV_A), lambda i, h: (idx(i), vcol // HP + h)),
                  pl.BlockSpec((TB, LANE), lambda i, h: (idx(i), 0))],
        out_specs=[pl.BlockSpec((TB, HP * DV_A), lambda i, h: (idx(i), h)),
                   pl.BlockSpec((HP, NC, DK_A, DV_A), lambda i, h: (h, idx(i), 0, 0)),
                   pl.BlockSpec((HP, NC, SUBLANE, DK_A), lambda i, h: (h, idx(i), 0, 0))],
        out_shape=[jax.ShapeDtypeStruct((T, dm.V), F32),
                   jax.ShapeDtypeStruct((NH_A, nch, DK_A, DV_A), F32),
                   jax.ShapeDtypeStruct((NH_A, nch, SUBLANE, DK_A), F32)],
        scratch_shapes=[pltpu.VMEM((NH_A, DK_A, DV_A), F32), pltpu.VMEM((NH_A, SUBLANE, DK_A), F32)],
        compiler_params=_cparams(("arbitrary", "arbitrary")),
    )(qk_act, qk_act, P, G)


def _mlstm_bwd(dm, qk_act, P, G, Cst, NM, H, dH, acc, reverse):
    TB, T = dm.TB, dm.T
    NC = TB // CHUNK
    idx = _mlstm_order(dm, reverse, True)
    vcol = dm.off["va"] // DV_A
    base = 2 * NH_A if reverse else 0
    has_acc = acc is not None
    HP = _heads_per_step()
    hcols = lambda hh, w: slice(hh * w, (hh + 1) * w)

    def kern(*refs):
        (q_ref, k_ref, v_ref, g_ref, cst_ref, nm_ref, hh_ref, dh_ref) = refs[:8]
        p = 8
        if has_acc:
            aq_ref, ak_ref, av_ref, ag_ref = refs[p:p + 4]
            p += 4
        dq_ref, dk_ref, dv_ref, dg_ref, R_s, Rn_s = refs[p:p + 6]
        i, hp = pl.program_id(0), pl.program_id(1)
        row8 = lax.broadcasted_iota(jnp.int32, (SUBLANE, DK_A), 0)
        heads = [hp * HP + hh for hh in range(HP)]

        @pl.when(i == 0)
        def _():
            for h in heads:
                R_s[h] = jnp.zeros((DK_A, DV_A), F32)
                Rn_s[h] = jnp.zeros((SUBLANE, DK_A), F32)

        @pl.when(hp == 0)
        def _():
            dg_ref[...] = ag_ref[...] if has_acc else jnp.zeros((TB, LANE), F32)

        masks = _chunk_masks(reverse)
        _, mask_f, maskT_f, eye_f = masks
        before_f = mask_f - eye_f
        state = [(R_s[h], Rn_s.at[h][0:1, :]) for h in heads]
        lane = lax.broadcasted_iota(jnp.int32, (CHUNK, LANE), 1)

        def as_row(col):
            return jnp.sum(eye_f * col, axis=0, keepdims=True)

        for c in (range(NC) if reverse else range(NC - 1, -1, -1)):
            rows = pl.ds(c * CHUNK, CHUNK)
            g = g_ref[rows, :]
            dg = jnp.zeros((CHUNK, LANE), F32)
            for hh, h in enumerate(heads):
                R, Rn = state[hh]
                qc, vc = hcols(hh, DK_A), hcols(hh, DV_A)
                q, k, v = q_ref[rows, qc], k_ref[rows, qc], v_ref[rows, vc]
                C0 = cst_ref[hh, c]
                n0, m0 = nm_ref.at[hh, c][0:1, :], nm_ref.at[hh, c][1:2, 0:1]
                qb, kb, s, w, a_c, m_c, den, w_end, a_end, _ = _chunk_fwd_core(
                    q, k, g, base + h, base + NH_A + h, C0, n0, m0, masks, reverse)
                vb = v.astype(BF16)
                e_m = jnp.exp(-m_c)
                r = 1.0 / jnp.maximum(jnp.abs(den), e_m)
                dh = dh_ref[rows, vc]
                dN = dh * r
                dD = jnp.where(jnp.abs(den) > e_m,
                               -jnp.sum(dh * hh_ref[rows, vc], axis=1, keepdims=True) * r * jnp.sign(den), 0.0)
                dNb = dN.astype(BF16)
                dS = _dot_nt(dNb, vb) + dD
                dqk = dS * w
                Cb, Rb = C0.astype(BF16), R.astype(BF16)
                dq_in = a_c * (_dot_nt(dNb, Cb) + dD * n0)
                dk_out = w_end * (_dot_nt(vb, Rb) + Rn)
                dq = _dot(dqk.astype(BF16), kb) + dq_in
                dk = _dot(dqk.T.astype(BF16), qb) + dk_out
                dv = _dot(s.T.astype(BF16), dNb) + w_end * _dot(kb, Rb)
                if has_acc:
                    dq_ref[rows, qc] = aq_ref[rows, qc] + dq
                    dk_ref[rows, qc] = ak_ref[rows, qc] + dk
                    dv_ref[rows, vc] = av_ref[rows, vc] + dv
                else:
                    dq_ref[rows, qc] = dq
                    dk_ref[rows, qc] = dk
                    dv_ref[rows, vc] = dv
                gm = dS * s
                g_row = jnp.sum(gm, axis=1, keepdims=True)
                g_col = jnp.sum(eye_f * jnp.sum(gm, axis=0, keepdims=True), axis=1, keepdims=True)
                q_in = jnp.sum(q * dq_in, axis=1, keepdims=True)
                k_out = jnp.sum(k * dk_out, axis=1, keepdims=True)
                through = a_end * (jnp.sum(jnp.sum(R * C0, axis=1, keepdims=True), axis=0, keepdims=True)
                                   + jnp.sum(Rn * n0, axis=1, keepdims=True))
                di = g_col + k_out
                df = (jnp.sum(maskT_f * as_row(g_row - g_col + q_in), axis=1, keepdims=True)
                      + jnp.sum(before_f * as_row(k_out), axis=1, keepdims=True) + through)
                dg = dg + jnp.where(lane == base + h, di, 0.0) + jnp.where(lane == base + NH_A + h, df, 0.0)
                aq = a_c * q
                state[hh] = (a_end * R + _dot(aq.T.astype(BF16), dNb),
                             a_end * Rn + jnp.sum(aq * dD, axis=0, keepdims=True))
            dg_ref[rows, :] += dg
        for hh, h in enumerate(heads):
            R, Rn = state[hh]
            R_s[h] = R
            Rn_s[h] = jnp.where(row8 == 0, Rn, 0.0)

    NG = NH_A // HP
    qspec = pl.BlockSpec((TB, HP * DK_A), lambda i, h: (idx(i), h))
    vspec = pl.BlockSpec((TB, HP * DV_A), lambda i, h: (idx(i), h))
    gspec = pl.BlockSpec((TB, LANE), lambda i, h: (idx(i), 0))
    in_specs = [qspec,
                pl.BlockSpec((TB, HP * DK_A), lambda i, h: (idx(i), NG + h)),
                pl.BlockSpec((TB, HP * DV_A), lambda i, h: (idx(i), vcol // HP + h)),
                gspec,
                pl.BlockSpec((HP, NC, DK_A, DV_A), lambda i, h: (h, idx(i), 0, 0)),
                pl.BlockSpec((HP, NC, SUBLANE, DK_A), lambda i, h: (h, idx(i), 0, 0)),
                vspec, vspec]
    args = [qk_act, qk_act, P, G, Cst, NM, H, dH]
    if has_acc:
        in_specs += [qspec, qspec, vspec, gspec]
        args += list(acc)
    return pl.pallas_call(
        kern, name="mlstm_bwd_rev" if reverse else "mlstm_bwd", grid=(dm.nblk, NG),
        in_specs=in_specs,
        out_specs=[qspec, qspec, vspec, gspec],
        out_shape=[jax.ShapeDtypeStruct((T, dm.QK), F32), jax.ShapeDtypeStruct((T, dm.QK), F32),
                   jax.ShapeDtypeStruct((T, dm.V), F32), jax.ShapeDtypeStruct((T, LANE), F32)],
        scratch_shapes=[pltpu.VMEM((NH_A, DK_A, DV_A), F32), pltpu.VMEM((NH_A, SUBLANE, DK_A), F32)],
        compiler_params=_cparams(("arbitrary", "arbitrary")),
    )(*args)


def _rms_heads(x, w_row, nh, hd):
    out = []
    for h in range(nh):
        xh = x[:, h * hd:(h + 1) * hd]
        rstd = lax.rsqrt(jnp.mean(xh * xh, axis=1, keepdims=True) + EPS)
        out.append((xh * rstd, rstd))
    return out


def _rope(x, cos, sa, sb):
    return x * cos + pltpu.roll(x, HD_B - HD_B // 4, 1) * sa + pltpu.roll(x, HD_B // 4, 1) * sb


def _rope_t(dy, cos, sa, sb):
    return dy * cos + pltpu.roll(dy * sa, HD_B // 4, 1) + pltpu.roll(dy * sb, HD_B - HD_B // 4, 1)


ATT_SCALE = HD_B ** -0.5
LOG2E = 1.4426950408889634
LN2 = 0.6931471805599453
QSCALE = ATT_SCALE * LOG2E


def _qk_prep(dm, P, rope, qn, kn):
    TB, nctx = dm.TB, dm.nctx
    qcol, kcol, vcol = dm.off["qb"] // dm.QB, dm.off["kb"] // dm.KVB, dm.off["vb"] // dm.KVB

    def kern_q(x_ref, t_ref, w_ref, o_ref):
        cos, sa, sb = t_ref[0], t_ref[1], t_ref[2]
        for h, (xn, _) in enumerate(_rms_heads(x_ref[...], None, NH_B, HD_B)):
            o_ref[h] = (_rope(xn * w_ref[0:1, :], cos, sa, sb) * QSCALE).astype(BF16)

    Qr = pl.pallas_call(
        kern_q, name="q_prep", grid=(dm.nlat,),
        in_specs=[pl.BlockSpec((TB, dm.QB), lambda i: (i + nctx, qcol)),
                  pl.BlockSpec((3, TB, HD_B), lambda i: (0, i + nctx, 0)),
                  pl.BlockSpec((SUBLANE, HD_B), lambda i: (0, 0))],
        out_specs=pl.BlockSpec((NH_B, TB, HD_B), lambda i: (0, i, 0)),
        out_shape=jax.ShapeDtypeStruct((NH_B, dm.S, HD_B), BF16),
        compiler_params=_cparams(("parallel",)),
    )(P, rope, qn)

    def kern_k(x_ref, v_ref, t_ref, w_ref, o_ref, vo_ref):
        cos, sa, sb = t_ref[0], t_ref[1], t_ref[2]
        for h, (xn, _) in enumerate(_rms_heads(x_ref[...], None, NKV_B, HD_B)):
            o_ref[:, h * HD_B:(h + 1) * HD_B] = _rope(xn * w_ref[0:1, :], cos, sa, sb).astype(BF16)
        vo_ref[...] = v_ref[...].astype(BF16)

    Kr, Vb = pl.pallas_call(
        kern_k, name="k_prep", grid=(dm.nblk,),
        in_specs=[pl.BlockSpec((TB, dm.KVB), lambda i: (i, kcol)),
                  pl.BlockSpec((TB, dm.KVB), lambda i: (i, vcol)),
                  pl.BlockSpec((3, TB, HD_B), lambda i: (0, i, 0)),
                  pl.BlockSpec((SUBLANE, HD_B), lambda i: (0, 0))],
        out_specs=[pl.BlockSpec((TB, dm.KVB), lambda i: (i, 0))] * 2,
        out_shape=[jax.ShapeDtypeStruct((dm.T, dm.KVB), BF16)] * 2,
        compiler_params=_cparams(("parallel",)),
    )(P, P, rope, kn)
    return Qr, Kr, Vb


def _attn_tiles(dm):
    return _tile(dm.S, 512, LANE), _tile(dm.T, 768, LANE)


def _attn_fwd(dm, Qr, Kr, Vb):
    S, T, G = dm.S, dm.T, dm.G
    tq, tk = _tile(S, 128, LANE), T
    nk = T // tk

    def kern(q_ref, k_ref, v_ref, o_ref, l_ref, m_s, l_s, a_s):
        j = pl.program_id(2)
        k, v = k_ref[...], v_ref[...]
        if nk == 1:
            for h in range(G):
                s = _dot_nt(q_ref[h], k)
                m = jnp.max(s, axis=1, keepdims=True)
                p = jnp.exp2(s - m)
                l = jnp.sum(p, axis=1, keepdims=True)
                o_ref[h] = _dot(p.astype(BF16), v) / l
                l_ref[0, :, h:h + 1] = m + jnp.log(l) * LOG2E
            return

        @pl.when(j == 0)
        def _():
            m_s[...] = jnp.full(m_s.shape, -jnp.inf, F32)
            l_s[...] = jnp.zeros(l_s.shape, F32)
            a_s[...] = jnp.zeros(a_s.shape, F32)

        for h in range(G):
            s = _dot_nt(q_ref[h], k)
            m_old = m_s[h]
            m_new = jnp.maximum(m_old, jnp.max(s, axis=1, keepdims=True))
            p = jnp.exp2(s - m_new)
            corr = jnp.exp2(m_old - m_new)
            l_s[h] = corr * l_s[h] + jnp.sum(p, axis=1, keepdims=True)
            m_s[h] = m_new
            a_s[h] = corr * a_s[h] + _dot(p.astype(BF16), v)

        @pl.when(j == nk - 1)
        def _():
            for h in range(G):
                o_ref[h] = a_s[h] / l_s[h]
                l_ref[0, :, h:h + 1] = m_s[h] + jnp.log(l_s[h]) * LOG2E

    qspec = pl.BlockSpec((G, tq, HD_B), lambda g, i, j: (g, i, 0))
    return pl.pallas_call(
        kern, name="attn_fwd", grid=(NKV_B, S // tq, nk),
        in_specs=[qspec,
                  pl.BlockSpec((tk, HD_B), lambda g, i, j: (j, g)),
                  pl.BlockSpec((tk, HD_B), lambda g, i, j: (j, g))],
        out_specs=[qspec, pl.BlockSpec((1, tq, G), lambda g, i, j: (g, i, 0))],
        out_shape=[jax.ShapeDtypeStruct((NH_B, S, HD_B), F32), jax.ShapeDtypeStruct((NKV_B, S, G), F32)],
        scratch_shapes=[pltpu.VMEM((G, tq, 1), F32), pltpu.VMEM((G, tq, 1), F32),
                        pltpu.VMEM((G, tq, HD_B), F32)],
        compiler_params=_cparams(("parallel", "parallel", "arbitrary")),
    )(Qr, Kr, Vb)


def _attn_bwd_dq(dm, Qr, Kr, Vb, dO, LSE, DEL):
    S, T, G = dm.S, dm.T, dm.G
    tq, tk = _tile(S, 128, LANE), T
    nk = T // tk

    def kern(q_ref, k_ref, v_ref, do_ref, l_ref, d_ref, dq_ref, a_s):
        j = pl.program_id(2)
        k, v = k_ref[...], v_ref[...]
        if nk == 1:
            for h in range(G):
                p = jnp.exp2(_dot_nt(q_ref[h], k) - l_ref[0, :, h:h + 1])
                dp = _dot_nt(do_ref[h], v)
                dq_ref[h] = _dot((p * (dp - d_ref[0, :, h:h + 1])).astype(BF16), k) * ATT_SCALE
            return

        @pl.when(j == 0)
        def _():
            a_s[...] = jnp.zeros(a_s.shape, F32)

        for h in range(G):
            p = jnp.exp2(_dot_nt(q_ref[h], k) - l_ref[0, :, h:h + 1])
            dp = _dot_nt(do_ref[h], v)
            a_s[h] += _dot((p * (dp - d_ref[0, :, h:h + 1])).astype(BF16), k)

        @pl.when(j == nk - 1)
        def _():
            for h in range(G):
                dq_ref[h] = a_s[h] * ATT_SCALE

    qspec = pl.BlockSpec((G, tq, HD_B), lambda g, i, j: (g, i, 0))
    kspec = pl.BlockSpec((tk, HD_B), lambda g, i, j: (j, g))
    lspec = pl.BlockSpec((1, tq, G), lambda g, i, j: (g, i, 0))
    return pl.pallas_call(
        kern, name="attn_bwd_dq", grid=(NKV_B, S // tq, nk),
        in_specs=[qspec, kspec, kspec, qspec, lspec, lspec],
        out_specs=qspec,
        out_shape=jax.ShapeDtypeStruct((NH_B, S, HD_B), F32),
        scratch_shapes=[pltpu.VMEM((G, tq, HD_B), F32)],
        compiler_params=_cparams(("parallel", "parallel", "arbitrary")),
    )(Qr, Kr, Vb, dO, LSE, DEL)


def _attn_bwd_dkv(dm, Qr, Kr, Vb, dO, LSE_T, DEL_T):
    S, T, G = dm.S, dm.T, dm.G
    tq, tk = _attn_tiles(dm)
    nq = S // tq

    def kern(q_ref, k_ref, v_ref, do_ref, l_ref, d_ref, dk_ref, dv_ref, ak_s, av_s):
        i = pl.program_id(2)

        @pl.when(i == 0)
        def _():
            ak_s[...] = jnp.zeros(ak_s.shape, F32)
            av_s[...] = jnp.zeros(av_s.shape, F32)

        k, v = k_ref[...], v_ref[...]
        for h in range(G):
            q, do = q_ref[h], do_ref[h]
            pT = jnp.exp2(_dot_nt(k, q) - l_ref[0, h:h + 1, :])
            dpT = _dot_nt(v, do)
            av_s[...] += _dot(pT.astype(BF16), do)
            ak_s[...] += _dot((pT * (dpT - d_ref[0, h:h + 1, :])).astype(BF16), q)

        @pl.when(i == nq - 1)
        def _():
            dk_ref[...] = ak_s[...] * LN2
            dv_ref[...] = av_s[...].astype(BF16)

    qspec = pl.BlockSpec((G, tq, HD_B), lambda g, j, i: (g, i, 0))
    kspec = pl.BlockSpec((tk, HD_B), lambda g, j, i: (j, g))
    lspec = pl.BlockSpec((1, G, tq), lambda g, j, i: (g, 0, i))
    return pl.pallas_call(
        kern, name="attn_bwd_dkv", grid=(NKV_B, T // tk, nq),
        in_specs=[qspec, kspec, kspec, qspec, lspec, lspec],
        out_specs=[kspec, kspec],
        out_shape=[jax.ShapeDtypeStruct((T, dm.KVB), F32), jax.ShapeDtypeStruct((T, dm.KVB), BF16)],
        scratch_shapes=[pltpu.VMEM((tk, HD_B), F32), pltpu.VMEM((tk, HD_B), F32)],
        compiler_params=_cparams(("parallel", "parallel", "arbitrary")),
    )(Qr, Kr, Vb, dO, LSE_T, DEL_T)


def _qk_bwd(dm, dQr, dKr, P, rope, qn, kn):
    TB, nctx = dm.TB, dm.nctx
    qcol, kcol = dm.off["qb"] // dm.QB, dm.off["kb"] // dm.KVB

    def head_bwd(dyr, x, w_row, cos, sa, sb):
        rstd = lax.rsqrt(jnp.mean(x * x, axis=1, keepdims=True) + EPS)
        xn = x * rstd
        dy = _rope_t(dyr, cos, sa, sb)
        dw = jnp.sum(dy * xn, axis=0, keepdims=True)
        dxn = dy * w_row
        dx = rstd * (dxn - xn * jnp.mean(dxn * xn, axis=1, keepdims=True))
        return dx, dw

    def make(nh, ctx_zero):
        def kern(d_ref, x_ref, t_ref, w_ref, o_ref, gw_ref):
            i = pl.program_id(0)

            @pl.when(i == 0)
            def _():
                gw_ref[...] = jnp.zeros(gw_ref.shape, F32)

            def live():
                cos, sa, sb = t_ref[0], t_ref[1], t_ref[2]
                tot = jnp.zeros((1, HD_B), F32)
                for h in range(nh):
                    cols = slice(h * HD_B, (h + 1) * HD_B)
                    dyr = d_ref[h] if ctx_zero else d_ref[:, cols]
                    dx, dw = head_bwd(dyr, x_ref[:, cols], w_ref[0:1, :], cos, sa, sb)
                    o_ref[:, cols] = dx.astype(BF16)
                    tot = tot + dw
                gw_ref[0:1, :] += tot

            if ctx_zero:
                @pl.when(i < nctx)
                def _():
                    o_ref[...] = jnp.zeros(o_ref.shape, BF16)

                pl.when(i >= nctx)(live)
            else:
                live()
        return kern

    lat = lambda i: jnp.maximum(i - nctx, 0)
    d_qb, gqn = pl.pallas_call(
        make(NH_B, True), name="q_bwd", grid=(dm.nblk,),
        in_specs=[pl.BlockSpec((NH_B, TB, HD_B), lambda i: (0, lat(i), 0)),
                  pl.BlockSpec((TB, dm.QB), lambda i: (i, qcol)),
                  pl.BlockSpec((3, TB, HD_B), lambda i: (0, i, 0)),
                  pl.BlockSpec((SUBLANE, HD_B), lambda i: (0, 0))],
        out_specs=[pl.BlockSpec((TB, dm.QB), lambda i: (i, 0)), pl.BlockSpec((SUBLANE, HD_B), lambda i: (0, 0))],
        out_shape=[jax.ShapeDtypeStruct((dm.T, dm.QB), BF16), jax.ShapeDtypeStruct((SUBLANE, HD_B), F32)],
        compiler_params=_cparams(("arbitrary",)),
    )(dQr, P, rope, qn)
    d_kb, gkn = pl.pallas_call(
        make(NKV_B, False), name="k_bwd", grid=(dm.nblk,),
        in_specs=[pl.BlockSpec((TB, dm.KVB), lambda i: (i, 0)),
                  pl.BlockSpec((TB, dm.KVB), lambda i: (i, kcol)),
                  pl.BlockSpec((3, TB, HD_B), lambda i: (0, i, 0)),
                  pl.BlockSpec((SUBLANE, HD_B), lambda i: (0, 0))],
        out_specs=[pl.BlockSpec((TB, dm.KVB), lambda i: (i, 0)), pl.BlockSpec((SUBLANE, HD_B), lambda i: (0, 0))],
        out_shape=[jax.ShapeDtypeStruct((dm.T, dm.KVB), BF16), jax.ShapeDtypeStruct((SUBLANE, HD_B), F32)],
        compiler_params=_cparams(("arbitrary",)),
    )(dKr, P, rope, kn)
    return d_qb, d_kb, gqn, gkn


def _merge_prep(dm, Hf, Hb, P, O, mhw):
    TB, nctx = dm.TB, dm.nctx
    lat = lambda c: (lambda i: (i + nctx, c))

    def kern(hf_ref, hb_ref, oa_ref, za_ref, zb_ref, o_ref, w_ref, a_ref, b_ref):
        for h in range(NH_A):
            cols = slice(h * DV_A, (h + 1) * DV_A)
            hs = hf_ref[:, cols] + hb_ref[:, cols]
            rstd = lax.rsqrt(jnp.mean(hs * hs, axis=1, keepdims=True) + EPS)
            za = za_ref[:, cols]
            a_ref[:, cols] = (_sigmoid(oa_ref[:, cols]) * (hs * rstd * w_ref[0:1, cols])
                              * (za * _sigmoid(za))).astype(BF16)
        for h in range(NH_B):
            cols = slice(h * HD_B, (h + 1) * HD_B)
            zb = zb_ref[:, cols]
            b_ref[:, cols] = (o_ref[h] * (zb * _sigmoid(zb))).astype(BF16)

    return pl.pallas_call(
        kern, name="merge_prep", grid=(dm.nlat,),
        in_specs=[pl.BlockSpec((TB, dm.V), lat(0)), pl.BlockSpec((TB, dm.V), lat(0)),
                  pl.BlockSpec((TB, dm.V), lat(dm.off["oa"] // dm.V)),
                  pl.BlockSpec((TB, dm.V), lat(dm.off["za"] // dm.V)),
                  pl.BlockSpec((TB, dm.QB), lat(dm.off["zb"] // dm.QB)),
                  pl.BlockSpec((NH_B, TB, HD_B), lambda i: (0, i, 0)),
                  pl.BlockSpec((SUBLANE, dm.V), lambda i: (0, 0))],
        out_specs=[pl.BlockSpec((TB, dm.V), lambda i: (i, 0)), pl.BlockSpec((TB, dm.QB), lambda i: (i, 0))],
        out_shape=[jax.ShapeDtypeStruct((dm.S, dm.V), BF16), jax.ShapeDtypeStruct((dm.S, dm.QB), BF16)],
        compiler_params=_cparams(("parallel",)),
    )(Hf, Hb, P, P, P, O, mhw)


def _gate_merge(dm, ya, yb, P):
    TB, D, nctx = dm.TB, dm.D, dm.nctx
    gcol = dm.off["g"] // D

    def kern(ya_ref, yb_ref, ga_ref, gb_ref, o_ref):
        o_ref[...] = (_sigmoid(ga_ref[...]) * ya_ref[...] + _sigmoid(gb_ref[...]) * yb_ref[...]).astype(BF16)

    row = pl.BlockSpec((TB, D), lambda i: (i, 0))
    return pl.pallas_call(
        kern, name="gate_merge", grid=(dm.nlat,),
        in_specs=[row, row, pl.BlockSpec((TB, D), lambda i: (i + nctx, gcol)),
                  pl.BlockSpec((TB, D), lambda i: (i + nctx, gcol + 1))],
        out_specs=row, out_shape=jax.ShapeDtypeStruct((dm.S, D), BF16),
        compiler_params=_cparams(("parallel",)),
    )(ya, yb, P, P)


def _final(dm, x, out, tgt, modv, lnp):
    TB, D = dm.TB, dm.D

    def kern(x_ref, o_ref, t_ref, m_ref, p_ref, dr_ref, do_ref, cs_ref, ls_ref):
        i = pl.program_id(0)

        @pl.when(i == 0)
        def _():
            cs_ref[...] = jnp.zeros(cs_ref.shape, F32)
            ls_ref[...] = jnp.zeros(ls_ref.shape, F32)

        gate, lnw, lnb = m_ref[4:5, :], p_ref[0:1, :], p_ref[1:2, :]
        out = o_ref[...]
        xh, rstd = _ln_stats(ALPHA * x_ref[...] + gate * out)
        e = xh * lnw + lnb - t_ref[...]
        ls_ref[...] += 0.5 * jnp.sum(jnp.sum(e * e, axis=1, keepdims=True), axis=0, keepdims=True) / D
        dy = e * (1.0 / D)
        dxh = dy * lnw
        dr = rstd * (dxh - jnp.mean(dxh, axis=1, keepdims=True)
                     - xh * jnp.mean(dxh * xh, axis=1, keepdims=True))
        cs_ref[0:1, :] += jnp.sum(dy * xh, axis=0, keepdims=True)
        cs_ref[1:2, :] += jnp.sum(dy, axis=0, keepdims=True)
        cs_ref[2:3, :] += jnp.sum(dr * out, axis=0, keepdims=True)
        dr_ref[...] = ALPHA * dr
        do_ref[...] = (dr * gate).astype(BF16)

    row = pl.BlockSpec((TB, D), lambda i: (i, 0))
    par = pl.BlockSpec((SUBLANE, D), lambda i: (0, 0))
    return pl.pallas_call(
        kern, name="final_norm_loss", grid=(dm.nlat,),
        in_specs=[row, row, row, par, par],
        out_specs=[row, row, par, pl.BlockSpec((SUBLANE, LANE), lambda i: (0, 0))],
        out_shape=[jax.ShapeDtypeStruct((dm.S, D), F32), jax.ShapeDtypeStruct((dm.S, D), BF16),
                   jax.ShapeDtypeStruct((SUBLANE, D), F32), jax.ShapeDtypeStruct((SUBLANE, LANE), F32)],
        compiler_params=_cparams(("arbitrary",)),
    )(x, out, tgt, modv, lnp)


def _merge_bwd(dm, dM, ya, yb, P):
    TB, D, nctx = dm.TB, dm.D, dm.nctx
    gcol = dm.off["g"] // D
    lat = lambda i: jnp.maximum(i - nctx, 0)

    def kern(dm_ref, ya_ref, yb_ref, ga_ref, gb_ref, da_ref, db_ref, dg_ref):
        i = pl.program_id(0)

        @pl.when(i < nctx)
        def _():
            dg_ref[...] = jnp.zeros(dg_ref.shape, BF16)

        @pl.when(i >= nctx)
        def _():
            d = dm_ref[...]
            sa, sb = _sigmoid(ga_ref[...]), _sigmoid(gb_ref[...])
            da_ref[...] = (d * sa).astype(BF16)
            db_ref[...] = (d * sb).astype(BF16)
            dg_ref[:, 0:D] = (d * ya_ref[...] * sa * (1.0 - sa)).astype(BF16)
            dg_ref[:, D:2 * D] = (d * yb_ref[...] * sb * (1.0 - sb)).astype(BF16)

    row = pl.BlockSpec((TB, D), lambda i: (lat(i), 0))
    return pl.pallas_call(
        kern, name="merge_bwd", grid=(dm.nblk,),
        in_specs=[row, row, row, pl.BlockSpec((TB, D), lambda i: (i, gcol)),
                  pl.BlockSpec((TB, D), lambda i: (i, gcol + 1))],
        out_specs=[row, row, pl.BlockSpec((TB, 2 * D), lambda i: (i, 0))],
        out_shape=[jax.ShapeDtypeStruct((dm.S, D), BF16), jax.ShapeDtypeStruct((dm.S, D), BF16),
                   jax.ShapeDtypeStruct((dm.T, 2 * D), BF16)],
        compiler_params=_cparams(("arbitrary",)),
    )(dM, ya, yb, P, P)


def _branch_bwd(dm, dA, dB, Hf, Hb, P, O, mhw):
    TB, nctx, G = dm.TB, dm.nctx, dm.G
    lat = lambda i: jnp.maximum(i - nctx, 0)

    def kern(da_ref, db_ref, hf_ref, hb_ref, oa_ref, za_ref, zb_ref, o_ref, w_ref,
             doa_ref, dza_ref, dzb_ref, dh_ref, do_ref, del_ref, gw_ref):
        i = pl.program_id(0)

        @pl.when(i == 0)
        def _():
            gw_ref[...] = jnp.zeros(gw_ref.shape, F32)

        @pl.when(i < nctx)
        def _():
            doa_ref[...] = jnp.zeros(doa_ref.shape, BF16)
            dza_ref[...] = jnp.zeros(dza_ref.shape, BF16)
            dzb_ref[...] = jnp.zeros(dzb_ref.shape, BF16)
            dh_ref[...] = jnp.zeros(dh_ref.shape, F32)

        @pl.when(i >= nctx)
        def _():
            for h in range(NH_A):
                cols = slice(h * DV_A, (h + 1) * DV_A)
                hs = hf_ref[:, cols] + hb_ref[:, cols]
                rstd = lax.rsqrt(jnp.mean(hs * hs, axis=1, keepdims=True) + EPS)
                xn = hs * rstd
                w = w_ref[0:1, cols]
                hn = xn * w
                so, za = _sigmoid(oa_ref[:, cols]), za_ref[:, cols]
                sz = _sigmoid(za)
                silu = za * sz
                da = da_ref[:, cols]
                doa_ref[:, cols] = (da * hn * silu * so * (1.0 - so)).astype(BF16)
                dza_ref[:, cols] = (da * hn * so * sz * (1.0 + za * (1.0 - sz))).astype(BF16)
                dhn = da * so * silu
                gw_ref[0:1, cols] += jnp.sum(dhn * xn, axis=0, keepdims=True)
                dxn = dhn * w
                dh_ref[:, cols] = rstd * (dxn - xn * jnp.mean(dxn * xn, axis=1, keepdims=True))
            for h in range(NH_B):
                cols = slice(h * HD_B, (h + 1) * HD_B)
                zb = zb_ref[:, cols]
                sz = _sigmoid(zb)
                db, o = db_ref[:, cols], o_ref[h]
                do = db * (zb * sz)
                do_ref[h] = do.astype(BF16)
                dzb_ref[:, cols] = (db * o * sz * (1.0 + zb * (1.0 - sz))).astype(BF16)
                del_ref[h // G, :, (h % G):(h % G) + 1] = jnp.sum(do * o, axis=1, keepdims=True)

    vlat = pl.BlockSpec((TB, dm.V), lambda i: (lat(i), 0))
    qlat = pl.BlockSpec((TB, dm.QB), lambda i: (lat(i), 0))
    hlat = pl.BlockSpec((NH_B, TB, HD_B), lambda i: (0, lat(i), 0))
    vrow = pl.BlockSpec((TB, dm.V), lambda i: (i, 0))
    qrow = pl.BlockSpec((TB, dm.QB), lambda i: (i, 0))
    pv = lambda n: pl.BlockSpec((TB, dm.V), lambda i: (i, dm.off[n] // dm.V))
    return pl.pallas_call(
        kern, name="branch_bwd", grid=(dm.nblk,),
        in_specs=[vlat, qlat, vrow, vrow, pv("oa"), pv("za"),
                  pl.BlockSpec((TB, dm.QB), lambda i: (i, dm.off["zb"] // dm.QB)), hlat,
                  pl.BlockSpec((SUBLANE, dm.V), lambda i: (0, 0))],
        out_specs=[vrow, vrow, qrow, vrow, hlat,
                   pl.BlockSpec((NKV_B, TB, G), lambda i: (0, lat(i), 0)),
                   pl.BlockSpec((SUBLANE, dm.V), lambda i: (0, 0))],
        out_shape=[jax.ShapeDtypeStruct((dm.T, dm.V), BF16), jax.ShapeDtypeStruct((dm.T, dm.V), BF16),
                   jax.ShapeDtypeStruct((dm.T, dm.QB), BF16), jax.ShapeDtypeStruct((dm.T, dm.V), F32),
                   jax.ShapeDtypeStruct((NH_B, dm.S, HD_B), BF16), jax.ShapeDtypeStruct((NKV_B, dm.S, G), F32),
                   jax.ShapeDtypeStruct((SUBLANE, dm.V), F32)],
        compiler_params=_cparams(("arbitrary",)),
    )(dA, dB, Hf, Hb, P, P, P, O, mhw)


def _conv_bwd(dm, dq, dk, P, convp):
    W = 2 * dm.QK
    col = dm.off["qk"] // W
    kscale = DK_A ** -0.5
    TB = dm.TB

    def kern1(dq_ref, dk_ref, x_ref, p_ref, n_ref, c_ref, dz_ref):
        x = x_ref[...]
        xp, xn = _shifted(dm, x, p_ref, n_ref)
        z = c_ref[3:4, :] + xp * c_ref[0:1, :] + x * c_ref[1:2, :] + xn * c_ref[2:3, :]
        sz = _sigmoid(z)
        dact = jnp.concatenate([dq_ref[...], dk_ref[...] * kscale], axis=1)
        dz_ref[...] = dact * sz * (1.0 + z * (1.0 - sz))

    half = pl.BlockSpec((TB, dm.QK), lambda i: (i, 0))
    par = pl.BlockSpec((SUBLANE, W), lambda i: (0, 0))
    dz = pl.pallas_call(
        kern1, name="conv_bwd_act", grid=(dm.nblk,),
        in_specs=[half, half] + _conv_specs(dm, W, col) + [par],
        out_specs=pl.BlockSpec((TB, W), lambda i: (i, 0)),
        out_shape=jax.ShapeDtypeStruct((dm.T, W), F32),
        compiler_params=_cparams(("parallel",)),
    )(dq, dk, P, P, P, convp)

    def kern2(z_ref, zp_ref, zn_ref, x_ref, p_ref, n_ref, c_ref, dx_ref, cs_ref):
        @pl.when(pl.program_id(0) == 0)
        def _():
            cs_ref[...] = jnp.zeros(cs_ref.shape, F32)

        dz = z_ref[...]
        dzp, dzn = _shifted(dm, dz, zp_ref, zn_ref)
        dx_ref[...] = (dzn * c_ref[0:1, :] + dz * c_ref[1:2, :] + dzp * c_ref[2:3, :]).astype(BF16)
        x = x_ref[...]
        xp, xn = _shifted(dm, x, p_ref, n_ref)
        cs_ref[0:1, :] += jnp.sum(dz * xp, axis=0, keepdims=True)
        cs_ref[1:2, :] += jnp.sum(dz * x, axis=0, keepdims=True)
        cs_ref[2:3, :] += jnp.sum(dz * xn, axis=0, keepdims=True)
        cs_ref[3:4, :] += jnp.sum(dz, axis=0, keepdims=True)

    return pl.pallas_call(
        kern2, name="conv_bwd_taps", grid=(dm.nblk,),
        in_specs=_conv_specs(dm, W, 0) + _conv_specs(dm, W, col) + [par],
        out_specs=[pl.BlockSpec((TB, W), lambda i: (i, 0)), par],
        out_shape=[jax.ShapeDtypeStruct((dm.T, W), BF16), jax.ShapeDtypeStruct((SUBLANE, W), F32)],
        compiler_params=_cparams(("arbitrary",)),
    )(dz, dz, dz, P, P, P, convp)


def _gates_bwd(dm, dG, G):
    TB = dm.TB

    def kern(d_ref, g_ref, o_ref, cs_ref):
        @pl.when(pl.program_id(0) == 0)
        def _():
            cs_ref[...] = jnp.zeros(cs_ref.shape, F32)

        lane = lax.broadcasted_iota(jnp.int32, (TB, LANE), 1)
        is_f = ((lane // NH_A) % 2) == 1
        d = d_ref[...]
        dpre = jnp.where(lane < dm.NIF, jnp.where(is_f, d * (1.0 - jnp.exp(g_ref[...])), d), 0.0)
        cs_ref[0:1, :] += jnp.sum(dpre, axis=0, keepdims=True)
        if dm.IFP > LANE:
            o_ref[:, LANE:] = jnp.zeros((TB, dm.IFP - LANE), BF16)
        o_ref[:, 0:LANE] = dpre.astype(BF16)

    row = pl.BlockSpec((TB, LANE), lambda i: (i, 0))
    return pl.pallas_call(
        kern, name="gates_bwd", grid=(dm.nblk,),
        in_specs=[row, row],
        out_specs=[pl.BlockSpec((TB, dm.IFP), lambda i: (i, 0)), pl.BlockSpec((SUBLANE, LANE), lambda i: (0, 0))],
        out_shape=[jax.ShapeDtypeStruct((dm.T, dm.IFP), BF16), jax.ShapeDtypeStruct((SUBLANE, LANE), F32)],
        compiler_params=_cparams(("arbitrary",)),
    )(dG, G)


def _ln_mod_bwd(dm, dU, xcat, modv, dr_a):
    TB, D, nctx = dm.TB, dm.D, dm.nctx
    lat = lambda i: jnp.maximum(i - nctx, 0)

    def kern(du_ref, x_ref, m_ref, dr_ref, gx_ref, cs_ref):
        i = pl.program_id(0)
        is_ctx = i < nctx

        @pl.when(i == 0)
        def _():
            cs_ref[...] = jnp.zeros(cs_ref.shape, F32)

        xh, rstd = _ln_stats(x_ref[...])
        du = du_ref[...]
        s_shift = jnp.sum(du, axis=0, keepdims=True)
        s_scale = jnp.sum(du * xh, axis=0, keepdims=True)
        cs_ref[0:1, :] += jnp.where(is_ctx, 0.0, s_shift)
        cs_ref[1:2, :] += jnp.where(is_ctx, 0.0, s_scale)
        cs_ref[2:3, :] += jnp.where(is_ctx, s_shift, 0.0)
        cs_ref[3:4, :] += jnp.where(is_ctx, s_scale, 0.0)

        @pl.when(i >= nctx)
        def _():
            dxh = du * (1.0 + m_ref[1:2, :])
            gx_ref[...] = dr_ref[...] + rstd * (dxh - jnp.mean(dxh, axis=1, keepdims=True)
                                                - xh * jnp.mean(dxh * xh, axis=1, keepdims=True))

    row = pl.BlockSpec((TB, D), lambda i: (i, 0))
    lrow = pl.BlockSpec((TB, D), lambda i: (lat(i), 0))
    par = pl.BlockSpec((SUBLANE, D), lambda i: (0, 0))
    return pl.pallas_call(
        kern, name="ln_mod_bwd", grid=(dm.nblk,),
        in_specs=[row, row, par, lrow],
        out_specs=[lrow, par],
        out_shape=[jax.ShapeDtypeStruct((dm.S, D), F32), jax.ShapeDtypeStruct((SUBLANE, D), F32)],
        compiler_params=_cparams(("arbitrary",)),
    )(dU, xcat, modv, dr_a)


def _mod_fwd(craw, w_loc, b_loc):
    R, D = craw.shape
    n = w_loc.shape[1]

    def kern(c_ref, w_ref, b_ref, o_ref):
        c = c_ref[...]
        o_ref[...] = _dot((c * _sigmoid(c)).astype(BF16), w_ref[...].astype(BF16)) + b_ref[0:1, :]

    return pl.pallas_call(
        kern, name="mod_fwd", out_shape=jax.ShapeDtypeStruct((R, n), F32),
        compiler_params=pltpu.CompilerParams(vmem_limit_bytes=VMEM_LIMIT),
    )(craw, w_loc, b_loc)


def _mod_bwd(crawT, dmod, w_loc):
    D, R = crawT.shape
    n = w_loc.shape[1]

    def kern(c_ref, d_ref, w_ref, gw_ref, dc_ref):
        c = c_ref[...]
        d = d_ref[...].astype(BF16)
        gw_ref[...] = _dot((c * _sigmoid(c)).astype(BF16), d)
        dc_ref[...] = _dot_nt(d, w_ref[...].astype(BF16))

    return pl.pallas_call(
        kern, name="mod_bwd",
        out_shape=[jax.ShapeDtypeStruct((D, n), F32), jax.ShapeDtypeStruct((R, D), F32)],
        compiler_params=pltpu.CompilerParams(vmem_limit_bytes=VMEM_LIMIT),
    )(crawT, dmod, w_loc)


def _cctx_grad(dsilu, c_ctx):
    def kern(p_ref, c_ref, o_ref):
        c = c_ref[...]
        sc = _sigmoid(c)
        o_ref[...] = p_ref[...] * (sc * (1.0 + c * (1.0 - sc)))

    return pl.pallas_call(kern, name="cctx_grad", out_shape=jax.ShapeDtypeStruct(c_ctx.shape, F32))(dsilu, c_ctx)


def _adamw(w, g, m, v, name):
    R, C = w.shape
    tb = _tile(R, max(SUBLANE, (1 << 19) // (4 * C) // SUBLANE * SUBLANE), SUBLANE)
    c1 = 1.0 / (1.0 - ADAM_B1 ** ADAM_STEP)
    c2 = 1.0 / (1.0 - ADAM_B2 ** ADAM_STEP)

    def kern(w_ref, g_ref, m_ref, v_ref, d_ref, nm_ref, nv_ref):
        g = g_ref[...]
        nm = ADAM_B1 * m_ref[...] + (1.0 - ADAM_B1) * g
        nv = ADAM_B2 * v_ref[...] + (1.0 - ADAM_B2) * (g * g)
        nm_ref[...] = nm
        nv_ref[...] = nv
        d_ref[...] = -ADAM_LR * ((nm * c1) / (jnp.sqrt(nv * c2) + ADAM_EPS) + ADAM_WD * w_ref[...])

    spec = pl.BlockSpec((tb, C), lambda i: (i, 0))
    return pl.pallas_call(
        kern, name=name, grid=(R // tb,), in_specs=[spec] * 4, out_specs=[spec] * 3,
        out_shape=[jax.ShapeDtypeStruct((R, C), F32)] * 3,
        compiler_params=_cparams(("parallel",)),
    )(w, g, m, v)


def _rope_tables(dm):
    half = HD_B // 2
    rows_n = dm.S // GRID_W
    row = jnp.repeat(jnp.arange(rows_n), GRID_W).astype(F32)
    col = jnp.tile(jnp.arange(GRID_W), rows_n).astype(F32)
    inv = ROPE_THETA ** (-jnp.arange(0, half, 2, dtype=F32) / half)
    ar, ac = row[:, None] * inv[None], col[:, None] * inv[None]
    cos = jnp.concatenate([jnp.cos(ar), jnp.cos(ar), jnp.cos(ac), jnp.cos(ac)], axis=1)
    zr = jnp.zeros_like(ar)
    sa = jnp.concatenate([-jnp.sin(ar), zr, -jnp.sin(ac), zr], axis=1)
    sb = jnp.concatenate([zr, jnp.sin(ar), zr, jnp.sin(ac)], axis=1)
    ctx = jnp.stack([jnp.ones((dm.Tc, HD_B), F32), jnp.zeros((dm.Tc, HD_B), F32), jnp.zeros((dm.Tc, HD_B), F32)])
    return jnp.concatenate([ctx, jnp.stack([cos, sa, sb])], axis=1)


def _rows8(*rows, width):
    out = [jnp.pad(r.reshape(-1).astype(F32), (0, width - r.size)) for r in rows]
    n = -(-len(out) // SUBLANE) * SUBLANE
    out += [jnp.zeros((width,), F32)] * (n - len(out))
    return jnp.stack(out)


def _to_padded(dm, w_full):
    cols, o = {}, 0
    for n, wd in zip(dm.ref_names, dm.ref_widths):
        cols[n] = w_full[:, o:o + wd]
        o += wd
    cols["if"] = jnp.pad(cols["if"], ((0, 0), (0, dm.IFP - dm.NIF)))
    return jnp.concatenate([cols[n] for n in dm.order], axis=1)


def _from_padded(dm, w_pad):
    return jnp.concatenate(
        [w_pad[:, dm.off[n]:dm.off[n] + wd] for n, wd in zip(dm.ref_names, dm.ref_widths)], axis=1)


def kernel(x, c, ctx, c_ctx, w_mod, b_mod, w_in, b_if, conv_w, conv_b, mh_norm_w, q_norm_w, k_norm_w, w_branch_a, w_branch_b, w_out, ln_w, ln_b, loss_target, m_c_ctx, m_w_mod, m_b_mod, m_w_in, m_b_if, m_conv_w, m_conv_b, m_mh_norm_w, m_q_norm_w, m_k_norm_w, m_w_branch_a, m_w_branch_b, m_w_out, m_ln_w, m_ln_b, v_c_ctx, v_w_mod, v_b_mod, v_w_in, v_b_if, v_conv_w, v_conv_b, v_mh_norm_w, v_q_norm_w, v_k_norm_w, v_w_branch_a, v_w_branch_b, v_w_out, v_ln_w, v_ln_b):
    S, D = x.shape[1], x.shape[2]
    Tc = ctx.shape[1]
    dm = Dims(S, Tc, D)
    T, QK2 = dm.T, 2 * dm.QK
    me = 4 * lax.axis_index("x") + 2 * lax.axis_index("y") + lax.axis_index("c")
    n_mod = w_mod.shape[2]
    n_in = w_in.shape[2]
    n_cv = conv_w.shape[2]
    rb = w_out.shape[1]

    pack0 = _all_gather(_rows8(c[0], conv_w[0, 0], conv_w[0, 1], conv_w[0, 2], width=D), "ag_cond", False)
    c_all = pack0[:, 0, :]
    conv_full = jnp.transpose(pack0[:, 1:4, :n_cv], (1, 0, 2)).reshape(CONV_W, QK2)
    convp = _rows8(conv_full[0], conv_full[1], conv_full[2], conv_b[0], width=QK2)

    w_in_all = _all_gather(w_in[0].astype(BF16), "ag_w_in", True)
    Wp = _to_padded(dm, jnp.transpose(w_in_all, (1, 0, 2)).reshape(D, dm.N_IN))
    wsq = jnp.concatenate([w_branch_a[0], w_branch_b[0], w_out[0]], axis=0).astype(BF16)

    craw = _rows8(*[c_all[j] for j in range(N_DEV)], c_ctx, width=D)
    b_loc = _rows8(lax.dynamic_slice(b_mod[0], (me * n_mod,), (n_mod,)), width=n_mod)
    mod_all = _all_gather(_mod_fwd(craw, w_mod[0], b_loc), "ag_mod", False)
    mod_rows = jnp.transpose(mod_all, (1, 0, 2)).reshape(2 * SUBLANE, 3 * D)
    mod_me = lax.dynamic_slice(mod_rows, (me, 0), (1, 3 * D))[0]
    mod_cx = mod_rows[N_DEV]
    modv = _rows8(mod_me[0:D], mod_me[D:2 * D], mod_cx[0:D], mod_cx[D:2 * D], mod_me[2 * D:3 * D], width=D)

    xcat = jnp.concatenate([ctx[0], x[0]], axis=0)
    U = _ln_mod_fwd(dm, xcat, modv)
    P, wsq_all = _mm(U, Wp, "mm_in_proj", tn=896, exchange=("ag", wsq))
    Wba = wsq_all[:, 0:rb, :].reshape(dm.V, D)
    Wbb = wsq_all[:, rb:2 * rb, :].reshape(dm.QB, D)
    Wout = wsq_all[:, 2 * rb:3 * rb, :].reshape(D, D)
    qk_act = _conv_silu_fwd(dm, P, convp)
    G = _gates_fwd(dm, P, _rows8(b_if[0], width=LANE))
    Hf, Cf, NMf = _mlstm_fwd(dm, qk_act, P, G, False)
    Hb, Cb, NMb = _mlstm_fwd(dm, qk_act, P, G, True)
    rope = _rope_tables(dm)
    qn, kn = _rows8(q_norm_w[0], width=HD_B), _rows8(k_norm_w[0], width=HD_B)
    Qr, Kr, Vb = _qk_prep(dm, P, rope, qn, kn)
    O, LSE = _attn_fwd(dm, Qr, Kr, Vb)
    mhw = _rows8(mh_norm_w[0], width=dm.V)
    A_in, B_in = _merge_prep(dm, Hf, Hb, P, O, mhw)
    ya = _mm(A_in, Wba, "mm_branch_a")
    yb = _mm(B_in, Wbb, "mm_branch_b")
    M_in = _gate_merge(dm, ya, yb, P)
    out = _mm(M_in, Wout, "mm_out")
    lnp = _rows8(ln_w[0], ln_b[0], width=D)
    dr_a, d_out, cs_fin, loss_p = _final(dm, x[0], out, loss_target[0], modv, lnp)
    loss = lax.psum(loss_p[0, 0], ("x", "y", "c"))

    dM = _mm(d_out, Wout, "mm_d_merge", nt=True)
    gWout = _mm(M_in.T, d_out, "mm_g_w_out", tk=2048)
    d_ya, d_yb, d_g = _merge_bwd(dm, dM, ya, yb, P)
    dA = _mm(d_ya, Wba, "mm_d_a", nt=True)
    gWba = _mm(A_in.T, d_ya, "mm_g_w_ba", tk=2048)
    dB = _mm(d_yb, Wbb, "mm_d_b", nt=True)
    gWbb = _mm(B_in.T, d_yb, "mm_g_w_bb", tk=2048)
    d_oa, d_za, d_zb, dH, dO, DEL, gmh = _branch_bwd(dm, dA, dB, Hf, Hb, P, O, mhw)
    dQr = _attn_bwd_dq(dm, Qr, Kr, Vb, dO, LSE, DEL)
    dKr, d_vb = _attn_bwd_dkv(dm, Qr, Kr, Vb, dO, jnp.transpose(LSE, (0, 2, 1)), jnp.transpose(DEL, (0, 2, 1)))
    d_qb, d_kb, gqn, gkn = _qk_bwd(dm, dQr, dKr, P, rope, qn, kn)
    acc = _mlstm_bwd(dm, qk_act, P, G, Cf, NMf, Hf, dH, None, False)
    dq, dk, dv, dG = _mlstm_bwd(dm, qk_act, P, G, Cb, NMb, Hb, dH, acc, True)
    d_qk, cs_conv = _conv_bwd(dm, dq, dk, P, convp)
    d_if, gbif = _gates_bwd(dm, dG, G)
    parts = {"g": d_g, "qk": d_qk, "va": dv.astype(BF16), "oa": d_oa, "za": d_za, "qb": d_qb,
             "zb": d_zb, "kb": d_kb, "vb": d_vb, "if": d_if}
    dP = jnp.concatenate([parts[n] for n in dm.order], axis=1)
    gsq = jnp.concatenate([gWba.reshape(N_DEV, rb, D), gWbb.reshape(N_DEV, rb, D),
                           gWout.reshape(N_DEV, rb, D)], axis=1).astype(BF16)
    gWp, gsq_all = _mm(U.T, dP, "mm_g_w_in", tm=1024, tn=896, tk=2816, exchange=("a2a", gsq))
    gW = _from_padded(dm, gWp).astype(BF16)
    gW = jnp.transpose(gW.reshape(D, N_DEV, n_in), (1, 0, 2))
    dU, gW_all = _mm(dP, Wp, "mm_d_u", nt=True, tk=1792, exchange=("a2a", gW))
    grad_x, cs_ln = _ln_mod_bwd(dm, dU, xcat, modv, dr_a)

    dmod_me = _rows8(jnp.concatenate([cs_ln[0], cs_ln[1], cs_fin[2]]),
                     jnp.concatenate([cs_ln[2], cs_ln[3], jnp.zeros((D,), F32)]), width=3 * D)
    dmod_all = _all_gather(dmod_me, "ag_dmod", False)
    dmod_loc = lax.dynamic_slice(dmod_all, (0, 0, me * n_mod), (N_DEV, 2, n_mod))
    dmod_rows = _rows8(*[dmod_loc[j, 0] for j in range(N_DEV)], jnp.sum(dmod_loc[:, 1, :], axis=0), width=n_mod)
    g_w_mod, dc_part = _mod_bwd(craw.T, dmod_rows, w_mod[0])

    PW = dm.PW
    small = _rows8(cs_fin[0], cs_fin[1], gmh[0], cs_conv[3], cs_conv[0], cs_conv[1], cs_conv[2],
                   dmod_me[0, 0:D], dmod_me[0, D:2 * D], dmod_me[0, 2 * D:3 * D],
                   dmod_me[1, 0:D], dmod_me[1, D:2 * D],
                   jnp.concatenate([gqn[0], gkn[0], gbif[0]]), dc_part[N_DEV], width=PW)
    tot = _sum_slots(_all_gather(small, "ag_small", False), "sum_small")
    g_ln_w, g_ln_b, g_mh, g_conv_b = tot[0, :D], tot[1, :D], tot[2, :dm.V], tot[3, :QK2]
    g_conv_full = tot[4:7, :QK2]
    g_b_mod = jnp.concatenate([tot[7, :D] + tot[10, :D], tot[8, :D] + tot[11, :D], tot[9, :D]])
    g_qn, g_kn, g_bif = tot[12, 0:HD_B], tot[12, HD_B:2 * HD_B], tot[12, 2 * HD_B:2 * HD_B + dm.NIF]
    g_c_ctx = _cctx_grad(tot[13:14, :D], c_ctx.reshape(1, D))[0]
    g_conv_w = lax.dynamic_slice(g_conv_full, (0, me * n_cv), (CONV_W, n_cv))

    g_w_in = _sum_slots(gW_all, "sum_g_w_in")
    g_sq = _sum_slots(gsq_all, "sum_g_w_sq")

    d_in, nm_in, nv_in = _adamw(w_in[0], g_w_in, m_w_in[0], v_w_in[0], "adam_w_in")
    d_md, nm_md, nv_md = _adamw(w_mod[0], g_w_mod, m_w_mod[0], v_w_mod[0], "adam_w_mod")
    cat3 = lambda a, b, cc: jnp.concatenate([a[0], b[0], cc[0]], axis=0)
    d_sq, nm_sq, nv_sq = _adamw(cat3(w_branch_a, w_branch_b, w_out), g_sq,
                                cat3(m_w_branch_a, m_w_branch_b, m_w_out),
                                cat3(v_w_branch_a, v_w_branch_b, v_w_out), "adam_w_sq")
    names = ["c_ctx", "b_mod", "b_if", "conv_w", "conv_b", "mh", "qn", "kn", "ln_w", "ln_b"]
    ws = [c_ctx, b_mod, b_if, conv_w, conv_b, mh_norm_w, q_norm_w, k_norm_w, ln_w, ln_b]
    ms = [m_c_ctx, m_b_mod, m_b_if, m_conv_w, m_conv_b, m_mh_norm_w, m_q_norm_w, m_k_norm_w, m_ln_w, m_ln_b]
    vs = [v_c_ctx, v_b_mod, v_b_if, v_conv_w, v_conv_b, v_mh_norm_w, v_q_norm_w, v_k_norm_w, v_ln_w, v_ln_b]
    gs = [g_c_ctx, g_b_mod, g_bif, g_conv_w, g_conv_b, g_mh, g_qn, g_kn, g_ln_w, g_ln_b]
    sizes = [a.size for a in ws]
    tot_n = sum(sizes)
    padn = -(-tot_n // LANE) * LANE
    flat = lambda arrs: jnp.pad(jnp.concatenate([a.reshape(-1) for a in arrs]), (0, padn - tot_n)).reshape(1, padn)
    d_s, nm_s, nv_s = _adamw(flat(ws), flat(gs), flat(ms), flat(vs), "adam_small")

    def split(a):
        res, o = {}, 0
        for n, wv, sz in zip(names, ws, sizes):
            res[n] = a[0, o:o + sz].reshape(wv.shape)
            o += sz
        return res

    def assemble(small_d, big_in, big_md, big_sq):
        s = small_d
        return [s["c_ctx"], big_md[None], s["b_mod"], big_in[None], s["b_if"], s["conv_w"], s["conv_b"],
                s["mh"], s["qn"], s["kn"], big_sq[None, 0:rb], big_sq[None, rb:2 * rb], big_sq[None, 2 * rb:3 * rb],
                s["ln_w"], s["ln_b"]]

    g_small = {n: g.reshape(wv.shape) for n, g, wv in zip(names, gs, ws)}
    grads = assemble(g_small, g_w_in, g_w_mod, g_sq)
    deltas = assemble(split(d_s), d_in, d_md, d_sq)
    new_m = assemble(split(nm_s), nm_in, nm_md, nm_sq)
    new_v = assemble(split(nv_s), nv_in, nv_md, nv_sq)
    return (loss, grad_x[None], *grads, *deltas, *new_m, *new_v)
```

```python
import jax
import jax.numpy as jnp
from jax import lax
from jax.experimental import pallas as pl
from jax.experimental.pallas import tpu as pltpu

F32 = jnp.float32
BF16 = jnp.bfloat16
MESH = pl.DeviceIdType.MESH
N_DEV = 8

GRID_W = 64
NH_A = 8
DK_A = 128
DV_A = 256
CONV_W = 3
CHUNK = 64
M_INIT = -1e30
NH_B = 16
NKV_B = 4
HD_B = 128
ROPE_THETA = 10000.0
EPS = 1e-6
DEPTH = 1
ALPHA = (2 * DEPTH) ** 0.25
ADAM_LR = 0.001
ADAM_B1 = 0.9
ADAM_B2 = 0.999
ADAM_EPS = 1e-08
ADAM_WD = 0.01
ADAM_STEP = 10

LANE = 128
SUBLANE = 8
VMEM_LIMIT = 56 << 20


def _tile(n, target, align):
    best = None
    t = align
    while t <= min(n, target):
        if n % t == 0:
            best = t
        t += align
    return best if best is not None else n


class Dims:
    def __init__(self, S, Tc, D):
        self.S, self.Tc, self.D = S, Tc, D
        self.T = S + Tc
        self.QK = NH_A * DK_A
        self.V = NH_A * DV_A
        self.QB = NH_B * HD_B
        self.KVB = NKV_B * HD_B
        self.G = NH_B // NKV_B
        self.NIF = 4 * NH_A
        self.IFP = 512 if self.KVB % 512 == 0 else LANE
        self.ref_widths = [2 * self.QK, self.V, self.NIF, self.KVB, self.KVB,
                           self.V, self.V, self.QB, self.QB, 2 * D]
        self.ref_names = ["qk", "va", "if", "kb", "vb", "oa", "za", "qb", "zb", "g"]
        self.N_IN = sum(self.ref_widths)
        self.order = ["g", "qk", "va", "oa", "za", "qb", "zb", "kb", "vb", "if"]
        w = dict(zip(self.ref_names, self.ref_widths))
        w["if"] = self.IFP
        self.w = w
        self.off = {}
        o = 0
        for n in self.order:
            assert o % w[n] == 0, (n, o, w[n])
            self.off[n] = o
            o += w[n]
        self.NP = o
        self.TB = min(256, Tc)
        assert Tc % self.TB == 0 and S % self.TB == 0 and self.TB % CHUNK == 0
        self.nctx = Tc // self.TB
        self.nlat = S // self.TB
        self.nblk = self.nctx + self.nlat
        self.PW = max(D, self.V, 2 * self.QK, 3 * LANE)


def _cparams(sem):
    return pltpu.CompilerParams(dimension_semantics=sem, vmem_limit_bytes=VMEM_LIMIT)


def _sigmoid(x):
    return 1.0 / (1.0 + jnp.exp(-x))


def _my_pos():
    return lax.axis_index("x"), lax.axis_index("y"), lax.axis_index("c")


def _all_gather(x, name, big):
    R, C = x.shape
    space = pl.ANY if big else pltpu.VMEM

    def body(x_ref, out_ref, send_sems, recv_sems, local_sem):
        px, py, pc = _my_pos()
        me, sibling = (px, py, pc), (px, py, 1 - pc)
        chips = [(1 - px, py), (px, 1 - py), (1 - px, 1 - py)]

        def slot(bx, by, bc):
            return out_ref.at[4 * bx + 2 * by + bc]

        def copy(k, block, to, src=None):
            return pltpu.make_async_remote_copy(
                src_ref=slot(*block) if src is None else src, dst_ref=slot(*block),
                send_sem=send_sems.at[k], recv_sem=recv_sems.at[k],
                device_id=to, device_id_type=MESH)

        mine = pltpu.make_async_copy(x_ref, slot(*me), local_sem)
        mine.start()
        first = [copy(0, me, sibling, src=x_ref)]
        first += [copy(1 + j, me, (*chip, pc), src=x_ref) for j, chip in enumerate(chips)]
        for cp in first:
            cp.start()
        passed = [copy(4 + j, (*chip, pc), sibling) for j, chip in enumerate(chips)]
        for j, chip in enumerate(chips):
            copy(1 + j, (*chip, pc), me).wait_recv()
            passed[j].start()
        copy(0, sibling, me).wait_recv()
        for j, chip in enumerate(chips):
            copy(4 + j, (*chip, 1 - pc), me).wait_recv()
        for cp in first + passed:
            cp.wait_send()
        mine.wait()

    return pl.pallas_call(
        body, name=name,
        out_shape=jax.ShapeDtypeStruct((N_DEV, R, C), x.dtype),
        in_specs=[pl.BlockSpec(memory_space=space)],
        out_specs=pl.BlockSpec(memory_space=space),
        scratch_shapes=[pltpu.SemaphoreType.DMA((7,)), pltpu.SemaphoreType.DMA((7,)),
                        pltpu.SemaphoreType.DMA],
    )(x)


EXCHANGE_SEMS = [pltpu.SemaphoreType.DMA((N_DEV - 1,)), pltpu.SemaphoreType.DMA((N_DEV - 1,)),
                 pltpu.SemaphoreType.DMA]


def _exchange(kind, src_ref, land_ref, send_sems, recv_sems, local_sem):
    def copies():
        px, py, pc = _my_pos()
        me = 4 * px + 2 * py + pc
        own = src_ref if kind == "ag" else src_ref.at[me]
        local = pltpu.make_async_copy(own, land_ref.at[me], local_sem)
        sends, recvs = [], []
        for r in range(1, N_DEV):
            dx, dy, dc = (r >> 2) & 1, (r >> 1) & 1, r & 1
            qx = px if dx == 0 else 1 - px
            qy = py if dy == 0 else 1 - py
            qc = pc if dc == 0 else 1 - pc
            peer = 4 * qx + 2 * qy + qc
            sems = dict(send_sem=send_sems.at[r - 1], recv_sem=recv_sems.at[r - 1],
                        device_id=(qx, qy, qc), device_id_type=MESH)
            sends.append(pltpu.make_async_remote_copy(
                src_ref=src_ref if kind == "ag" else src_ref.at[peer], dst_ref=land_ref.at[me], **sems))
            recvs.append(pltpu.make_async_remote_copy(src_ref=own, dst_ref=land_ref.at[peer], **sems))
        return local, sends, recvs

    def start():
        local, sends, _ = copies()
        local.start()
        for cp in sends:
            cp.start()

    def wait():
        local, sends, recvs = copies()
        for cp in recvs:
            cp.wait_recv()
        for cp in sends:
            cp.wait_send()
        local.wait()

    return start, wait


def _land_shape(kind, src):
    return jax.ShapeDtypeStruct(src.shape if kind == "a2a" else (N_DEV,) + src.shape, src.dtype)


def _sum_slots(a, name):
    _, R, C = a.shape
    tb = _tile(R, max(SUBLANE, (1 << 20) // (4 * C) // SUBLANE * SUBLANE), SUBLANE)

    def kern(a_ref, o_ref):
        acc = a_ref[0].astype(F32)
        for j in range(1, N_DEV):
            acc = acc + a_ref[j].astype(F32)
        o_ref[...] = acc

    return pl.pallas_call(
        kern, name=name, grid=(R // tb,),
        in_specs=[pl.BlockSpec((N_DEV, tb, C), lambda i: (0, i, 0))],
        out_specs=pl.BlockSpec((tb, C), lambda i: (i, 0)),
        out_shape=jax.ShapeDtypeStruct((R, C), F32),
        compiler_params=_cparams(("parallel",)),
    )(a)


def _mm(a, b, name, nt=False, tm=768, tn=1024, tk=2048, out_dtype=F32, exchange=None):
    M, K = a.shape
    N = b.shape[0] if nt else b.shape[1]
    assert (b.shape[1] if nt else b.shape[0]) == K
    tm, tn, tk = _tile(M, tm, 16), _tile(N, tn, LANE), _tile(K, tk, LANE)
    ni, nj, nk = M // tm, N // tn, K // tk

    def dot(x, y):
        if nt:
            return lax.dot_general(x, y, (((1,), (1,)), ((), ())), preferred_element_type=F32)
        return jnp.dot(x, y, preferred_element_type=F32)

    def kern(a_ref, b_ref, *rest):
        if exchange is not None:
            src_ref, o_ref, land_ref, acc_ref, send_sems, recv_sems, local_sem = rest
            start, wait = _exchange(exchange[0], src_ref, land_ref, send_sems, recv_sems, local_sem)
            i, j, kk = pl.program_id(0), pl.program_id(1), pl.program_id(2)
            pl.when((i == 0) & (j == 0) & (kk == 0))(start)
        else:
            o_ref, acc_ref = rest
        k = pl.program_id(2)
        part = dot(a_ref[...], b_ref[...])
        if nk == 1:
            o_ref[...] = part.astype(o_ref.dtype)
        else:
            @pl.when(k == 0)
            def _():
                acc_ref[...] = part

            @pl.when(k > 0)
            def _():
                acc_ref[...] += part

            @pl.when(k == nk - 1)
            def _():
                o_ref[...] = acc_ref[...].astype(o_ref.dtype)

        if exchange is not None:
            pl.when((i == ni - 1) & (j == nj - 1) & (kk == nk - 1))(wait)

    b_spec = (pl.BlockSpec((tn, tk), lambda i, j, k: (j, k)) if nt
              else pl.BlockSpec((tk, tn), lambda i, j, k: (k, j)))
    in_specs = [pl.BlockSpec((tm, tk), lambda i, j, k: (i, k)), b_spec]
    out_specs = pl.BlockSpec((tm, tn), lambda i, j, k: (i, j))
    out_shape = jax.ShapeDtypeStruct((M, N), out_dtype)
    scratch = [pltpu.VMEM((tm, tn) if nk > 1 else (SUBLANE, LANE), F32)]
    args = (a, b)
    sem = ("parallel", "parallel", "arbitrary")
    if exchange is not None:
        in_specs = in_specs + [pl.BlockSpec(memory_space=pl.ANY)]
        out_specs = [out_specs, pl.BlockSpec(memory_space=pl.ANY)]
        out_shape = [out_shape, _land_shape(*exchange)]
        scratch = scratch + EXCHANGE_SEMS
        args = (a, b, exchange[1])
        sem = ("arbitrary", "arbitrary", "arbitrary")
    return pl.pallas_call(
        kern, name=name, grid=(ni, nj, nk), in_specs=in_specs, out_specs=out_specs,
        out_shape=out_shape, scratch_shapes=scratch, compiler_params=_cparams(sem),
    )(*args)


def _ln_stats(x):
    mu = jnp.mean(x, axis=-1, keepdims=True)
    xc = x - mu
    var = jnp.mean(xc * xc, axis=-1, keepdims=True)
    rstd = lax.rsqrt(var + EPS)
    return xc * rstd, rstd


def _row_specs(dm):
    TB, D, nctx = dm.TB, dm.D, dm.nctx
    return [pl.BlockSpec((TB, D), lambda i: (jnp.minimum(i, nctx - 1), 0)),
            pl.BlockSpec((TB, D), lambda i: (jnp.maximum(i - nctx, 0), 0))]


def _ln_mod_fwd(dm, ctx2, x2, modv):
    TB, D, nctx = dm.TB, dm.D, dm.nctx

    def kern(c_ref, x_ref, m_ref, u_ref, ut_ref):
        is_ctx = pl.program_id(0) < nctx
        xh, _ = _ln_stats(jnp.where(is_ctx, c_ref[...], x_ref[...]))
        shift = jnp.where(is_ctx, m_ref[2:3, :], m_ref[0:1, :])
        scale = jnp.where(is_ctx, m_ref[3:4, :], m_ref[1:2, :])
        u = xh * (1.0 + scale) + shift
        u_ref[...] = u.astype(BF16)
        ut_ref[...] = u.T.astype(BF16)

    return pl.pallas_call(
        kern, name="ln_mod_fwd", grid=(dm.nblk,),
        in_specs=_row_specs(dm) + [pl.BlockSpec((SUBLANE, D), lambda i: (0, 0))],
        out_specs=[pl.BlockSpec((TB, D), lambda i: (i, 0)), pl.BlockSpec((D, TB), lambda i: (0, i))],
        out_shape=[jax.ShapeDtypeStruct((dm.T, D), BF16), jax.ShapeDtypeStruct((D, dm.T), BF16)],
        compiler_params=_cparams(("parallel",)),
    )(ctx2, x2, modv)


def _conv_specs(dm, W, col):
    TB, T = dm.TB, dm.T
    r8 = TB // SUBLANE
    last8 = T // SUBLANE - 1
    return [pl.BlockSpec((TB, W), lambda i: (i, col)),
            pl.BlockSpec((SUBLANE, W), lambda i: (jnp.maximum(i * r8 - 1, 0), col)),
            pl.BlockSpec((SUBLANE, W), lambda i: (jnp.minimum((i + 1) * r8, last8), col))]


def _shifted(dm, x, prev_ref, next_ref):
    TB, nctx, nblk = dm.TB, dm.nctx, dm.nblk
    i = pl.program_id(0)
    row = lax.broadcasted_iota(jnp.int32, x.shape, 0)
    zero_prev = (i == 0) | (i == nctx)
    zero_next = (i == nctx - 1) | (i == nblk - 1)
    before = jnp.where(zero_prev, 0.0, prev_ref[SUBLANE - 1:SUBLANE, :])
    after = jnp.where(zero_next, 0.0, next_ref[0:1, :])
    xp = jnp.where(row == 0, before, pltpu.roll(x, 1, 0))
    xn = jnp.where(row == TB - 1, after, pltpu.roll(x, TB - 1, 0))
    return xp, xn


def _conv_silu_fwd(dm, P, convp):
    W = 2 * dm.QK
    col = dm.off["qk"] // W
    kscale = DK_A ** -0.5

    def kern(x_ref, p_ref, n_ref, c_ref, o_ref):
        x = x_ref[...]
        xp, xn = _shifted(dm, x, p_ref, n_ref)
        z = c_ref[3:4, :] + xp * c_ref[0:1, :] + x * c_ref[1:2, :] + xn * c_ref[2:3, :]
        lane = lax.broadcasted_iota(jnp.int32, (1, W), 1)
        cs = jnp.where(lane >= dm.QK, kscale, 1.0)
        o_ref[...] = z * _sigmoid(z) * cs

    return pl.pallas_call(
        kern, name="conv_silu_fwd", grid=(dm.nblk,),
        in_specs=_conv_specs(dm, W, col) + [pl.BlockSpec((SUBLANE, W), lambda i: (0, 0))],
        out_specs=pl.BlockSpec((dm.TB, W), lambda i: (i, 0)),
        out_shape=jax.ShapeDtypeStruct((dm.T, W), F32),
        compiler_params=_cparams(("parallel",)),
    )(P, P, P, convp)


def _gates_fwd(dm, P, bif):
    col = dm.off["if"] // LANE

    def kern(x_ref, b_ref, o_ref):
        x = x_ref[...] + b_ref[0:1, :]
        lane = lax.broadcasted_iota(jnp.int32, x.shape, 1)
        is_f = ((lane // NH_A) % 2) == 1
        ls = jnp.minimum(x, 0.0) - jnp.log(1.0 + jnp.exp(-jnp.abs(x)))
        o_ref[...] = jnp.where(lane < dm.NIF, jnp.where(is_f, ls, x), 0.0)

    return pl.pallas_call(
        kern, name="gates_fwd", grid=(dm.nblk,),
        in_specs=[pl.BlockSpec((dm.TB, LANE), lambda i: (i, col)),
                  pl.BlockSpec((SUBLANE, LANE), lambda i: (0, 0))],
        out_specs=pl.BlockSpec((dm.TB, LANE), lambda i: (i, 0)),
        out_shape=jax.ShapeDtypeStruct((dm.T, LANE), F32),
        compiler_params=_cparams(("parallel",)),
    )(P, bif)


def _mlstm_order(dm, reverse, backward):
    nctx, nblk = dm.nctx, dm.nblk

    def idx(i):
        if backward:
            i = nblk - 1 - i
        if not reverse:
            return i
        return jnp.where(i < nctx, nctx - 1 - i, nblk - 1 - (i - nctx))

    return idx


def _chunk_gates(g, ci, cf, mask_f, maskT_f, eye_f):
    lane = lax.broadcasted_iota(jnp.int32, g.shape, 1)
    gi_c = jnp.sum(jnp.where(lane == ci, g, 0.0), axis=1, keepdims=True)
    gf_c = jnp.sum(jnp.where(lane == cf, g, 0.0), axis=1, keepdims=True)
    gi_r = jnp.sum(eye_f * gi_c, axis=0, keepdims=True)
    gf_r = jnp.sum(eye_f * gf_c, axis=0, keepdims=True)
    b_c = jnp.sum(mask_f * gf_r, axis=1, keepdims=True)
    b_r = jnp.sum(maskT_f * gf_c, axis=0, keepdims=True)
    return gi_c, gi_r, b_c, b_r


def _chunk_masks(reverse):
    L = CHUNK
    r = lax.broadcasted_iota(jnp.int32, (L, L), 0)
    c = lax.broadcasted_iota(jnp.int32, (L, L), 1)
    mask = (c >= r) if reverse else (c <= r)
    maskT = (r >= c) if reverse else (r <= c)
    return mask, mask.astype(F32), maskT.astype(F32), (r == c).astype(F32)


def _pick_row(x, e):
    r = lax.broadcasted_iota(jnp.int32, x.shape, 0)
    return jnp.sum(jnp.where(r == e, x, 0.0), axis=0, keepdims=True)


def _dot_nt(a, b):
    return lax.dot_general(a, b, (((1,), (1,)), ((), ())), preferred_element_type=F32)


def _dot(a, b):
    return jnp.dot(a, b, preferred_element_type=F32)


def _chunk_fwd_core(q, k, g, ci, cf, C0, n0, m0, masks, reverse):
    mask, mask_f, maskT_f, eye_f = masks
    gi_c, gi_r, b_c, b_r = _chunk_gates(g, ci, cf, mask_f, maskT_f, eye_f)
    d = jnp.where(mask, b_c - b_r + gi_r, -jnp.inf)
    m_c = jnp.maximum(b_c + m0, jnp.max(d, axis=1, keepdims=True))
    w = jnp.exp(d - m_c)
    a_c = jnp.exp(b_c + m0 - m_c)
    qb, kb = q.astype(BF16), k.astype(BF16)
    s = _dot_nt(qb, kb) * w
    den = a_c * jnp.sum(q * n0, axis=1, keepdims=True) + jnp.sum(s, axis=1, keepdims=True)
    e = 0 if reverse else CHUNK - 1
    m_end, b_end, a_end = _pick_row(m_c, e), _pick_row(b_c, e), _pick_row(a_c, e)
    w_end = jnp.exp(b_end - b_c + gi_c - m_end)
    return qb, kb, s, w, a_c, m_c, den, w_end, a_end, m_end


MLSTM_HEADS_PER_STEP = 4


def _heads_per_step():
    return MLSTM_HEADS_PER_STEP if NH_A % MLSTM_HEADS_PER_STEP == 0 else 1


def _mlstm_fwd(dm, qk_act, P, G, reverse):
    TB, T = dm.TB, dm.T
    NC = TB // CHUNK
    idx = _mlstm_order(dm, reverse, False)
    vcol = dm.off["va"] // DV_A
    base = 2 * NH_A if reverse else 0

    HP = _heads_per_step()
    hcols = lambda hh, w: slice(hh * w, (hh + 1) * w)

    def kern(q_ref, k_ref, v_ref, g_ref, h_ref, cst_ref, nm_ref, C_s, N_s):
        i, hp = pl.program_id(0), pl.program_id(1)
        row8 = lax.broadcasted_iota(jnp.int32, (SUBLANE, DK_A), 0)
        heads = [hp * HP + hh for hh in range(HP)]

        @pl.when(i == 0)
        def _():
            for h in heads:
                C_s[h] = jnp.zeros((DK_A, DV_A), F32)
                N_s[h] = jnp.where(row8 == 1, M_INIT, 0.0)

        masks = _chunk_masks(reverse)
        state = [(C_s[h], N_s.at[h][0:1, :], N_s.at[h][1:2, 0:1]) for h in heads]
        for c in (range(NC - 1, -1, -1) if reverse else range(NC)):
            rows = pl.ds(c * CHUNK, CHUNK)
            g = g_ref[rows, :]
            for hh, h in enumerate(heads):
                C0, n0, m0 = state[hh]
                q, k, v = q_ref[rows, hcols(hh, DK_A)], k_ref[rows, hcols(hh, DK_A)], v_ref[rows, hcols(hh, DV_A)]
                cst_ref[hh, c] = C0
                nm_ref[hh, c] = jnp.where(row8 == 0, n0, jnp.where(row8 == 1, m0, 0.0))
                qb, kb, s, w, a_c, m_c, den, w_end, a_end, m_end = _chunk_fwd_core(
                    q, k, g, base + h, base + NH_A + h, C0, n0, m0, masks, reverse)
                vb = v.astype(BF16)
                num = a_c * _dot(qb, C0.astype(BF16)) + _dot(s.astype(BF16), vb)
                h_ref[rows, hcols(hh, DV_A)] = num / jnp.maximum(jnp.abs(den), jnp.exp(-m_c))
                state[hh] = (a_end * C0 + _dot(k.T.astype(BF16), (w_end * v).astype(BF16)),
                             a_end * n0 + jnp.sum(w_end * k, axis=0, keepdims=True), m_end)
        for hh, h in enumerate(heads):
            C0, n0, m0 = state[hh]
            C_s[h] = C0
            N_s[h] = jnp.where(row8 == 0, n0, jnp.where(row8 == 1, m0, 0.0))

    nch = T // CHUNK
    NG = NH_A // HP
    return pl.pallas_call(
        kern, name="mlstm_fwd_rev" if reverse else "mlstm_fwd", grid=(dm.nblk, NG),
        in_specs=[pl.BlockSpec((TB, HP * DK_A), lambda i, h: (idx(i), h)),
                  pl.BlockSpec((TB, HP * DK_A), lambda i, h: (idx(i), NG + h)),
                  pl.BlockSpec((TB, HP * DV_A), lambda i, h: (idx(i), vcol // HP + h)),
                  pl.BlockSpec((TB, LANE), lambda i, h: (idx(i), 0))],
        out_specs=[pl.BlockSpec((TB, HP * DV_A), lambda i, h: (idx(i), h)),
                   pl.BlockSpec((HP, NC, DK_A, DV_A), lambda i, h: (h, idx(i), 0, 0)),
                   pl.BlockSpec((HP, NC, SUBLANE, DK_A), lambda i, h: (h, idx(i), 0, 0))],
        out_shape=[jax.ShapeDtypeStruct((T, dm.V), F32),
                   jax.ShapeDtypeStruct((NH_A, nch, DK_A, DV_A), F32),
                   jax.ShapeDtypeStruct((NH_A, nch, SUBLANE, DK_A), F32)],
        scratch_shapes=[pltpu.VMEM((NH_A, DK_A, DV_A), F32), pltpu.VMEM((NH_A, SUBLANE, DK_A), F32)],
        compiler_params=_cparams(("arbitrary", "arbitrary")),
    )(qk_act, qk_act, P, G)


def _mlstm_bwd(dm, qk_act, P, G, Cst, NM, H, dH, acc, reverse):
    TB, T = dm.TB, dm.T
    NC = TB // CHUNK
    idx = _mlstm_order(dm, reverse, True)
    vcol = dm.off["va"] // DV_A
    base = 2 * NH_A if reverse else 0
    has_acc = acc is not None
    HP = _heads_per_step()
    hcols = lambda hh, w: slice(hh * w, (hh + 1) * w)

    def kern(*refs):
        (q_ref, k_ref, v_ref, g_ref, cst_ref, nm_ref, hh_ref, dh_ref) = refs[:8]
        p = 8
        if has_acc:
            aq_ref, ak_ref, av_ref, ag_ref = refs[p:p + 4]
            p += 4
        dq_ref, dk_ref, dv_ref, dg_ref, R_s, Rn_s = refs[p:p + 6]
        i, hp = pl.program_id(0), pl.program_id(1)
        row8 = lax.broadcasted_iota(jnp.int32, (SUBLANE, DK_A), 0)
        heads = [hp * HP + hh for hh in range(HP)]

        @pl.when(i == 0)
        def _():
            for h in heads:
                R_s[h] = jnp.zeros((DK_A, DV_A), F32)
                Rn_s[h] = jnp.zeros((SUBLANE, DK_A), F32)

        @pl.when(hp == 0)
        def _():
            dg_ref[...] = ag_ref[...] if has_acc else jnp.zeros((TB, LANE), F32)

        masks = _chunk_masks(reverse)
        _, mask_f, maskT_f, eye_f = masks
        before_f = mask_f - eye_f
        state = [(R_s[h], Rn_s.at[h][0:1, :]) for h in heads]
        lane = lax.broadcasted_iota(jnp.int32, (CHUNK, LANE), 1)

        def as_row(col):
            return jnp.sum(eye_f * col, axis=0, keepdims=True)

        for c in (range(NC) if reverse else range(NC - 1, -1, -1)):
            rows = pl.ds(c * CHUNK, CHUNK)
            g = g_ref[rows, :]
            dg = jnp.zeros((CHUNK, LANE), F32)
            for hh, h in enumerate(heads):
                R, Rn = state[hh]
                qc, vc = hcols(hh, DK_A), hcols(hh, DV_A)
                q, k, v = q_ref[rows, qc], k_ref[rows, qc], v_ref[rows, vc]
                C0 = cst_ref[hh, c]
                n0, m0 = nm_ref.at[hh, c][0:1, :], nm_ref.at[hh, c][1:2, 0:1]
                qb, kb, s, w, a_c, m_c, den, w_end, a_end, _ = _chunk_fwd_core(
                    q, k, g, base + h, base + NH_A + h, C0, n0, m0, masks, reverse)
                vb = v.astype(BF16)
                e_m = jnp.exp(-m_c)
                r = 1.0 / jnp.maximum(jnp.abs(den), e_m)
                dh = dh_ref[rows, vc]
                dN = dh * r
                dD = jnp.where(jnp.abs(den) > e_m,
                               -jnp.sum(dh * hh_ref[rows, vc], axis=1, keepdims=True) * r * jnp.sign(den), 0.0)
                dNb = dN.astype(BF16)
                dS = _dot_nt(dNb, vb) + dD
                dqk = dS * w
                Cb, Rb = C0.astype(BF16), R.astype(BF16)
                dq_in = a_c * (_dot_nt(dNb, Cb) + dD * n0)
                dk_out = w_end * (_dot_nt(vb, Rb) + Rn)
                dq = _dot(dqk.astype(BF16), kb) + dq_in
                dk = _dot(dqk.T.astype(BF16), qb) + dk_out
                dv = _dot(s.T.astype(BF16), dNb) + w_end * _dot(kb, Rb)
                if has_acc:
                    dq_ref[rows, qc] = aq_ref[rows, qc] + dq
                    dk_ref[rows, qc] = ak_ref[rows, qc] + dk
                    dv_ref[rows, vc] = av_ref[rows, vc] + dv
                else:
                    dq_ref[rows, qc] = dq
                    dk_ref[rows, qc] = dk
                    dv_ref[rows, vc] = dv
                gm = dS * s
                g_row = jnp.sum(gm, axis=1, keepdims=True)
                g_col = jnp.sum(eye_f * jnp.sum(gm, axis=0, keepdims=True), axis=1, keepdims=True)
                q_in = jnp.sum(q * dq_in, axis=1, keepdims=True)
                k_out = jnp.sum(k * dk_out, axis=1, keepdims=True)
                through = a_end * (jnp.sum(jnp.sum(R * C0, axis=1, keepdims=True), axis=0, keepdims=True)
                                   + jnp.sum(Rn * n0, axis=1, keepdims=True))
                di = g_col + k_out
                df = (jnp.sum(maskT_f * as_row(g_row - g_col + q_in), axis=1, keepdims=True)
                      + jnp.sum(before_f * as_row(k_out), axis=1, keepdims=True) + through)
                dg = dg + jnp.where(lane == base + h, di, 0.0) + jnp.where(lane == base + NH_A + h, df, 0.0)
                aq = a_c * q
                state[hh] = (a_end * R + _dot(aq.T.astype(BF16), dNb),
                             a_end * Rn + jnp.sum(aq * dD, axis=0, keepdims=True))
            dg_ref[rows, :] += dg
        for hh, h in enumerate(heads):
            R, Rn = state[hh]
            R_s[h] = R
            Rn_s[h] = jnp.where(row8 == 0, Rn, 0.0)

    NG = NH_A // HP
    qspec = pl.BlockSpec((TB, HP * DK_A), lambda i, h: (idx(i), h))
    vspec = pl.BlockSpec((TB, HP * DV_A), lambda i, h: (idx(i), h))
    gspec = pl.BlockSpec((TB, LANE), lambda i, h: (idx(i), 0))
    in_specs = [qspec,
                pl.BlockSpec((TB, HP * DK_A), lambda i, h: (idx(i), NG + h)),
                pl.BlockSpec((TB, HP * DV_A), lambda i, h: (idx(i), vcol // HP + h)),
                gspec,
                pl.BlockSpec((HP, NC, DK_A, DV_A), lambda i, h: (h, idx(i), 0, 0)),
                pl.BlockSpec((HP, NC, SUBLANE, DK_A), lambda i, h: (h, idx(i), 0, 0)),
                vspec, vspec]
    args = [qk_act, qk_act, P, G, Cst, NM, H, dH]
    if has_acc:
        in_specs += [qspec, qspec, vspec, gspec]
        args += list(acc)
    return pl.pallas_call(
        kern, name="mlstm_bwd_rev" if reverse else "mlstm_bwd", grid=(dm.nblk, NG),
        in_specs=in_specs,
        out_specs=[qspec, qspec, vspec, gspec],
        out_shape=[jax.ShapeDtypeStruct((T, dm.QK), F32), jax.ShapeDtypeStruct((T, dm.QK), F32),
                   jax.ShapeDtypeStruct((T, dm.V), F32), jax.ShapeDtypeStruct((T, LANE), F32)],
        scratch_shapes=[pltpu.VMEM((NH_A, DK_A, DV_A), F32), pltpu.VMEM((NH_A, SUBLANE, DK_A), F32)],
        compiler_params=_cparams(("arbitrary", "arbitrary")),
    )(*args)


def _rms_heads(x, w_row, nh, hd):
    out = []
    for h in range(nh):
        xh = x[:, h * hd:(h + 1) * hd]
        rstd = lax.rsqrt(jnp.mean(xh * xh, axis=1, keepdims=True) + EPS)
        out.append((xh * rstd, rstd))
    return out


def _rope(x, cos, sa, sb):
    return x * cos + pltpu.roll(x, HD_B - HD_B // 4, 1) * sa + pltpu.roll(x, HD_B // 4, 1) * sb


def _rope_t(dy, cos, sa, sb):
    return dy * cos + pltpu.roll(dy * sa, HD_B // 4, 1) + pltpu.roll(dy * sb, HD_B - HD_B // 4, 1)


ATT_SCALE = HD_B ** -0.5
LOG2E = 1.4426950408889634
LN2 = 0.6931471805599453
QSCALE = ATT_SCALE * LOG2E


def _qk_prep(dm, P, rope, qn, kn):
    TB, nctx = dm.TB, dm.nctx
    qcol, kcol, vcol = dm.off["qb"] // dm.QB, dm.off["kb"] // dm.KVB, dm.off["vb"] // dm.KVB

    def kern_q(x_ref, t_ref, w_ref, o_ref):
        cos, sa, sb = t_ref[0], t_ref[1], t_ref[2]
        for h, (xn, _) in enumerate(_rms_heads(x_ref[...], None, NH_B, HD_B)):
            o_ref[h] = (_rope(xn * w_ref[0:1, :], cos, sa, sb) * QSCALE).astype(BF16)

    Qr = pl.pallas_call(
        kern_q, name="q_prep", grid=(dm.nlat,),
        in_specs=[pl.BlockSpec((TB, dm.QB), lambda i: (i + nctx, qcol)),
                  pl.BlockSpec((3, TB, HD_B), lambda i: (0, i + nctx, 0)),
                  pl.BlockSpec((SUBLANE, HD_B), lambda i: (0, 0))],
        out_specs=pl.BlockSpec((NH_B, TB, HD_B), lambda i: (0, i, 0)),
        out_shape=jax.ShapeDtypeStruct((NH_B, dm.S, HD_B), BF16),
        compiler_params=_cparams(("parallel",)),
    )(P, rope, qn)

    def kern_k(x_ref, v_ref, t_ref, w_ref, o_ref, vo_ref):
        cos, sa, sb = t_ref[0], t_ref[1], t_ref[2]
        for h, (xn, _) in enumerate(_rms_heads(x_ref[...], None, NKV_B, HD_B)):
            o_ref[:, h * HD_B:(h + 1) * HD_B] = _rope(xn * w_ref[0:1, :], cos, sa, sb).astype(BF16)
        vo_ref[...] = v_ref[...].astype(BF16)

    Kr, Vb = pl.pallas_call(
        kern_k, name="k_prep", grid=(dm.nblk,),
        in_specs=[pl.BlockSpec((TB, dm.KVB), lambda i: (i, kcol)),
                  pl.BlockSpec((TB, dm.KVB), lambda i: (i, vcol)),
                  pl.BlockSpec((3, TB, HD_B), lambda i: (0, i, 0)),
                  pl.BlockSpec((SUBLANE, HD_B), lambda i: (0, 0))],
        out_specs=[pl.BlockSpec((TB, dm.KVB), lambda i: (i, 0))] * 2,
        out_shape=[jax.ShapeDtypeStruct((dm.T, dm.KVB), BF16)] * 2,
        compiler_params=_cparams(("parallel",)),
    )(P, P, rope, kn)
    return Qr, Kr, Vb


def _attn_tiles(dm):
    return _tile(dm.S, 512, LANE), _tile(dm.T, 768, LANE)


def _attn_fwd(dm, Qr, Kr, Vb):
    S, T, G = dm.S, dm.T, dm.G
    tq, tk = _tile(S, 128, LANE), T
    nk = T // tk

    def kern(q_ref, k_ref, v_ref, o_ref, l_ref, m_s, l_s, a_s):
        j = pl.program_id(2)
        k, v = k_ref[...], v_ref[...]
        if nk == 1:
            for h in range(G):
                s = _dot_nt(q_ref[h], k)
                m = jnp.max(s, axis=1, keepdims=True)
                p = jnp.exp2(s - m)
                l = jnp.sum(p, axis=1, keepdims=True)
                o_ref[h] = _dot(p.astype(BF16), v) / l
                l_ref[0, :, h:h + 1] = m + jnp.log(l) * LOG2E
            return

        @pl.when(j == 0)
        def _():
            m_s[...] = jnp.full(m_s.shape, -jnp.inf, F32)
            l_s[...] = jnp.zeros(l_s.shape, F32)
            a_s[...] = jnp.zeros(a_s.shape, F32)

        for h in range(G):
            s = _dot_nt(q_ref[h], k)
            m_old = m_s[h]
            m_new = jnp.maximum(m_old, jnp.max(s, axis=1, keepdims=True))
            p = jnp.exp2(s - m_new)
            corr = jnp.exp2(m_old - m_new)
            l_s[h] = corr * l_s[h] + jnp.sum(p, axis=1, keepdims=True)
            m_s[h] = m_new
            a_s[h] = corr * a_s[h] + _dot(p.astype(BF16), v)

        @pl.when(j == nk - 1)
        def _():
            for h in range(G):
                o_ref[h] = a_s[h] / l_s[h]
                l_ref[0, :, h:h + 1] = m_s[h] + jnp.log(l_s[h]) * LOG2E

    qspec = pl.BlockSpec((G, tq, HD_B), lambda g, i, j: (g, i, 0))
    return pl.pallas_call(
        kern, name="attn_fwd", grid=(NKV_B, S // tq, nk),
        in_specs=[qspec,
                  pl.BlockSpec((tk, HD_B), lambda g, i, j: (j, g)),
                  pl.BlockSpec((tk, HD_B), lambda g, i, j: (j, g))],
        out_specs=[qspec, pl.BlockSpec((1, tq, G), lambda g, i, j: (g, i, 0))],
        out_shape=[jax.ShapeDtypeStruct((NH_B, S, HD_B), F32), jax.ShapeDtypeStruct((NKV_B, S, G), F32)],
        scratch_shapes=[pltpu.VMEM((G, tq, 1), F32), pltpu.VMEM((G, tq, 1), F32),
                        pltpu.VMEM((G, tq, HD_B), F32)],
        compiler_params=_cparams(("parallel", "parallel", "arbitrary")),
    )(Qr, Kr, Vb)


def _attn_bwd_dq(dm, Qr, Kr, Vb, dO, LSE, DEL):
    S, T, G = dm.S, dm.T, dm.G
    tq, tk = _tile(S, 128, LANE), T
    nk = T // tk

    def kern(q_ref, k_ref, v_ref, do_ref, l_ref, d_ref, dq_ref, a_s):
        j = pl.program_id(2)
        k, v = k_ref[...], v_ref[...]
        if nk == 1:
            for h in range(G):
                p = jnp.exp2(_dot_nt(q_ref[h], k) - l_ref[0, :, h:h + 1])
                dp = _dot_nt(do_ref[h], v)
                dq_ref[h] = _dot((p * (dp - d_ref[0, :, h:h + 1])).astype(BF16), k) * ATT_SCALE
            return

        @pl.when(j == 0)
        def _():
            a_s[...] = jnp.zeros(a_s.shape, F32)

        for h in range(G):
            p = jnp.exp2(_dot_nt(q_ref[h], k) - l_ref[0, :, h:h + 1])
            dp = _dot_nt(do_ref[h], v)
            a_s[h] += _dot((p * (dp - d_ref[0, :, h:h + 1])).astype(BF16), k)

        @pl.when(j == nk - 1)
        def _():
            for h in range(G):
                dq_ref[h] = a_s[h] * ATT_SCALE

    qspec = pl.BlockSpec((G, tq, HD_B), lambda g, i, j: (g, i, 0))
    kspec = pl.BlockSpec((tk, HD_B), lambda g, i, j: (j, g))
    lspec = pl.BlockSpec((1, tq, G), lambda g, i, j: (g, i, 0))
    return pl.pallas_call(
        kern, name="attn_bwd_dq", grid=(NKV_B, S // tq, nk),
        in_specs=[qspec, kspec, kspec, qspec, lspec, lspec],
        out_specs=qspec,
        out_shape=jax.ShapeDtypeStruct((NH_B, S, HD_B), F32),
        scratch_shapes=[pltpu.VMEM((G, tq, HD_B), F32)],
        compiler_params=_cparams(("parallel", "parallel", "arbitrary")),
    )(Qr, Kr, Vb, dO, LSE, DEL)


def _attn_bwd_dkv(dm, Qr, Kr, Vb, dO, LSE_T, DEL_T):
    S, T, G = dm.S, dm.T, dm.G
    tq, tk = _attn_tiles(dm)
    nq = S // tq

    def kern(q_ref, k_ref, v_ref, do_ref, l_ref, d_ref, dk_ref, dv_ref, ak_s, av_s):
        i = pl.program_id(2)

        @pl.when(i == 0)
        def _():
            ak_s[...] = jnp.zeros(ak_s.shape, F32)
            av_s[...] = jnp.zeros(av_s.shape, F32)

        k, v = k_ref[...], v_ref[...]
        for h in range(G):
            q, do = q_ref[h], do_ref[h]
            pT = jnp.exp2(_dot_nt(k, q) - l_ref[0, h:h + 1, :])
            dpT = _dot_nt(v, do)
            av_s[...] += _dot(pT.astype(BF16), do)
            ak_s[...] += _dot((pT * (dpT - d_ref[0, h:h + 1, :])).astype(BF16), q)

        @pl.when(i == nq - 1)
        def _():
            dk_ref[...] = ak_s[...] * LN2
            dv_ref[...] = av_s[...].astype(BF16)

    qspec = pl.BlockSpec((G, tq, HD_B), lambda g, j, i: (g, i, 0))
    kspec = pl.BlockSpec((tk, HD_B), lambda g, j, i: (j, g))
    lspec = pl.BlockSpec((1, G, tq), lambda g, j, i: (g, 0, i))
    return pl.pallas_call(
        kern, name="attn_bwd_dkv", grid=(NKV_B, T // tk, nq),
        in_specs=[qspec, kspec, kspec, qspec, lspec, lspec],
        out_specs=[kspec, kspec],
        out_shape=[jax.ShapeDtypeStruct((T, dm.KVB), F32), jax.ShapeDtypeStruct((T, dm.KVB), BF16)],
        scratch_shapes=[pltpu.VMEM((tk, HD_B), F32), pltpu.VMEM((tk, HD_B), F32)],
        compiler_params=_cparams(("parallel", "parallel", "arbitrary")),
    )(Qr, Kr, Vb, dO, LSE_T, DEL_T)


def _qk_bwd(dm, dQr, dKr, P, rope, qn, kn):
    TB, nctx = dm.TB, dm.nctx
    qcol, kcol = dm.off["qb"] // dm.QB, dm.off["kb"] // dm.KVB

    def head_bwd(dyr, x, w_row, cos, sa, sb):
        rstd = lax.rsqrt(jnp.mean(x * x, axis=1, keepdims=True) + EPS)
        xn = x * rstd
        dy = _rope_t(dyr, cos, sa, sb)
        dw = jnp.sum(dy * xn, axis=0, keepdims=True)
        dxn = dy * w_row
        dx = rstd * (dxn - xn * jnp.mean(dxn * xn, axis=1, keepdims=True))
        return dx, dw

    def make(nh, ctx_zero):
        def kern(d_ref, x_ref, t_ref, w_ref, o_ref, gw_ref):
            i = pl.program_id(0)

            @pl.when(i == 0)
            def _():
                gw_ref[...] = jnp.zeros(gw_ref.shape, F32)

            def live():
                cos, sa, sb = t_ref[0], t_ref[1], t_ref[2]
                tot = jnp.zeros((1, HD_B), F32)
                for h in range(nh):
                    cols = slice(h * HD_B, (h + 1) * HD_B)
                    dyr = d_ref[h] if ctx_zero else d_ref[:, cols]
                    dx, dw = head_bwd(dyr, x_ref[:, cols], w_ref[0:1, :], cos, sa, sb)
                    o_ref[:, cols] = dx.astype(BF16)
                    tot = tot + dw
                gw_ref[0:1, :] += tot

            if ctx_zero:
                @pl.when(i < nctx)
                def _():
                    o_ref[...] = jnp.zeros(o_ref.shape, BF16)

                pl.when(i >= nctx)(live)
            else:
                live()
        return kern

    lat = lambda i: jnp.maximum(i - nctx, 0)
    d_qb, gqn = pl.pallas_call(
        make(NH_B, True), name="q_bwd", grid=(dm.nblk,),
        in_specs=[pl.BlockSpec((NH_B, TB, HD_B), lambda i: (0, lat(i), 0)),
                  pl.BlockSpec((TB, dm.QB), lambda i: (i, qcol)),
                  pl.BlockSpec((3, TB, HD_B), lambda i: (0, i, 0)),
                  pl.BlockSpec((SUBLANE, HD_B), lambda i: (0, 0))],
        out_specs=[pl.BlockSpec((TB, dm.QB), lambda i: (i, 0)), pl.BlockSpec((SUBLANE, HD_B), lambda i: (0, 0))],
        out_shape=[jax.ShapeDtypeStruct((dm.T, dm.QB), BF16), jax.ShapeDtypeStruct((SUBLANE, HD_B), F32)],
        compiler_params=_cparams(("arbitrary",)),
    )(dQr, P, rope, qn)
    d_kb, gkn = pl.pallas_call(
        make(NKV_B, False), name="k_bwd", grid=(dm.nblk,),
        in_specs=[pl.BlockSpec((TB, dm.KVB), lambda i: (i, 0)),
                  pl.BlockSpec((TB, dm.KVB), lambda i: (i, kcol)),
                  pl.BlockSpec((3, TB, HD_B), lambda i: (0, i, 0)),
                  pl.BlockSpec((SUBLANE, HD_B), lambda i: (0, 0))],
        out_specs=[pl.BlockSpec((TB, dm.KVB), lambda i: (i, 0)), pl.BlockSpec((SUBLANE, HD_B), lambda i: (0, 0))],
        out_shape=[jax.ShapeDtypeStruct((dm.T, dm.KVB), BF16), jax.ShapeDtypeStruct((SUBLANE, HD_B), F32)],
        compiler_params=_cparams(("arbitrary",)),
    )(dKr, P, rope, kn)
    return d_qb, d_kb, gqn, gkn


def _merge_prep(dm, Hf, Hb, P, O, mhw):
    TB, nctx = dm.TB, dm.nctx
    lat = lambda c: (lambda i: (i + nctx, c))

    def kern(hf_ref, hb_ref, oa_ref, za_ref, zb_ref, o_ref, w_ref, a_ref, b_ref, at_ref, bt_ref):
        for h in range(NH_A):
            cols = slice(h * DV_A, (h + 1) * DV_A)
            hs = hf_ref[:, cols] + hb_ref[:, cols]
            rstd = lax.rsqrt(jnp.mean(hs * hs, axis=1, keepdims=True) + EPS)
            za = za_ref[:, cols]
            a = _sigmoid(oa_ref[:, cols]) * (hs * rstd * w_ref[0:1, cols]) * (za * _sigmoid(za))
            a_ref[:, cols] = a.astype(BF16)
            at_ref[cols, :] = a.T.astype(BF16)
        for h in range(NH_B):
            cols = slice(h * HD_B, (h + 1) * HD_B)
            zb = zb_ref[:, cols]
            b = o_ref[h] * (zb * _sigmoid(zb))
            b_ref[:, cols] = b.astype(BF16)
            bt_ref[cols, :] = b.T.astype(BF16)

    return pl.pallas_call(
        kern, name="merge_prep", grid=(dm.nlat,),
        in_specs=[pl.BlockSpec((TB, dm.V), lat(0)), pl.BlockSpec((TB, dm.V), lat(0)),
                  pl.BlockSpec((TB, dm.V), lat(dm.off["oa"] // dm.V)),
                  pl.BlockSpec((TB, dm.V), lat(dm.off["za"] // dm.V)),
                  pl.BlockSpec((TB, dm.QB), lat(dm.off["zb"] // dm.QB)),
                  pl.BlockSpec((NH_B, TB, HD_B), lambda i: (0, i, 0)),
                  pl.BlockSpec((SUBLANE, dm.V), lambda i: (0, 0))],
        out_specs=[pl.BlockSpec((TB, dm.V), lambda i: (i, 0)), pl.BlockSpec((TB, dm.QB), lambda i: (i, 0)),
                   pl.BlockSpec((dm.V, TB), lambda i: (0, i)), pl.BlockSpec((dm.QB, TB), lambda i: (0, i))],
        out_shape=[jax.ShapeDtypeStruct((dm.S, dm.V), BF16), jax.ShapeDtypeStruct((dm.S, dm.QB), BF16),
                   jax.ShapeDtypeStruct((dm.V, dm.S), BF16), jax.ShapeDtypeStruct((dm.QB, dm.S), BF16)],
        compiler_params=_cparams(("parallel",)),
    )(Hf, Hb, P, P, P, O, mhw)


def _gate_merge(dm, ya, yb, P):
    TB, D, nctx = dm.TB, dm.D, dm.nctx
    gcol = dm.off["g"] // D

    def kern(ya_ref, yb_ref, ga_ref, gb_ref, o_ref, ot_ref):
        m = _sigmoid(ga_ref[...]) * ya_ref[...] + _sigmoid(gb_ref[...]) * yb_ref[...]
        o_ref[...] = m.astype(BF16)
        ot_ref[...] = m.T.astype(BF16)

    row = pl.BlockSpec((TB, D), lambda i: (i, 0))
    return pl.pallas_call(
        kern, name="gate_merge", grid=(dm.nlat,),
        in_specs=[row, row, pl.BlockSpec((TB, D), lambda i: (i + nctx, gcol)),
                  pl.BlockSpec((TB, D), lambda i: (i + nctx, gcol + 1))],
        out_specs=[row, pl.BlockSpec((D, TB), lambda i: (0, i))],
        out_shape=[jax.ShapeDtypeStruct((dm.S, D), BF16), jax.ShapeDtypeStruct((D, dm.S), BF16)],
        compiler_params=_cparams(("parallel",)),
    )(ya, yb, P, P)


def _final(dm, x, out, tgt, modv, lnp):
    TB, D = dm.TB, dm.D

    def kern(x_ref, o_ref, t_ref, m_ref, p_ref, dr_ref, do_ref, cs_ref, ls_ref):
        i = pl.program_id(0)

        @pl.when(i == 0)
        def _():
            cs_ref[...] = jnp.zeros(cs_ref.shape, F32)
            ls_ref[...] = jnp.zeros(ls_ref.shape, F32)

        gate, lnw, lnb = m_ref[4:5, :], p_ref[0:1, :], p_ref[1:2, :]
        out = o_ref[...]
        xh, rstd = _ln_stats(ALPHA * x_ref[...] + gate * out)
        e = xh * lnw + lnb - t_ref[...]
        ls_ref[...] += 0.5 * jnp.sum(jnp.sum(e * e, axis=1, keepdims=True), axis=0, keepdims=True) / D
        dy = e * (1.0 / D)
        dxh = dy * lnw
        dr = rstd * (dxh - jnp.mean(dxh, axis=1, keepdims=True)
                     - xh * jnp.mean(dxh * xh, axis=1, keepdims=True))
        cs_ref[0:1, :] += jnp.sum(dy * xh, axis=0, keepdims=True)
        cs_ref[1:2, :] += jnp.sum(dy, axis=0, keepdims=True)
        cs_ref[2:3, :] += jnp.sum(dr * out, axis=0, keepdims=True)
        dr_ref[...] = ALPHA * dr
        do_ref[...] = (dr * gate).astype(BF16)

    row = pl.BlockSpec((TB, D), lambda i: (i, 0))
    par = pl.BlockSpec((SUBLANE, D), lambda i: (0, 0))
    return pl.pallas_call(
        kern, name="final_norm_loss", grid=(dm.nlat,),
        in_specs=[row, row, row, par, par],
        out_specs=[row, row, par, pl.BlockSpec((SUBLANE, LANE), lambda i: (0, 0))],
        out_shape=[jax.ShapeDtypeStruct((dm.S, D), F32), jax.ShapeDtypeStruct((dm.S, D), BF16),
                   jax.ShapeDtypeStruct((SUBLANE, D), F32), jax.ShapeDtypeStruct((SUBLANE, LANE), F32)],
        compiler_params=_cparams(("arbitrary",)),
    )(x, out, tgt, modv, lnp)


def _merge_bwd(dm, dM, ya, yb, P):
    TB, D, nctx = dm.TB, dm.D, dm.nctx
    gcol = dm.off["g"] // D
    lat = lambda i: jnp.maximum(i - nctx, 0)

    def kern(dm_ref, ya_ref, yb_ref, ga_ref, gb_ref, da_ref, db_ref, dg_ref):
        i = pl.program_id(0)

        @pl.when(i < nctx)
        def _():
            dg_ref[...] = jnp.zeros(dg_ref.shape, BF16)

        @pl.when(i >= nctx)
        def _():
            d = dm_ref[...]
            sa, sb = _sigmoid(ga_ref[...]), _sigmoid(gb_ref[...])
            da_ref[...] = (d * sa).astype(BF16)
            db_ref[...] = (d * sb).astype(BF16)
            dg_ref[:, 0:D] = (d * ya_ref[...] * sa * (1.0 - sa)).astype(BF16)
            dg_ref[:, D:2 * D] = (d * yb_ref[...] * sb * (1.0 - sb)).astype(BF16)

    row = pl.BlockSpec((TB, D), lambda i: (lat(i), 0))
    return pl.pallas_call(
        kern, name="merge_bwd", grid=(dm.nblk,),
        in_specs=[row, row, row, pl.BlockSpec((TB, D), lambda i: (i, gcol)),
                  pl.BlockSpec((TB, D), lambda i: (i, gcol + 1))],
        out_specs=[row, row, pl.BlockSpec((TB, 2 * D), lambda i: (i, 0))],
        out_shape=[jax.ShapeDtypeStruct((dm.S, D), BF16), jax.ShapeDtypeStruct((dm.S, D), BF16),
                   jax.ShapeDtypeStruct((dm.T, 2 * D), BF16)],
        compiler_params=_cparams(("arbitrary",)),
    )(dM, ya, yb, P, P)


def _branch_bwd(dm, dA, dB, Hf, Hb, P, O, mhw):
    TB, nctx, G = dm.TB, dm.nctx, dm.G
    lat = lambda i: jnp.maximum(i - nctx, 0)

    def kern(da_ref, db_ref, hf_ref, hb_ref, oa_ref, za_ref, zb_ref, o_ref, w_ref,
             doa_ref, dza_ref, dzb_ref, dh_ref, do_ref, del_ref, gw_ref):
        i = pl.program_id(0)

        @pl.when(i == 0)
        def _():
            gw_ref[...] = jnp.zeros(gw_ref.shape, F32)

        @pl.when(i < nctx)
        def _():
            doa_ref[...] = jnp.zeros(doa_ref.shape, BF16)
            dza_ref[...] = jnp.zeros(dza_ref.shape, BF16)
            dzb_ref[...] = jnp.zeros(dzb_ref.shape, BF16)
            dh_ref[...] = jnp.zeros(dh_ref.shape, F32)

        @pl.when(i >= nctx)
        def _():
            for h in range(NH_A):
                cols = slice(h * DV_A, (h + 1) * DV_A)
                hs = hf_ref[:, cols] + hb_ref[:, cols]
                rstd = lax.rsqrt(jnp.mean(hs * hs, axis=1, keepdims=True) + EPS)
                xn = hs * rstd
                w = w_ref[0:1, cols]
                hn = xn * w
                so, za = _sigmoid(oa_ref[:, cols]), za_ref[:, cols]
                sz = _sigmoid(za)
                silu = za * sz
                da = da_ref[:, cols]
                doa_ref[:, cols] = (da * hn * silu * so * (1.0 - so)).astype(BF16)
                dza_ref[:, cols] = (da * hn * so * sz * (1.0 + za * (1.0 - sz))).astype(BF16)
                dhn = da * so * silu
                gw_ref[0:1, cols] += jnp.sum(dhn * xn, axis=0, keepdims=True)
                dxn = dhn * w
                dh_ref[:, cols] = rstd * (dxn - xn * jnp.mean(dxn * xn, axis=1, keepdims=True))
            for h in range(NH_B):
                cols = slice(h * HD_B, (h + 1) * HD_B)
                zb = zb_ref[:, cols]
                sz = _sigmoid(zb)
                db, o = db_ref[:, cols], o_ref[h]
                do = db * (zb * sz)
                do_ref[h] = do.astype(BF16)
                dzb_ref[:, cols] = (db * o * sz * (1.0 + zb * (1.0 - sz))).astype(BF16)
                del_ref[h // G, :, (h % G):(h % G) + 1] = jnp.sum(do * o, axis=1, keepdims=True)

    vlat = pl.BlockSpec((TB, dm.V), lambda i: (lat(i), 0))
    qlat = pl.BlockSpec((TB, dm.QB), lambda i: (lat(i), 0))
    hlat = pl.BlockSpec((NH_B, TB, HD_B), lambda i: (0, lat(i), 0))
    vrow = pl.BlockSpec((TB, dm.V), lambda i: (i, 0))
    qrow = pl.BlockSpec((TB, dm.QB), lambda i: (i, 0))
    pv = lambda n: pl.BlockSpec((TB, dm.V), lambda i: (i, dm.off[n] // dm.V))
    return pl.pallas_call(
        kern, name="branch_bwd", grid=(dm.nblk,),
        in_specs=[vlat, qlat, vrow, vrow, pv("oa"), pv("za"),
                  pl.BlockSpec((TB, dm.QB), lambda i: (i, dm.off["zb"] // dm.QB)), hlat,
                  pl.BlockSpec((SUBLANE, dm.V), lambda i: (0, 0))],
        out_specs=[vrow, vrow, qrow, vrow, hlat,
                   pl.BlockSpec((NKV_B, TB, G), lambda i: (0, lat(i), 0)),
                   pl.BlockSpec((SUBLANE, dm.V), lambda i: (0, 0))],
        out_shape=[jax.ShapeDtypeStruct((dm.T, dm.V), BF16), jax.ShapeDtypeStruct((dm.T, dm.V), BF16),
                   jax.ShapeDtypeStruct((dm.T, dm.QB), BF16), jax.ShapeDtypeStruct((dm.T, dm.V), F32),
                   jax.ShapeDtypeStruct((NH_B, dm.S, HD_B), BF16), jax.ShapeDtypeStruct((NKV_B, dm.S, G), F32),
                   jax.ShapeDtypeStruct((SUBLANE, dm.V), F32)],
        compiler_params=_cparams(("arbitrary",)),
    )(dA, dB, Hf, Hb, P, P, P, O, mhw)


def _conv_bwd(dm, dq, dk, P, convp):
    W = 2 * dm.QK
    col = dm.off["qk"] // W
    kscale = DK_A ** -0.5
    TB = dm.TB

    def kern1(dq_ref, dk_ref, x_ref, p_ref, n_ref, c_ref, dz_ref):
        x = x_ref[...]
        xp, xn = _shifted(dm, x, p_ref, n_ref)
        z = c_ref[3:4, :] + xp * c_ref[0:1, :] + x * c_ref[1:2, :] + xn * c_ref[2:3, :]
        sz = _sigmoid(z)
        dact = jnp.concatenate([dq_ref[...], dk_ref[...] * kscale], axis=1)
        dz_ref[...] = dact * sz * (1.0 + z * (1.0 - sz))

    half = pl.BlockSpec((TB, dm.QK), lambda i: (i, 0))
    par = pl.BlockSpec((SUBLANE, W), lambda i: (0, 0))
    dz = pl.pallas_call(
        kern1, name="conv_bwd_act", grid=(dm.nblk,),
        in_specs=[half, half] + _conv_specs(dm, W, col) + [par],
        out_specs=pl.BlockSpec((TB, W), lambda i: (i, 0)),
        out_shape=jax.ShapeDtypeStruct((dm.T, W), F32),
        compiler_params=_cparams(("parallel",)),
    )(dq, dk, P, P, P, convp)

    def kern2(z_ref, zp_ref, zn_ref, x_ref, p_ref, n_ref, c_ref, dx_ref, cs_ref):
        @pl.when(pl.program_id(0) == 0)
        def _():
            cs_ref[...] = jnp.zeros(cs_ref.shape, F32)

        dz = z_ref[...]
        dzp, dzn = _shifted(dm, dz, zp_ref, zn_ref)
        dx_ref[...] = (dzn * c_ref[0:1, :] + dz * c_ref[1:2, :] + dzp * c_ref[2:3, :]).astype(BF16)
        x = x_ref[...]
        xp, xn = _shifted(dm, x, p_ref, n_ref)
        cs_ref[0:1, :] += jnp.sum(dz * xp, axis=0, keepdims=True)
        cs_ref[1:2, :] += jnp.sum(dz * x, axis=0, keepdims=True)
        cs_ref[2:3, :] += jnp.sum(dz * xn, axis=0, keepdims=True)
        cs_ref[3:4, :] += jnp.sum(dz, axis=0, keepdims=True)

    return pl.pallas_call(
        kern2, name="conv_bwd_taps", grid=(dm.nblk,),
        in_specs=_conv_specs(dm, W, 0) + _conv_specs(dm, W, col) + [par],
        out_specs=[pl.BlockSpec((TB, W), lambda i: (i, 0)), par],
        out_shape=[jax.ShapeDtypeStruct((dm.T, W), BF16), jax.ShapeDtypeStruct((SUBLANE, W), F32)],
        compiler_params=_cparams(("arbitrary",)),
    )(dz, dz, dz, P, P, P, convp)


def _gates_bwd(dm, dG, G):
    TB = dm.TB

    def kern(d_ref, g_ref, o_ref, cs_ref):
        @pl.when(pl.program_id(0) == 0)
        def _():
            cs_ref[...] = jnp.zeros(cs_ref.shape, F32)

        lane = lax.broadcasted_iota(jnp.int32, (TB, LANE), 1)
        is_f = ((lane // NH_A) % 2) == 1
        d = d_ref[...]
        dpre = jnp.where(lane < dm.NIF, jnp.where(is_f, d * (1.0 - jnp.exp(g_ref[...])), d), 0.0)
        cs_ref[0:1, :] += jnp.sum(dpre, axis=0, keepdims=True)
        if dm.IFP > LANE:
            o_ref[:, LANE:] = jnp.zeros((TB, dm.IFP - LANE), BF16)
        o_ref[:, 0:LANE] = dpre.astype(BF16)

    row = pl.BlockSpec((TB, LANE), lambda i: (i, 0))
    return pl.pallas_call(
        kern, name="gates_bwd", grid=(dm.nblk,),
        in_specs=[row, row],
        out_specs=[pl.BlockSpec((TB, dm.IFP), lambda i: (i, 0)), pl.BlockSpec((SUBLANE, LANE), lambda i: (0, 0))],
        out_shape=[jax.ShapeDtypeStruct((dm.T, dm.IFP), BF16), jax.ShapeDtypeStruct((SUBLANE, LANE), F32)],
        compiler_params=_cparams(("arbitrary",)),
    )(dG, G)


def _ln_mod_bwd(dm, dU, ctx2, x2, modv, dr_a):
    TB, D, nctx = dm.TB, dm.D, dm.nctx
    lat = lambda i: jnp.maximum(i - nctx, 0)

    def kern(du_ref, c_ref, x_ref, m_ref, dr_ref, gx_ref, cs_ref):
        i = pl.program_id(0)
        is_ctx = i < nctx

        @pl.when(i == 0)
        def _():
            cs_ref[...] = jnp.zeros(cs_ref.shape, F32)

        xh, rstd = _ln_stats(jnp.where(is_ctx, c_ref[...], x_ref[...]))
        du = du_ref[...]
        s_shift = jnp.sum(du, axis=0, keepdims=True)
        s_scale = jnp.sum(du * xh, axis=0, keepdims=True)
        cs_ref[0:1, :] += jnp.where(is_ctx, 0.0, s_shift)
        cs_ref[1:2, :] += jnp.where(is_ctx, 0.0, s_scale)
        cs_ref[2:3, :] += jnp.where(is_ctx, s_shift, 0.0)
        cs_ref[3:4, :] += jnp.where(is_ctx, s_scale, 0.0)

        @pl.when(i >= nctx)
        def _():
            dxh = du * (1.0 + m_ref[1:2, :])
            gx_ref[...] = dr_ref[...] + rstd * (dxh - jnp.mean(dxh, axis=1, keepdims=True)
                                                - xh * jnp.mean(dxh * xh, axis=1, keepdims=True))

    row = pl.BlockSpec((TB, D), lambda i: (i, 0))
    lrow = pl.BlockSpec((TB, D), lambda i: (lat(i), 0))
    par = pl.BlockSpec((SUBLANE, D), lambda i: (0, 0))
    return pl.pallas_call(
        kern, name="ln_mod_bwd", grid=(dm.nblk,),
        in_specs=[row] + _row_specs(dm) + [par, lrow],
        out_specs=[lrow, par],
        out_shape=[jax.ShapeDtypeStruct((dm.S, D), F32), jax.ShapeDtypeStruct((SUBLANE, D), F32)],
        compiler_params=_cparams(("arbitrary",)),
    )(dU, ctx2, x2, modv, dr_a)


def _mod_fwd(craw, w_loc, b_loc):
    R, D = craw.shape
    n = w_loc.shape[1]

    def kern(c_ref, w_ref, b_ref, o_ref):
        c = c_ref[...]
        o_ref[...] = _dot((c * _sigmoid(c)).astype(BF16), w_ref[...].astype(BF16)) + b_ref[0:1, :]

    return pl.pallas_call(
        kern, name="mod_fwd", out_shape=jax.ShapeDtypeStruct((R, n), F32),
        compiler_params=pltpu.CompilerParams(vmem_limit_bytes=VMEM_LIMIT),
    )(craw, w_loc, b_loc)


def _mod_bwd(crawT, dmod, w_loc):
    D, R = crawT.shape
    n = w_loc.shape[1]

    def kern(c_ref, d_ref, w_ref, gw_ref, dc_ref):
        c = c_ref[...]
        d = d_ref[...].astype(BF16)
        gw_ref[...] = _dot((c * _sigmoid(c)).astype(BF16), d)
        dc_ref[...] = _dot_nt(d, w_ref[...].astype(BF16))

    return pl.pallas_call(
        kern, name="mod_bwd",
        out_shape=[jax.ShapeDtypeStruct((D, n), F32), jax.ShapeDtypeStruct((R, D), F32)],
        compiler_params=pltpu.CompilerParams(vmem_limit_bytes=VMEM_LIMIT),
    )(crawT, dmod, w_loc)


def _cctx_grad(dsilu, c_ctx):
    def kern(p_ref, c_ref, o_ref):
        c = c_ref[...]
        sc = _sigmoid(c)
        o_ref[...] = p_ref[...] * (sc * (1.0 + c * (1.0 - sc)))

    return pl.pallas_call(kern, name="cctx_grad", out_shape=jax.ShapeDtypeStruct(c_ctx.shape, F32))(dsilu, c_ctx)


def _adamw(w, g, m, v, name):
    lead, (R, C) = w.shape[:-2], w.shape[-2:]
    assert all(d == 1 for d in lead)
    tb = _tile(R, max(SUBLANE, (1 << 19) // (4 * C) // SUBLANE * SUBLANE), SUBLANE)
    c1 = 1.0 / (1.0 - ADAM_B1 ** ADAM_STEP)
    c2 = 1.0 / (1.0 - ADAM_B2 ** ADAM_STEP)

    def kern(w_ref, g_ref, m_ref, v_ref, d_ref, nm_ref, nv_ref):
        g = g_ref[...]
        nm = ADAM_B1 * m_ref[...] + (1.0 - ADAM_B1) * g
        nv = ADAM_B2 * v_ref[...] + (1.0 - ADAM_B2) * (g * g)
        nm_ref[...] = nm
        nv_ref[...] = nv
        d_ref[...] = -ADAM_LR * ((nm * c1) / (jnp.sqrt(nv * c2) + ADAM_EPS) + ADAM_WD * w_ref[...])

    spec = pl.BlockSpec(lead + (tb, C), lambda i: (0,) * len(lead) + (i, 0))
    return pl.pallas_call(
        kern, name=name, grid=(R // tb,), in_specs=[spec] * 4, out_specs=[spec] * 3,
        out_shape=[jax.ShapeDtypeStruct(w.shape, F32)] * 3,
        compiler_params=_cparams(("parallel",)),
    )(w, g, m, v)


def _rope_tables(dm):
    half = HD_B // 2
    rows_n = dm.S // GRID_W
    row = jnp.repeat(jnp.arange(rows_n), GRID_W).astype(F32)
    col = jnp.tile(jnp.arange(GRID_W), rows_n).astype(F32)
    inv = ROPE_THETA ** (-jnp.arange(0, half, 2, dtype=F32) / half)
    ar, ac = row[:, None] * inv[None], col[:, None] * inv[None]
    cos = jnp.concatenate([jnp.cos(ar), jnp.cos(ar), jnp.cos(ac), jnp.cos(ac)], axis=1)
    zr = jnp.zeros_like(ar)
    sa = jnp.concatenate([-jnp.sin(ar), zr, -jnp.sin(ac), zr], axis=1)
    sb = jnp.concatenate([zr, jnp.sin(ar), zr, jnp.sin(ac)], axis=1)
    ctx = jnp.stack([jnp.ones((dm.Tc, HD_B), F32), jnp.zeros((dm.Tc, HD_B), F32), jnp.zeros((dm.Tc, HD_B), F32)])
    return jnp.concatenate([ctx, jnp.stack([cos, sa, sb])], axis=1)


def _rows8(*rows, width):
    out = [jnp.pad(r.reshape(-1).astype(F32), (0, width - r.size)) for r in rows]
    n = -(-len(out) // SUBLANE) * SUBLANE
    out += [jnp.zeros((width,), F32)] * (n - len(out))
    return jnp.stack(out)


def _ref_starts(dm):
    starts, o = {}, 0
    for n, wd in zip(dm.ref_names, dm.ref_widths):
        starts[n] = o
        o += wd
    return starts


def _to_padded(dm, shards):
    n = shards.shape[2]
    starts, wref = _ref_starts(dm), dict(zip(dm.ref_names, dm.ref_widths))
    pieces = []
    for name in dm.order:
        lo, hi = starts[name], starts[name] + wref[name]
        for j in range(N_DEV):
            a, b = max(lo, j * n), min(hi, (j + 1) * n)
            if a < b:
                pieces.append(shards[j][:, a - j * n:b - j * n])
        if dm.w[name] > wref[name]:
            pieces.append(jnp.zeros((shards.shape[1], dm.w[name] - wref[name]), shards.dtype))
    return jnp.concatenate(pieces, axis=1)


def _from_padded(dm, w_pad, n):
    starts = _ref_starts(dm)
    slabs = []
    for j in range(N_DEV):
        pieces = []
        for name, wd in zip(dm.ref_names, dm.ref_widths):
            a, b = max(starts[name], j * n), min(starts[name] + wd, (j + 1) * n)
            if a < b:
                o = dm.off[name] - starts[name]
                pieces.append(w_pad[:, a + o:b + o])
        slabs.append(jnp.concatenate(pieces, axis=1))
    return jnp.stack(slabs)


def kernel(x, c, ctx, c_ctx, w_mod, b_mod, w_in, b_if, conv_w, conv_b, mh_norm_w, q_norm_w, k_norm_w, w_branch_a, w_branch_b, w_out, ln_w, ln_b, loss_target, m_c_ctx, m_w_mod, m_b_mod, m_w_in, m_b_if, m_conv_w, m_conv_b, m_mh_norm_w, m_q_norm_w, m_k_norm_w, m_w_branch_a, m_w_branch_b, m_w_out, m_ln_w, m_ln_b, v_c_ctx, v_w_mod, v_b_mod, v_w_in, v_b_if, v_conv_w, v_conv_b, v_mh_norm_w, v_q_norm_w, v_k_norm_w, v_w_branch_a, v_w_branch_b, v_w_out, v_ln_w, v_ln_b):
    S, D = x.shape[1], x.shape[2]
    Tc = ctx.shape[1]
    dm = Dims(S, Tc, D)
    T, QK2 = dm.T, 2 * dm.QK
    me = 4 * lax.axis_index("x") + 2 * lax.axis_index("y") + lax.axis_index("c")
    n_mod = w_mod.shape[2]
    n_in = w_in.shape[2]
    n_cv = conv_w.shape[2]
    rb = w_out.shape[1]

    pack0 = _all_gather(_rows8(c[0], conv_w[0, 0], conv_w[0, 1], conv_w[0, 2], width=D), "ag_cond", False)
    c_all = pack0[:, 0, :]
    conv_full = jnp.transpose(pack0[:, 1:4, :n_cv], (1, 0, 2)).reshape(CONV_W, QK2)
    convp = _rows8(conv_full[0], conv_full[1], conv_full[2], conv_b[0], width=QK2)

    w_in_all = _all_gather(w_in[0].astype(BF16), "ag_w_in", True)
    Wp = _to_padded(dm, w_in_all)
    wsq = jnp.concatenate([w_branch_a[0], w_branch_b[0], w_out[0]], axis=0).astype(BF16)

    craw = _rows8(*[c_all[j] for j in range(N_DEV)], c_ctx, width=D)
    b_loc = _rows8(lax.dynamic_slice(b_mod[0], (me * n_mod,), (n_mod,)), width=n_mod)
    mod_all = _all_gather(_mod_fwd(craw, w_mod[0], b_loc), "ag_mod", False)
    mod_rows = jnp.transpose(mod_all, (1, 0, 2)).reshape(2 * SUBLANE, 3 * D)
    mod_me = lax.dynamic_slice(mod_rows, (me, 0), (1, 3 * D))[0]
    mod_cx = mod_rows[N_DEV]
    modv = _rows8(mod_me[0:D], mod_me[D:2 * D], mod_cx[0:D], mod_cx[D:2 * D], mod_me[2 * D:3 * D], width=D)

    U, UT = _ln_mod_fwd(dm, ctx[0], x[0], modv)
    P, wsq_all = _mm(U, Wp, "mm_in_proj", tn=896, exchange=("ag", wsq))
    Wba = wsq_all[:, 0:rb, :].reshape(dm.V, D)
    Wbb = wsq_all[:, rb:2 * rb, :].reshape(dm.QB, D)
    Wout = wsq_all[:, 2 * rb:3 * rb, :].reshape(D, D)
    qk_act = _conv_silu_fwd(dm, P, convp)
    G = _gates_fwd(dm, P, _rows8(b_if[0], width=LANE))
    Hf, Cf, NMf = _mlstm_fwd(dm, qk_act, P, G, False)
    Hb, Cb, NMb = _mlstm_fwd(dm, qk_act, P, G, True)
    rope = _rope_tables(dm)
    qn, kn = _rows8(q_norm_w[0], width=HD_B), _rows8(k_norm_w[0], width=HD_B)
    Qr, Kr, Vb = _qk_prep(dm, P, rope, qn, kn)
    O, LSE = _attn_fwd(dm, Qr, Kr, Vb)
    mhw = _rows8(mh_norm_w[0], width=dm.V)
    A_in, B_in, A_inT, B_inT = _merge_prep(dm, Hf, Hb, P, O, mhw)
    ya = _mm(A_in, Wba, "mm_branch_a")
    yb = _mm(B_in, Wbb, "mm_branch_b")
    M_in, M_inT = _gate_merge(dm, ya, yb, P)
    out = _mm(M_in, Wout, "mm_out")
    lnp = _rows8(ln_w[0], ln_b[0], width=D)
    dr_a, d_out, cs_fin, loss_p = _final(dm, x[0], out, loss_target[0], modv, lnp)
    loss = lax.psum(loss_p[0, 0], ("x", "y", "c"))

    dM = _mm(d_out, Wout, "mm_d_merge", nt=True)
    gWout = _mm(M_inT, d_out, "mm_g_w_out", tk=2048, out_dtype=BF16)
    d_ya, d_yb, d_g = _merge_bwd(dm, dM, ya, yb, P)
    dA = _mm(d_ya, Wba, "mm_d_a", nt=True)
    gWba = _mm(A_inT, d_ya, "mm_g_w_ba", tk=2048, out_dtype=BF16)
    dB = _mm(d_yb, Wbb, "mm_d_b", nt=True)
    gWbb = _mm(B_inT, d_yb, "mm_g_w_bb", tk=2048, out_dtype=BF16)
    d_oa, d_za, d_zb, dH, dO, DEL, gmh = _branch_bwd(dm, dA, dB, Hf, Hb, P, O, mhw)
    dQr = _attn_bwd_dq(dm, Qr, Kr, Vb, dO, LSE, DEL)
    dKr, d_vb = _attn_bwd_dkv(dm, Qr, Kr, Vb, dO, jnp.transpose(LSE, (0, 2, 1)), jnp.transpose(DEL, (0, 2, 1)))
    d_qb, d_kb, gqn, gkn = _qk_bwd(dm, dQr, dKr, P, rope, qn, kn)
    acc = _mlstm_bwd(dm, qk_act, P, G, Cf, NMf, Hf, dH, None, False)
    dq, dk, dv, dG = _mlstm_bwd(dm, qk_act, P, G, Cb, NMb, Hb, dH, acc, True)
    d_qk, cs_conv = _conv_bwd(dm, dq, dk, P, convp)
    d_if, gbif = _gates_bwd(dm, dG, G)
    parts = {"g": d_g, "qk": d_qk, "va": dv.astype(BF16), "oa": d_oa, "za": d_za, "qb": d_qb,
             "zb": d_zb, "kb": d_kb, "vb": d_vb, "if": d_if}
    dP = jnp.concatenate([parts[n] for n in dm.order], axis=1)
    gsq = jnp.concatenate([gWba.reshape(N_DEV, rb, D), gWbb.reshape(N_DEV, rb, D),
                           gWout.reshape(N_DEV, rb, D)], axis=1)
    gWp, gsq_all = _mm(UT, dP, "mm_g_w_in", tm=1024, tn=896, tk=2816, out_dtype=BF16,
                       exchange=("a2a", gsq))
    gW = _from_padded(dm, gWp, n_in)
    dU, gW_all = _mm(dP, Wp, "mm_d_u", nt=True, tk=1792, exchange=("a2a", gW))
    grad_x, cs_ln = _ln_mod_bwd(dm, dU, ctx[0], x[0], modv, dr_a)

    dmod_me = _rows8(jnp.concatenate([cs_ln[0], cs_ln[1], cs_fin[2]]),
                     jnp.concatenate([cs_ln[2], cs_ln[3], jnp.zeros((D,), F32)]), width=3 * D)
    dmod_all = _all_gather(dmod_me, "ag_dmod", False)
    dmod_loc = lax.dynamic_slice(dmod_all, (0, 0, me * n_mod), (N_DEV, 2, n_mod))
    dmod_rows = _rows8(*[dmod_loc[j, 0] for j in range(N_DEV)], jnp.sum(dmod_loc[:, 1, :], axis=0), width=n_mod)
    g_w_mod, dc_part = _mod_bwd(craw.T, dmod_rows, w_mod[0])

    PW = dm.PW
    small = _rows8(cs_fin[0], cs_fin[1], gmh[0], cs_conv[3], cs_conv[0], cs_conv[1], cs_conv[2],
                   dmod_me[0, 0:D], dmod_me[0, D:2 * D], dmod_me[0, 2 * D:3 * D],
                   dmod_me[1, 0:D], dmod_me[1, D:2 * D],
                   jnp.concatenate([gqn[0], gkn[0], gbif[0]]), dc_part[N_DEV], width=PW)
    tot = _sum_slots(_all_gather(small, "ag_small", False), "sum_small")
    g_ln_w, g_ln_b, g_mh, g_conv_b = tot[0, :D], tot[1, :D], tot[2, :dm.V], tot[3, :QK2]
    g_conv_full = tot[4:7, :QK2]
    g_b_mod = jnp.concatenate([tot[7, :D] + tot[10, :D], tot[8, :D] + tot[11, :D], tot[9, :D]])
    g_qn, g_kn, g_bif = tot[12, 0:HD_B], tot[12, HD_B:2 * HD_B], tot[12, 2 * HD_B:2 * HD_B + dm.NIF]
    g_c_ctx = _cctx_grad(tot[13:14, :D], c_ctx.reshape(1, D))[0]
    g_conv_w = lax.dynamic_slice(g_conv_full, (0, me * n_cv), (CONV_W, n_cv))

    g_w_in = _sum_slots(gW_all, "sum_g_w_in")
    g_sq = _sum_slots(gsq_all, "sum_g_w_sq")

    upd_in = [a[None] for a in _adamw(w_in[0], g_w_in, m_w_in[0], v_w_in[0], "adam_w_in")]
    g_w_in, g_w_mod = g_w_in[None], g_w_mod[None]
    g_ba, g_bb, g_out = g_sq[None, 0:rb], g_sq[None, rb:2 * rb], g_sq[None, 2 * rb:3 * rb]
    upd_md = _adamw(w_mod, g_w_mod, m_w_mod, v_w_mod, "adam_w_mod")
    upd_ba = _adamw(w_branch_a, g_ba, m_w_branch_a, v_w_branch_a, "adam_w_ba")
    upd_bb = _adamw(w_branch_b, g_bb, m_w_branch_b, v_w_branch_b, "adam_w_bb")
    upd_out = _adamw(w_out, g_out, m_w_out, v_w_out, "adam_w_out")
    names = ["c_ctx", "b_mod", "b_if", "conv_w", "conv_b", "mh", "qn", "kn", "ln_w", "ln_b"]
    ws = [c_ctx, b_mod, b_if, conv_w, conv_b, mh_norm_w, q_norm_w, k_norm_w, ln_w, ln_b]
    ms = [m_c_ctx, m_b_mod, m_b_if, m_conv_w, m_conv_b, m_mh_norm_w, m_q_norm_w, m_k_norm_w, m_ln_w, m_ln_b]
    vs = [v_c_ctx, v_b_mod, v_b_if, v_conv_w, v_conv_b, v_mh_norm_w, v_q_norm_w, v_k_norm_w, v_ln_w, v_ln_b]
    gs = [g_c_ctx, g_b_mod, g_bif, g_conv_w, g_conv_b, g_mh, g_qn, g_kn, g_ln_w, g_ln_b]
    sizes = [a.size for a in ws]
    tot_n = sum(sizes)
    padn = -(-tot_n // LANE) * LANE
    flat = lambda arrs: jnp.pad(jnp.concatenate([a.reshape(-1) for a in arrs]), (0, padn - tot_n)).reshape(1, padn)
    d_s, nm_s, nv_s = _adamw(flat(ws), flat(gs), flat(ms), flat(vs), "adam_small")

    def split(a):
        res, o = {}, 0
        for n, wv, sz in zip(names, ws, sizes):
            res[n] = a[0, o:o + sz].reshape(wv.shape)
            o += sz
        return res

    def assemble(s, big_in, big_md, big_ba, big_bb, big_out):
        return [s["c_ctx"], big_md, s["b_mod"], big_in, s["b_if"], s["conv_w"], s["conv_b"],
                s["mh"], s["qn"], s["kn"], big_ba, big_bb, big_out, s["ln_w"], s["ln_b"]]

    g_small = {n: g.reshape(wv.shape) for n, g, wv in zip(names, gs, ws)}
    grads = assemble(g_small, g_w_in, g_w_mod, g_ba, g_bb, g_out)
    deltas, new_m, new_v = [
        assemble(split(sm), upd_in[t], upd_md[t], upd_ba[t], upd_bb[t], upd_out[t])
        for t, sm in enumerate((d_s, nm_s, nv_s))]
    return (loss, grad_x[None], *grads, *deltas, *new_m, *new_v)
```

```python
import jax
import jax.numpy as jnp
from jax import lax
from jax.experimental import pallas as pl
from jax.experimental.pallas import tpu as pltpu

F32 = jnp.float32
BF16 = jnp.bfloat16
MESH = pl.DeviceIdType.MESH
N_DEV = 8

GRID_W = 64
NH_A = 8
DK_A = 128
DV_A = 256
CONV_W = 3
CHUNK = 64
M_INIT = -1e30
NH_B = 16
NKV_B = 4
HD_B = 128
ROPE_THETA = 10000.0
EPS = 1e-6
DEPTH = 1
ALPHA = (2 * DEPTH) ** 0.25
ADAM_LR = 0.001
ADAM_B1 = 0.9
ADAM_B2 = 0.999
ADAM_EPS = 1e-08
ADAM_WD = 0.01
ADAM_STEP = 10

LANE = 128
SUBLANE = 8
VMEM_LIMIT = 56 << 20


def _tile(n, target, align):
    best = None
    t = align
    while t <= min(n, target):
        if n % t == 0:
            best = t
        t += align
    return best if best is not None else n


class Dims:
    def __init__(self, S, Tc, D):
        self.S, self.Tc, self.D = S, Tc, D
        self.T = S + Tc
        self.QK = NH_A * DK_A
        self.V = NH_A * DV_A
        self.QB = NH_B * HD_B
        self.KVB = NKV_B * HD_B
        self.G = NH_B // NKV_B
        self.NIF = 4 * NH_A
        self.IFP = 512 if self.KVB % 512 == 0 else LANE
        self.ref_widths = [2 * self.QK, self.V, self.NIF, self.KVB, self.KVB,
                           self.V, self.V, self.QB, self.QB, 2 * D]
        self.ref_names = ["qk", "va", "if", "kb", "vb", "oa", "za", "qb", "zb", "g"]
        self.N_IN = sum(self.ref_widths)
        self.order = ["g", "qk", "va", "oa", "za", "qb", "zb", "kb", "vb", "if"]
        w = dict(zip(self.ref_names, self.ref_widths))
        w["if"] = self.IFP
        self.w = w
        self.off = {}
        o = 0
        for n in self.order:
            assert o % w[n] == 0, (n, o, w[n])
            self.off[n] = o
            o += w[n]
        self.NP = o
        self.TB = min(256, Tc)
        assert Tc % self.TB == 0 and S % self.TB == 0 and self.TB % CHUNK == 0
        self.nctx = Tc // self.TB
        self.nlat = S // self.TB
        self.nblk = self.nctx + self.nlat
        self.PW = max(D, self.V, 2 * self.QK, 3 * LANE)


def _cparams(sem):
    return pltpu.CompilerParams(dimension_semantics=sem, vmem_limit_bytes=VMEM_LIMIT)


def _sigmoid(x):
    return 1.0 / (1.0 + jnp.exp(-x))


def _my_pos():
    return lax.axis_index("x"), lax.axis_index("y"), lax.axis_index("c")


def _all_gather(x, name, big):
    R, C = x.shape
    space = pl.ANY if big else pltpu.VMEM

    def body(x_ref, out_ref, send_sems, recv_sems, local_sem):
        px, py, pc = _my_pos()
        me, sibling = (px, py, pc), (px, py, 1 - pc)
        chips = [(1 - px, py), (px, 1 - py), (1 - px, 1 - py)]

        def slot(bx, by, bc):
            return out_ref.at[4 * bx + 2 * by + bc]

        def copy(k, block, to, src=None):
            return pltpu.make_async_remote_copy(
                src_ref=slot(*block) if src is None else src, dst_ref=slot(*block),
                send_sem=send_sems.at[k], recv_sem=recv_sems.at[k],
                device_id=to, device_id_type=MESH)

        mine = pltpu.make_async_copy(x_ref, slot(*me), local_sem)
        mine.start()
        first = [copy(0, me, sibling, src=x_ref)]
        first += [copy(1 + j, me, (*chip, pc), src=x_ref) for j, chip in enumerate(chips)]
        for cp in first:
            cp.start()
        passed = [copy(4 + j, (*chip, pc), sibling) for j, chip in enumerate(chips)]
        for j, chip in enumerate(chips):
            copy(1 + j, (*chip, pc), me).wait_recv()
            passed[j].start()
        copy(0, sibling, me).wait_recv()
        for j, chip in enumerate(chips):
            copy(4 + j, (*chip, 1 - pc), me).wait_recv()
        for cp in first + passed:
            cp.wait_send()
        mine.wait()

    return pl.pallas_call(
        body, name=name,
        out_shape=jax.ShapeDtypeStruct((N_DEV, R, C), x.dtype),
        in_specs=[pl.BlockSpec(memory_space=space)],
        out_specs=pl.BlockSpec(memory_space=space),
        scratch_shapes=[pltpu.SemaphoreType.DMA((7,)), pltpu.SemaphoreType.DMA((7,)),
                        pltpu.SemaphoreType.DMA],
    )(x)


EXCHANGE_SEMS = [pltpu.SemaphoreType.DMA((N_DEV - 1,)), pltpu.SemaphoreType.DMA((N_DEV - 1,)),
                 pltpu.SemaphoreType.DMA]


def _exchange(kind, src_ref, land_ref, send_sems, recv_sems, local_sem):
    def copies():
        px, py, pc = _my_pos()
        me = 4 * px + 2 * py + pc
        own = src_ref if kind == "ag" else src_ref.at[me]
        local = pltpu.make_async_copy(own, land_ref.at[me], local_sem)
        sends, recvs = [], []
        for r in range(1, N_DEV):
            dx, dy, dc = (r >> 2) & 1, (r >> 1) & 1, r & 1
            qx = px if dx == 0 else 1 - px
            qy = py if dy == 0 else 1 - py
            qc = pc if dc == 0 else 1 - pc
            peer = 4 * qx + 2 * qy + qc
            sems = dict(send_sem=send_sems.at[r - 1], recv_sem=recv_sems.at[r - 1],
                        device_id=(qx, qy, qc), device_id_type=MESH)
            sends.append(pltpu.make_async_remote_copy(
                src_ref=src_ref if kind == "ag" else src_ref.at[peer], dst_ref=land_ref.at[me], **sems))
            recvs.append(pltpu.make_async_remote_copy(src_ref=own, dst_ref=land_ref.at[peer], **sems))
        return local, sends, recvs

    def start():
        local, sends, _ = copies()
        local.start()
        for cp in sends:
            cp.start()

    def wait():
        local, sends, recvs = copies()
        for cp in recvs:
            cp.wait_recv()
        for cp in sends:
            cp.wait_send()
        local.wait()

    return start, wait


def _land_shape(kind, src):
    return jax.ShapeDtypeStruct(src.shape if kind == "a2a" else (N_DEV,) + src.shape, src.dtype)


def _sum_slots(a, name):
    _, R, C = a.shape
    tb = _tile(R, max(SUBLANE, (1 << 20) // (4 * C) // SUBLANE * SUBLANE), SUBLANE)

    def kern(a_ref, o_ref):
        acc = a_ref[0].astype(F32)
        for j in range(1, N_DEV):
            acc = acc + a_ref[j].astype(F32)
        o_ref[...] = acc

    return pl.pallas_call(
        kern, name=name, grid=(R // tb,),
        in_specs=[pl.BlockSpec((N_DEV, tb, C), lambda i: (0, i, 0))],
        out_specs=pl.BlockSpec((tb, C), lambda i: (i, 0)),
        out_shape=jax.ShapeDtypeStruct((R, C), F32),
        compiler_params=_cparams(("parallel",)),
    )(a)


def _mm(a, b, name, nt=False, tm=768, tn=1024, tk=2048, out_dtype=F32, exchange=None):
    M, K = a.shape
    N = b.shape[0] if nt else b.shape[1]
    assert (b.shape[1] if nt else b.shape[0]) == K
    tm, tn, tk = _tile(M, tm, 16), _tile(N, tn, LANE), _tile(K, tk, LANE)
    ni, nj, nk = M // tm, N // tn, K // tk

    def dot(x, y):
        if nt:
            return lax.dot_general(x, y, (((1,), (1,)), ((), ())), preferred_element_type=F32)
        return jnp.dot(x, y, preferred_element_type=F32)

    def kern(a_ref, b_ref, *rest):
        if exchange is not None:
            src_ref, o_ref, land_ref, acc_ref, send_sems, recv_sems, local_sem = rest
            start, wait = _exchange(exchange[0], src_ref, land_ref, send_sems, recv_sems, local_sem)
            i, j, kk = pl.program_id(0), pl.program_id(1), pl.program_id(2)
            pl.when((i == 0) & (j == 0) & (kk == 0))(start)
        else:
            o_ref, acc_ref = rest
        k = pl.program_id(2)
        part = dot(a_ref[...], b_ref[...])
        if nk == 1:
            o_ref[...] = part.astype(o_ref.dtype)
        else:
            @pl.when(k == 0)
            def _():
                acc_ref[...] = part

            @pl.when(k > 0)
            def _():
                acc_ref[...] += part

            @pl.when(k == nk - 1)
            def _():
                o_ref[...] = acc_ref[...].astype(o_ref.dtype)

        if exchange is not None:
            pl.when((i == ni - 1) & (j == nj - 1) & (kk == nk - 1))(wait)

    b_spec = (pl.BlockSpec((tn, tk), lambda i, j, k: (j, k)) if nt
              else pl.BlockSpec((tk, tn), lambda i, j, k: (k, j)))
    in_specs = [pl.BlockSpec((tm, tk), lambda i, j, k: (i, k)), b_spec]
    out_specs = pl.BlockSpec((tm, tn), lambda i, j, k: (i, j))
    out_shape = jax.ShapeDtypeStruct((M, N), out_dtype)
    scratch = [pltpu.VMEM((tm, tn) if nk > 1 else (SUBLANE, LANE), F32)]
    args = (a, b)
    sem = ("parallel", "parallel", "arbitrary")
    if exchange is not None:
        in_specs = in_specs + [pl.BlockSpec(memory_space=pl.ANY)]
        out_specs = [out_specs, pl.BlockSpec(memory_space=pl.ANY)]
        out_shape = [out_shape, _land_shape(*exchange)]
        scratch = scratch + EXCHANGE_SEMS
        args = (a, b, exchange[1])
        sem = ("arbitrary", "arbitrary", "arbitrary")
    return pl.pallas_call(
        kern, name=name, grid=(ni, nj, nk), in_specs=in_specs, out_specs=out_specs,
        out_shape=out_shape, scratch_shapes=scratch, compiler_params=_cparams(sem),
    )(*args)


def _ln_stats(x):
    mu = jnp.mean(x, axis=-1, keepdims=True)
    xc = x - mu
    var = jnp.mean(xc * xc, axis=-1, keepdims=True)
    rstd = lax.rsqrt(var + EPS)
    return xc * rstd, rstd


def _row_specs(dm):
    TB, D, nctx = dm.TB, dm.D, dm.nctx
    return [pl.BlockSpec((TB, D), lambda i: (jnp.minimum(i, nctx - 1), 0)),
            pl.BlockSpec((TB, D), lambda i: (jnp.maximum(i - nctx, 0), 0))]


def _ln_mod_fwd(dm, ctx2, x2, modv):
    TB, D, nctx = dm.TB, dm.D, dm.nctx

    def kern(c_ref, x_ref, m_ref, u_ref, ut_ref):
        is_ctx = pl.program_id(0) < nctx
        xh, _ = _ln_stats(jnp.where(is_ctx, c_ref[...], x_ref[...]))
        shift = jnp.where(is_ctx, m_ref[2:3, :], m_ref[0:1, :])
        scale = jnp.where(is_ctx, m_ref[3:4, :], m_ref[1:2, :])
        u = xh * (1.0 + scale) + shift
        u_ref[...] = u.astype(BF16)
        ut_ref[...] = u.T.astype(BF16)

    return pl.pallas_call(
        kern, name="ln_mod_fwd", grid=(dm.nblk,),
        in_specs=_row_specs(dm) + [pl.BlockSpec((SUBLANE, D), lambda i: (0, 0))],
        out_specs=[pl.BlockSpec((TB, D), lambda i: (i, 0)), pl.BlockSpec((D, TB), lambda i: (0, i))],
        out_shape=[jax.ShapeDtypeStruct((dm.T, D), BF16), jax.ShapeDtypeStruct((D, dm.T), BF16)],
        compiler_params=_cparams(("parallel",)),
    )(ctx2, x2, modv)


def _conv_specs(dm, W, col):
    TB, T = dm.TB, dm.T
    r8 = TB // SUBLANE
    last8 = T // SUBLANE - 1
    return [pl.BlockSpec((TB, W), lambda i: (i, col)),
            pl.BlockSpec((SUBLANE, W), lambda i: (jnp.maximum(i * r8 - 1, 0), col)),
            pl.BlockSpec((SUBLANE, W), lambda i: (jnp.minimum((i + 1) * r8, last8), col))]


def _shifted(dm, x, prev_ref, next_ref):
    TB, nctx, nblk = dm.TB, dm.nctx, dm.nblk
    i = pl.program_id(0)
    row = lax.broadcasted_iota(jnp.int32, x.shape, 0)
    zero_prev = (i == 0) | (i == nctx)
    zero_next = (i == nctx - 1) | (i == nblk - 1)
    before = jnp.where(zero_prev, 0.0, prev_ref[SUBLANE - 1:SUBLANE, :])
    after = jnp.where(zero_next, 0.0, next_ref[0:1, :])
    xp = jnp.where(row == 0, before, pltpu.roll(x, 1, 0))
    xn = jnp.where(row == TB - 1, after, pltpu.roll(x, TB - 1, 0))
    return xp, xn


def _conv_silu_fwd(dm, P, convp):
    W = 2 * dm.QK
    col = dm.off["qk"] // W
    kscale = DK_A ** -0.5

    def kern(x_ref, p_ref, n_ref, c_ref, o_ref):
        x = x_ref[...]
        xp, xn = _shifted(dm, x, p_ref, n_ref)
        z = c_ref[3:4, :] + xp * c_ref[0:1, :] + x * c_ref[1:2, :] + xn * c_ref[2:3, :]
        lane = lax.broadcasted_iota(jnp.int32, (1, W), 1)
        cs = jnp.where(lane >= dm.QK, kscale, 1.0)
        o_ref[...] = z * _sigmoid(z) * cs

    return pl.pallas_call(
        kern, name="conv_silu_fwd", grid=(dm.nblk,),
        in_specs=_conv_specs(dm, W, col) + [pl.BlockSpec((SUBLANE, W), lambda i: (0, 0))],
        out_specs=pl.BlockSpec((dm.TB, W), lambda i: (i, 0)),
        out_shape=jax.ShapeDtypeStruct((dm.T, W), F32),
        compiler_params=_cparams(("parallel",)),
    )(P, P, P, convp)


def _gates_fwd(dm, P, bif):
    col = dm.off["if"] // LANE

    def kern(x_ref, b_ref, o_ref):
        x = x_ref[...] + b_ref[0:1, :]
        lane = lax.broadcasted_iota(jnp.int32, x.shape, 1)
        is_f = ((lane // NH_A) % 2) == 1
        ls = jnp.minimum(x, 0.0) - jnp.log(1.0 + jnp.exp(-jnp.abs(x)))
        o_ref[...] = jnp.where(lane < dm.NIF, jnp.where(is_f, ls, x), 0.0)

    return pl.pallas_call(
        kern, name="gates_fwd", grid=(dm.nblk,),
        in_specs=[pl.BlockSpec((dm.TB, LANE), lambda i: (i, col)),
                  pl.BlockSpec((SUBLANE, LANE), lambda i: (0, 0))],
        out_specs=pl.BlockSpec((dm.TB, LANE), lambda i: (i, 0)),
        out_shape=jax.ShapeDtypeStruct((dm.T, LANE), F32),
        compiler_params=_cparams(("parallel",)),
    )(P, bif)


def _mlstm_order(dm, reverse, backward):
    nctx, nblk = dm.nctx, dm.nblk

    def idx(i):
        if backward:
            i = nblk - 1 - i
        if not reverse:
            return i
        return jnp.where(i < nctx, nctx - 1 - i, nblk - 1 - (i - nctx))

    return idx


def _chunk_gates(g, ci, cf, mask_f, maskT_f, eye_f):
    lane = lax.broadcasted_iota(jnp.int32, g.shape, 1)
    gi_c = jnp.sum(jnp.where(lane == ci, g, 0.0), axis=1, keepdims=True)
    gf_c = jnp.sum(jnp.where(lane == cf, g, 0.0), axis=1, keepdims=True)
    gi_r = jnp.sum(eye_f * gi_c, axis=0, keepdims=True)
    gf_r = jnp.sum(eye_f * gf_c, axis=0, keepdims=True)
    b_c = jnp.sum(mask_f * gf_r, axis=1, keepdims=True)
    b_r = jnp.sum(maskT_f * gf_c, axis=0, keepdims=True)
    return gi_c, gi_r, b_c, b_r


def _chunk_masks(reverse):
    L = CHUNK
    r = lax.broadcasted_iota(jnp.int32, (L, L), 0)
    c = lax.broadcasted_iota(jnp.int32, (L, L), 1)
    mask = (c >= r) if reverse else (c <= r)
    maskT = (r >= c) if reverse else (r <= c)
    return mask, mask.astype(F32), maskT.astype(F32), (r == c).astype(F32)


def _pick_row(x, e):
    r = lax.broadcasted_iota(jnp.int32, x.shape, 0)
    return jnp.sum(jnp.where(r == e, x, 0.0), axis=0, keepdims=True)


def _dot_nt(a, b):
    return lax.dot_general(a, b, (((1,), (1,)), ((), ())), preferred_element_type=F32)


def _dot(a, b):
    return jnp.dot(a, b, preferred_element_type=F32)


def _chunk_fwd_core(q, k, g, ci, cf, C0, n0, m0, masks, reverse):
    mask, mask_f, maskT_f, eye_f = masks
    gi_c, gi_r, b_c, b_r = _chunk_gates(g, ci, cf, mask_f, maskT_f, eye_f)
    d = jnp.where(mask, b_c - b_r + gi_r, -jnp.inf)
    m_c = jnp.maximum(b_c + m0, jnp.max(d, axis=1, keepdims=True))
    w = jnp.exp(d - m_c)
    a_c = jnp.exp(b_c + m0 - m_c)
    qb, kb = q.astype(BF16), k.astype(BF16)
    s = _dot_nt(qb, kb) * w
    den = a_c * jnp.sum(q * n0, axis=1, keepdims=True) + jnp.sum(s, axis=1, keepdims=True)
    e = 0 if reverse else CHUNK - 1
    m_end, b_end, a_end = _pick_row(m_c, e), _pick_row(b_c, e), _pick_row(a_c, e)
    w_end = jnp.exp(b_end - b_c + gi_c - m_end)
    return qb, kb, s, w, a_c, m_c, den, w_end, a_end, m_end


MLSTM_HEADS_PER_STEP = 4


def _heads_per_step():
    return MLSTM_HEADS_PER_STEP if NH_A % MLSTM_HEADS_PER_STEP == 0 else 1


def _mlstm_fwd(dm, qk_act, P, G, reverse):
    TB, T = dm.TB, dm.T
    NC = TB // CHUNK
    idx = _mlstm_order(dm, reverse, False)
    vcol = dm.off["va"] // DV_A
    base = 2 * NH_A if reverse else 0

    HP = _heads_per_step()
    hcols = lambda hh, w: slice(hh * w, (hh + 1) * w)

    def kern(q_ref, k_ref, v_ref, g_ref, h_ref, cst_ref, nm_ref, C_s, N_s):
        i, hp = pl.program_id(0), pl.program_id(1)
        row8 = lax.broadcasted_iota(jnp.int32, (SUBLANE, DK_A), 0)
        heads = [hp * HP + hh for hh in range(HP)]

        @pl.when(i == 0)
        def _():
            for h in heads:
                C_s[h] = jnp.zeros((DK_A, DV_A), F32)
                N_s[h] = jnp.where(row8 == 1, M_INIT, 0.0)

        masks = _chunk_masks(reverse)
        state = [(C_s[h], N_s.at[h][0:1, :], N_s.at[h][1:2, 0:1]) for h in heads]
        for c in (range(NC - 1, -1, -1) if reverse else range(NC)):
            rows = pl.ds(c * CHUNK, CHUNK)
            g = g_ref[rows, :]
            for hh, h in enumerate(heads):
                C0, n0, m0 = state[hh]
                q, k, v = q_ref[rows, hcols(hh, DK_A)], k_ref[rows, hcols(hh, DK_A)], v_ref[rows, hcols(hh, DV_A)]
                cst_ref[hh, c] = C0
                nm_ref[hh, c] = jnp.where(row8 == 0, n0, jnp.where(row8 == 1, m0, 0.0))
                qb, kb, s, w, a_c, m_c, den, w_end, a_end, m_end = _chunk_fwd_core(
                    q, k, g, base + h, base + NH_A + h, C0, n0, m0, masks, reverse)
                vb = v.astype(BF16)
                num = a_c * _dot(qb, C0.astype(BF16)) + _dot(s.astype(BF16), vb)
                h_ref[rows, hcols(hh, DV_A)] = num / jnp.maximum(jnp.abs(den), jnp.exp(-m_c))
                state[hh] = (a_end * C0 + _dot(k.T.astype(BF16), (w_end * v).astype(BF16)),
                             a_end * n0 + jnp.sum(w_end * k, axis=0, keepdims=True), m_end)
        for hh, h in enumerate(heads):
            C0, n0, m0 = state[hh]
            C_s[h] = C0
            N_s[h] = jnp.where(row8 == 0, n0, jnp.where(row8 == 1, m0, 0.0))

    nch = T // CHUNK
    NG = NH_A // HP
    return pl.pallas_call(
        kern, name="mlstm_fwd_rev" if reverse else "mlstm_fwd", grid=(dm.nblk, NG),
        in_specs=[pl.BlockSpec((TB, HP * DK_A), lambda i, h: (idx(i), h)),
                  pl.BlockSpec((TB, HP * DK_A), lambda i, h: (idx(i), NG + h)),
                  pl.BlockSpec((TB, HP * DV_A), lambda i, h: (idx(i), vcol // HP + h)),
                  pl.BlockSpec((TB, LANE), lambda i, h: (idx(i), 0))],
        out_specs=[pl.BlockSpec((TB, HP * DV_A), lambda i, h: (idx(i), h)),
                   pl.BlockSpec((HP, NC, DK_A, DV_A), lambda i, h: (h, idx(i), 0, 0)),
                   pl.BlockSpec((HP, NC, SUBLANE, DK_A), lambda i, h: (h, idx(i), 0, 0))],
        out_shape=[jax.ShapeDtypeStruct((T, dm.V), F32),
                   jax.ShapeDtypeStruct((NH_A, nch, DK_A, DV_A), F32),
                   jax.ShapeDtypeStruct((NH_A, nch, SUBLANE, DK_A), F32)],
        scratch_shapes=[pltpu.VMEM((NH_A, DK_A, DV_A), F32), pltpu.VMEM((NH_A, SUBLANE, DK_A), F32)],
        compiler_params=_cparams(("arbitrary", "arbitrary")),
    )(qk_act, qk_act, P, G)


def _mlstm_bwd(dm, qk_act, P, G, Cst, NM, H, dH, acc, reverse):
    TB, T = dm.TB, dm.T
    NC = TB // CHUNK
    idx = _mlstm_order(dm, reverse, True)
    vcol = dm.off["va"] // DV_A
    base = 2 * NH_A if reverse else 0
    has_acc = acc is not None
    HP = _heads_per_step()
    hcols = lambda hh, w: slice(hh * w, (hh + 1) * w)

    def kern(*refs):
        (q_ref, k_ref, v_ref, g_ref, cst_ref, nm_ref, hh_ref, dh_ref) = refs[:8]
        p = 8
        if has_acc:
            aq_ref, ak_ref, av_ref, ag_ref = refs[p:p + 4]
            p += 4
        dq_ref, dk_ref, dv_ref, dg_ref, R_s, Rn_s = refs[p:p + 6]
        i, hp = pl.program_id(0), pl.program_id(1)
        row8 = lax.broadcasted_iota(jnp.int32, (SUBLANE, DK_A), 0)
        heads = [hp * HP + hh for hh in range(HP)]

        @pl.when(i == 0)
        def _():
            for h in heads:
                R_s[h] = jnp.zeros((DK_A, DV_A), F32)
                Rn_s[h] = jnp.zeros((SUBLANE, DK_A), F32)

        @pl.when(hp == 0)
        def _():
            dg_ref[...] = ag_ref[...] if has_acc else jnp.zeros((TB, LANE), F32)

        masks = _chunk_masks(reverse)
        _, mask_f, maskT_f, eye_f = masks
        before_f = mask_f - eye_f
        state = [(R_s[h], Rn_s.at[h][0:1, :]) for h in heads]
        lane = lax.broadcasted_iota(jnp.int32, (CHUNK, LANE), 1)

        def as_row(col):
            return jnp.sum(eye_f * col, axis=0, keepdims=True)

        for c in (range(NC) if reverse else range(NC - 1, -1, -1)):
            rows = pl.ds(c * CHUNK, CHUNK)
            g = g_ref[rows, :]
            dg = jnp.zeros((CHUNK, LANE), F32)
            for hh, h in enumerate(heads):
                R, Rn = state[hh]
                qc, vc = hcols(hh, DK_A), hcols(hh, DV_A)
                q, k, v = q_ref[rows, qc], k_ref[rows, qc], v_ref[rows, vc]
                C0 = cst_ref[hh, c]
                n0, m0 = nm_ref.at[hh, c][0:1, :], nm_ref.at[hh, c][1:2, 0:1]
                qb, kb, s, w, a_c, m_c, den, w_end, a_end, _ = _chunk_fwd_core(
                    q, k, g, base + h, base + NH_A + h, C0, n0, m0, masks, reverse)
                vb = v.astype(BF16)
                e_m = jnp.exp(-m_c)
                r = 1.0 / jnp.maximum(jnp.abs(den), e_m)
                dh = dh_ref[rows, vc]
                dN = dh * r
                dD = jnp.where(jnp.abs(den) > e_m,
                               -jnp.sum(dh * hh_ref[rows, vc], axis=1, keepdims=True) * r * jnp.sign(den), 0.0)
                dNb = dN.astype(BF16)
                dS = _dot_nt(dNb, vb) + dD
                dqk = dS * w
                Cb, Rb = C0.astype(BF16), R.astype(BF16)
                dq_in = a_c * (_dot_nt(dNb, Cb) + dD * n0)
                dk_out = w_end * (_dot_nt(vb, Rb) + Rn)
                dq = _dot(dqk.astype(BF16), kb) + dq_in
                dk = _dot(dqk.T.astype(BF16), qb) + dk_out
                dv = _dot(s.T.astype(BF16), dNb) + w_end * _dot(kb, Rb)
                if has_acc:
                    dq_ref[rows, qc] = aq_ref[rows, qc] + dq
                    dk_ref[rows, qc] = ak_ref[rows, qc] + dk
                    dv_ref[rows, vc] = av_ref[rows, vc] + dv
                else:
                    dq_ref[rows, qc] = dq
                    dk_ref[rows, qc] = dk
                    dv_ref[rows, vc] = dv
                gm = dS * s
                g_row = jnp.sum(gm, axis=1, keepdims=True)
                g_col = jnp.sum(eye_f * jnp.sum(gm, axis=0, keepdims=True), axis=1, keepdims=True)
                q_in = jnp.sum(q * dq_in, axis=1, keepdims=True)
                k_out = jnp.sum(k * dk_out, axis=1, keepdims=True)
                through = a_end * (jnp.sum(jnp.sum(R * C0, axis=1, keepdims=True), axis=0, keepdims=True)
                                   + jnp.sum(Rn * n0, axis=1, keepdims=True))
                di = g_col + k_out
                df = (jnp.sum(maskT_f * as_row(g_row - g_col + q_in), axis=1, keepdims=True)
                      + jnp.sum(before_f * as_row(k_out), axis=1, keepdims=True) + through)
                dg = dg + jnp.where(lane == base + h, di, 0.0) + jnp.where(lane == base + NH_A + h, df, 0.0)
                aq = a_c * q
                state[hh] = (a_end * R + _dot(aq.T.astype(BF16), dNb),
                             a_end * Rn + jnp.sum(aq * dD, axis=0, keepdims=True))
            dg_ref[rows, :] += dg
        for hh, h in enumerate(heads):
            R, Rn = state[hh]
            R_s[h] = R
            Rn_s[h] = jnp.where(row8 == 0, Rn, 0.0)

    NG = NH_A // HP
    qspec = pl.BlockSpec((TB, HP * DK_A), lambda i, h: (idx(i), h))
    vspec = pl.BlockSpec((TB, HP * DV_A), lambda i, h: (idx(i), h))
    gspec = pl.BlockSpec((TB, LANE), lambda i, h: (idx(i), 0))
    in_specs = [qspec,
                pl.BlockSpec((TB, HP * DK_A), lambda i, h: (idx(i), NG + h)),
                pl.BlockSpec((TB, HP * DV_A), lambda i, h: (idx(i), vcol // HP + h)),
                gspec,
                pl.BlockSpec((HP, NC, DK_A, DV_A), lambda i, h: (h, idx(i), 0, 0)),
                pl.BlockSpec((HP, NC, SUBLANE, DK_A), lambda i, h: (h, idx(i), 0, 0)),
                vspec, vspec]
    args = [qk_act, qk_act, P, G, Cst, NM, H, dH]
    if has_acc:
        in_specs += [qspec, qspec, vspec, gspec]
        args += list(acc)
    return pl.pallas_call(
        kern, name="mlstm_bwd_rev" if reverse else "mlstm_bwd", grid=(dm.nblk, NG),
        in_specs=in_specs,
        out_specs=[qspec, qspec, vspec, gspec],
        out_shape=[jax.ShapeDtypeStruct((T, dm.QK), F32), jax.ShapeDtypeStruct((T, dm.QK), F32),
                   jax.ShapeDtypeStruct((T, dm.V), F32), jax.ShapeDtypeStruct((T, LANE), F32)],
        scratch_shapes=[pltpu.VMEM((NH_A, DK_A, DV_A), F32), pltpu.VMEM((NH_A, SUBLANE, DK_A), F32)],
        compiler_params=_cparams(("arbitrary", "arbitrary")),
    )(*args)


def _rms_heads(x, w_row, nh, hd):
    out = []
    for h in range(nh):
        xh = x[:, h * hd:(h + 1) * hd]
        rstd = lax.rsqrt(jnp.mean(xh * xh, axis=1, keepdims=True) + EPS)
        out.append((xh * rstd, rstd))
    return out


def _rope(x, cos, sa, sb):
    return x * cos + pltpu.roll(x, HD_B - HD_B // 4, 1) * sa + pltpu.roll(x, HD_B // 4, 1) * sb


def _rope_t(dy, cos, sa, sb):
    return dy * cos + pltpu.roll(dy * sa, HD_B // 4, 1) + pltpu.roll(dy * sb, HD_B - HD_B // 4, 1)


ATT_SCALE = HD_B ** -0.5
LOG2E = 1.4426950408889634
LN2 = 0.6931471805599453
QSCALE = ATT_SCALE * LOG2E


def _qk_prep(dm, P, rope, qn, kn):
    TB, nctx = dm.TB, dm.nctx
    qcol, kcol, vcol = dm.off["qb"] // dm.QB, dm.off["kb"] // dm.KVB, dm.off["vb"] // dm.KVB

    def kern_q(x_ref, t_ref, w_ref, o_ref):
        cos, sa, sb = t_ref[0], t_ref[1], t_ref[2]
        for h, (xn, _) in enumerate(_rms_heads(x_ref[...], None, NH_B, HD_B)):
            o_ref[h] = (_rope(xn * w_ref[0:1, :], cos, sa, sb) * QSCALE).astype(BF16)

    Qr = pl.pallas_call(
        kern_q, name="q_prep", grid=(dm.nlat,),
        in_specs=[pl.BlockSpec((TB, dm.QB), lambda i: (i + nctx, qcol)),
                  pl.BlockSpec((3, TB, HD_B), lambda i: (0, i + nctx, 0)),
                  pl.BlockSpec((SUBLANE, HD_B), lambda i: (0, 0))],
        out_specs=pl.BlockSpec((NH_B, TB, HD_B), lambda i: (0, i, 0)),
        out_shape=jax.ShapeDtypeStruct((NH_B, dm.S, HD_B), BF16),
        compiler_params=_cparams(("parallel",)),
    )(P, rope, qn)

    def kern_k(x_ref, v_ref, t_ref, w_ref, o_ref, vo_ref):
        cos, sa, sb = t_ref[0], t_ref[1], t_ref[2]
        for h, (xn, _) in enumerate(_rms_heads(x_ref[...], None, NKV_B, HD_B)):
            o_ref[:, h * HD_B:(h + 1) * HD_B] = _rope(xn * w_ref[0:1, :], cos, sa, sb).astype(BF16)
        vo_ref[...] = v_ref[...].astype(BF16)

    Kr, Vb = pl.pallas_call(
        kern_k, name="k_prep", grid=(dm.nblk,),
        in_specs=[pl.BlockSpec((TB, dm.KVB), lambda i: (i, kcol)),
                  pl.BlockSpec((TB, dm.KVB), lambda i: (i, vcol)),
                  pl.BlockSpec((3, TB, HD_B), lambda i: (0, i, 0)),
                  pl.BlockSpec((SUBLANE, HD_B), lambda i: (0, 0))],
        out_specs=[pl.BlockSpec((TB, dm.KVB), lambda i: (i, 0))] * 2,
        out_shape=[jax.ShapeDtypeStruct((dm.T, dm.KVB), BF16)] * 2,
        compiler_params=_cparams(("parallel",)),
    )(P, P, rope, kn)
    return Qr, Kr, Vb


def _attn_tiles(dm):
    return _tile(dm.S, 512, LANE), _tile(dm.T, 768, LANE)


def _attn_fwd(dm, Qr, Kr, Vb):
    S, T, G = dm.S, dm.T, dm.G
    tq, tk = _tile(S, 128, LANE), T
    nk = T // tk

    def kern(q_ref, k_ref, v_ref, o_ref, l_ref, m_s, l_s, a_s):
        j = pl.program_id(2)
        k, v = k_ref[...], v_ref[...]
        if nk == 1:
            for h in range(G):
                s = _dot_nt(q_ref[h], k)
                m = jnp.max(s, axis=1, keepdims=True)
                p = jnp.exp2(s - m)
                l = jnp.sum(p, axis=1, keepdims=True)
                o_ref[h] = _dot(p.astype(BF16), v) / l
                l_ref[0, :, h:h + 1] = m + jnp.log(l) * LOG2E
            return

        @pl.when(j == 0)
        def _():
            m_s[...] = jnp.full(m_s.shape, -jnp.inf, F32)
            l_s[...] = jnp.zeros(l_s.shape, F32)
            a_s[...] = jnp.zeros(a_s.shape, F32)

        for h in range(G):
            s = _dot_nt(q_ref[h], k)
            m_old = m_s[h]
            m_new = jnp.maximum(m_old, jnp.max(s, axis=1, keepdims=True))
            p = jnp.exp2(s - m_new)
            corr = jnp.exp2(m_old - m_new)
            l_s[h] = corr * l_s[h] + jnp.sum(p, axis=1, keepdims=True)
            m_s[h] = m_new
            a_s[h] = corr * a_s[h] + _dot(p.astype(BF16), v)

        @pl.when(j == nk - 1)
        def _():
            for h in range(G):
                o_ref[h] = a_s[h] / l_s[h]
                l_ref[0, :, h:h + 1] = m_s[h] + jnp.log(l_s[h]) * LOG2E

    qspec = pl.BlockSpec((G, tq, HD_B), lambda g, i, j: (g, i, 0))
    return pl.pallas_call(
        kern, name="attn_fwd", grid=(NKV_B, S // tq, nk),
        in_specs=[qspec,
                  pl.BlockSpec((tk, HD_B), lambda g, i, j: (j, g)),
                  pl.BlockSpec((tk, HD_B), lambda g, i, j: (j, g))],
        out_specs=[qspec, pl.BlockSpec((1, tq, G), lambda g, i, j: (g, i, 0))],
        out_shape=[jax.ShapeDtypeStruct((NH_B, S, HD_B), F32), jax.ShapeDtypeStruct((NKV_B, S, G), F32)],
        scratch_shapes=[pltpu.VMEM((G, tq, 1), F32), pltpu.VMEM((G, tq, 1), F32),
                        pltpu.VMEM((G, tq, HD_B), F32)],
        compiler_params=_cparams(("parallel", "parallel", "arbitrary")),
    )(Qr, Kr, Vb)


def _attn_bwd(dm, Qr, Kr, Vb, dO, LSE_T, DEL_T):
    S, T, G = dm.S, dm.T, dm.G
    tq, tk = _attn_tiles(dm)
    nq, nkt = S // tq, T // tk

    def kern(q_ref, k_ref, v_ref, do_ref, l_ref, d_ref, dq_hbm, dk_ref, dv_ref, ak_s, av_s, dq_s, stage, sem):
        g, j, i = pl.program_id(0), pl.program_id(1), pl.program_id(2)
        rows = pl.ds(pl.multiple_of(i * tq, tq), tq)

        @pl.when(i == 0)
        def _():
            ak_s[...] = jnp.zeros(ak_s.shape, F32)
            av_s[...] = jnp.zeros(av_s.shape, F32)

        @pl.when(j == 0)
        def _():
            dq_s[:, rows, :] = jnp.zeros((G, tq, HD_B), F32)

        k, v = k_ref[...], v_ref[...]
        for h in range(G):
            q, do = q_ref[h], do_ref[h]
            pT = jnp.exp2(_dot_nt(k, q) - l_ref[0, h:h + 1, :])
            dpT = _dot_nt(v, do)
            dsT = (pT * (dpT - d_ref[0, h:h + 1, :])).astype(BF16)
            av_s[...] += _dot(pT.astype(BF16), do)
            ak_s[...] += _dot(dsT, q)
            dq_s[h, rows, :] += lax.dot_general(dsT, k, (((0,), (0,)), ((), ())),
                                                preferred_element_type=F32)

        @pl.when(i == nq - 1)
        def _():
            dk_ref[...] = ak_s[...] * LN2
            dv_ref[...] = av_s[...].astype(BF16)

        @pl.when(j == nkt - 1)
        def _():
            stage[...] = dq_s[:, rows, :] * ATT_SCALE
            out = pltpu.make_async_copy(stage, dq_hbm.at[pl.ds(g * G, G), rows, :], sem)
            out.start()
            out.wait()

    qspec = pl.BlockSpec((G, tq, HD_B), lambda g, j, i: (g, i, 0))
    kspec = pl.BlockSpec((tk, HD_B), lambda g, j, i: (j, g))
    lspec = pl.BlockSpec((1, G, tq), lambda g, j, i: (g, 0, i))
    return pl.pallas_call(
        kern, name="attn_bwd", grid=(NKV_B, nkt, nq),
        in_specs=[qspec, kspec, kspec, qspec, lspec, lspec],
        out_specs=[pl.BlockSpec(memory_space=pl.ANY), kspec, kspec],
        out_shape=[jax.ShapeDtypeStruct((NH_B, S, HD_B), F32), jax.ShapeDtypeStruct((T, dm.KVB), F32),
                   jax.ShapeDtypeStruct((T, dm.KVB), BF16)],
        scratch_shapes=[pltpu.VMEM((tk, HD_B), F32), pltpu.VMEM((tk, HD_B), F32),
                        pltpu.VMEM((G, S, HD_B), F32), pltpu.VMEM((G, tq, HD_B), F32),
                        pltpu.SemaphoreType.DMA],
        compiler_params=_cparams(("arbitrary", "arbitrary", "arbitrary")),
    )(Qr, Kr, Vb, dO, LSE_T, DEL_T)


def _qk_bwd(dm, dQr, dKr, P, rope, qn, kn):
    TB, nctx = dm.TB, dm.nctx
    qcol, kcol = dm.off["qb"] // dm.QB, dm.off["kb"] // dm.KVB

    def head_bwd(dyr, x, w_row, cos, sa, sb):
        rstd = lax.rsqrt(jnp.mean(x * x, axis=1, keepdims=True) + EPS)
        xn = x * rstd
        dy = _rope_t(dyr, cos, sa, sb)
        dw = jnp.sum(dy * xn, axis=0, keepdims=True)
        dxn = dy * w_row
        dx = rstd * (dxn - xn * jnp.mean(dxn * xn, axis=1, keepdims=True))
        return dx, dw

    def make(nh, ctx_zero):
        def kern(d_ref, x_ref, t_ref, w_ref, o_ref, gw_ref):
            i = pl.program_id(0)

            @pl.when(i == 0)
            def _():
                gw_ref[...] = jnp.zeros(gw_ref.shape, F32)

            def live():
                cos, sa, sb = t_ref[0], t_ref[1], t_ref[2]
                tot = jnp.zeros((1, HD_B), F32)
                for h in range(nh):
                    cols = slice(h * HD_B, (h + 1) * HD_B)
                    dyr = d_ref[h] if ctx_zero else d_ref[:, cols]
                    dx, dw = head_bwd(dyr, x_ref[:, cols], w_ref[0:1, :], cos, sa, sb)
                    o_ref[:, cols] = dx.astype(BF16)
                    tot = tot + dw
                gw_ref[0:1, :] += tot

            if ctx_zero:
                @pl.when(i < nctx)
                def _():
                    o_ref[...] = jnp.zeros(o_ref.shape, BF16)

                pl.when(i >= nctx)(live)
            else:
                live()
        return kern

    lat = lambda i: jnp.maximum(i - nctx, 0)
    d_qb, gqn = pl.pallas_call(
        make(NH_B, True), name="q_bwd", grid=(dm.nblk,),
        in_specs=[pl.BlockSpec((NH_B, TB, HD_B), lambda i: (0, lat(i), 0)),
                  pl.BlockSpec((TB, dm.QB), lambda i: (i, qcol)),
                  pl.BlockSpec((3, TB, HD_B), lambda i: (0, i, 0)),
                  pl.BlockSpec((SUBLANE, HD_B), lambda i: (0, 0))],
        out_specs=[pl.BlockSpec((TB, dm.QB), lambda i: (i, 0)), pl.BlockSpec((SUBLANE, HD_B), lambda i: (0, 0))],
        out_shape=[jax.ShapeDtypeStruct((dm.T, dm.QB), BF16), jax.ShapeDtypeStruct((SUBLANE, HD_B), F32)],
        compiler_params=_cparams(("arbitrary",)),
    )(dQr, P, rope, qn)
    d_kb, gkn = pl.pallas_call(
        make(NKV_B, False), name="k_bwd", grid=(dm.nblk,),
        in_specs=[pl.BlockSpec((TB, dm.KVB), lambda i: (i, 0)),
                  pl.BlockSpec((TB, dm.KVB), lambda i: (i, kcol)),
                  pl.BlockSpec((3, TB, HD_B), lambda i: (0, i, 0)),
                  pl.BlockSpec((SUBLANE, HD_B), lambda i: (0, 0))],
        out_specs=[pl.BlockSpec((TB, dm.KVB), lambda i: (i, 0)), pl.BlockSpec((SUBLANE, HD_B), lambda i: (0, 0))],
        out_shape=[jax.ShapeDtypeStruct((dm.T, dm.KVB), BF16), jax.ShapeDtypeStruct((SUBLANE, HD_B), F32)],
        compiler_params=_cparams(("arbitrary",)),
    )(dKr, P, rope, kn)
    return d_qb, d_kb, gqn, gkn


def _merge_prep(dm, Hf, Hb, P, O, mhw):
    TB, nctx = dm.TB, dm.nctx
    lat = lambda c: (lambda i: (i + nctx, c))

    def kern(hf_ref, hb_ref, oa_ref, za_ref, zb_ref, o_ref, w_ref, a_ref, b_ref, at_ref, bt_ref):
        for h in range(NH_A):
            cols = slice(h * DV_A, (h + 1) * DV_A)
            hs = hf_ref[:, cols] + hb_ref[:, cols]
            rstd = lax.rsqrt(jnp.mean(hs * hs, axis=1, keepdims=True) + EPS)
            za = za_ref[:, cols]
            a = _sigmoid(oa_ref[:, cols]) * (hs * rstd * w_ref[0:1, cols]) * (za * _sigmoid(za))
            a_ref[:, cols] = a.astype(BF16)
            at_ref[cols, :] = a.T.astype(BF16)
        for h in range(NH_B):
            cols = slice(h * HD_B, (h + 1) * HD_B)
            zb = zb_ref[:, cols]
            b = o_ref[h] * (zb * _sigmoid(zb))
            b_ref[:, cols] = b.astype(BF16)
            bt_ref[cols, :] = b.T.astype(BF16)

    return pl.pallas_call(
        kern, name="merge_prep", grid=(dm.nlat,),
        in_specs=[pl.BlockSpec((TB, dm.V), lat(0)), pl.BlockSpec((TB, dm.V), lat(0)),
                  pl.BlockSpec((TB, dm.V), lat(dm.off["oa"] // dm.V)),
                  pl.BlockSpec((TB, dm.V), lat(dm.off["za"] // dm.V)),
                  pl.BlockSpec((TB, dm.QB), lat(dm.off["zb"] // dm.QB)),
                  pl.BlockSpec((NH_B, TB, HD_B), lambda i: (0, i, 0)),
                  pl.BlockSpec((SUBLANE, dm.V), lambda i: (0, 0))],
        out_specs=[pl.BlockSpec((TB, dm.V), lambda i: (i, 0)), pl.BlockSpec((TB, dm.QB), lambda i: (i, 0)),
                   pl.BlockSpec((dm.V, TB), lambda i: (0, i)), pl.BlockSpec((dm.QB, TB), lambda i: (0, i))],
        out_shape=[jax.ShapeDtypeStruct((dm.S, dm.V), BF16), jax.ShapeDtypeStruct((dm.S, dm.QB), BF16),
                   jax.ShapeDtypeStruct((dm.V, dm.S), BF16), jax.ShapeDtypeStruct((dm.QB, dm.S), BF16)],
        compiler_params=_cparams(("parallel",)),
    )(Hf, Hb, P, P, P, O, mhw)


def _gate_merge(dm, ya, yb, P):
    TB, D, nctx = dm.TB, dm.D, dm.nctx
    gcol = dm.off["g"] // D

    def kern(ya_ref, yb_ref, ga_ref, gb_ref, o_ref, ot_ref):
        m = _sigmoid(ga_ref[...]) * ya_ref[...] + _sigmoid(gb_ref[...]) * yb_ref[...]
        o_ref[...] = m.astype(BF16)
        ot_ref[...] = m.T.astype(BF16)

    row = pl.BlockSpec((TB, D), lambda i: (i, 0))
    return pl.pallas_call(
        kern, name="gate_merge", grid=(dm.nlat,),
        in_specs=[row, row, pl.BlockSpec((TB, D), lambda i: (i + nctx, gcol)),
                  pl.BlockSpec((TB, D), lambda i: (i + nctx, gcol + 1))],
        out_specs=[row, pl.BlockSpec((D, TB), lambda i: (0, i))],
        out_shape=[jax.ShapeDtypeStruct((dm.S, D), BF16), jax.ShapeDtypeStruct((D, dm.S), BF16)],
        compiler_params=_cparams(("parallel",)),
    )(ya, yb, P, P)


def _final(dm, x, out, tgt, modv, lnp):
    TB, D = dm.TB, dm.D

    def kern(x_ref, o_ref, t_ref, m_ref, p_ref, dr_ref, do_ref, cs_ref, ls_ref):
        i = pl.program_id(0)

        @pl.when(i == 0)
        def _():
            cs_ref[...] = jnp.zeros(cs_ref.shape, F32)
            ls_ref[...] = jnp.zeros(ls_ref.shape, F32)

        gate, lnw, lnb = m_ref[4:5, :], p_ref[0:1, :], p_ref[1:2, :]
        out = o_ref[...]
        xh, rstd = _ln_stats(ALPHA * x_ref[...] + gate * out)
        e = xh * lnw + lnb - t_ref[...]
        ls_ref[...] += 0.5 * jnp.sum(jnp.sum(e * e, axis=1, keepdims=True), axis=0, keepdims=True) / D
        dy = e * (1.0 / D)
        dxh = dy * lnw
        dr = rstd * (dxh - jnp.mean(dxh, axis=1, keepdims=True)
                     - xh * jnp.mean(dxh * xh, axis=1, keepdims=True))
        cs_ref[0:1, :] += jnp.sum(dy * xh, axis=0, keepdims=True)
        cs_ref[1:2, :] += jnp.sum(dy, axis=0, keepdims=True)
        cs_ref[2:3, :] += jnp.sum(dr * out, axis=0, keepdims=True)
        dr_ref[...] = ALPHA * dr
        do_ref[...] = (dr * gate).astype(BF16)

    row = pl.BlockSpec((TB, D), lambda i: (i, 0))
    par = pl.BlockSpec((SUBLANE, D), lambda i: (0, 0))
    return pl.pallas_call(
        kern, name="final_norm_loss", grid=(dm.nlat,),
        in_specs=[row, row, row, par, par],
        out_specs=[row, row, par, pl.BlockSpec((SUBLANE, LANE), lambda i: (0, 0))],
        out_shape=[jax.ShapeDtypeStruct((dm.S, D), F32), jax.ShapeDtypeStruct((dm.S, D), BF16),
                   jax.ShapeDtypeStruct((SUBLANE, D), F32), jax.ShapeDtypeStruct((SUBLANE, LANE), F32)],
        compiler_params=_cparams(("arbitrary",)),
    )(x, out, tgt, modv, lnp)


def _merge_bwd(dm, dM, ya, yb, P):
    TB, D, nctx = dm.TB, dm.D, dm.nctx
    gcol = dm.off["g"] // D
    lat = lambda i: jnp.maximum(i - nctx, 0)

    def kern(dm_ref, ya_ref, yb_ref, ga_ref, gb_ref, da_ref, db_ref, dg_ref):
        i = pl.program_id(0)

        @pl.when(i < nctx)
        def _():
            dg_ref[...] = jnp.zeros(dg_ref.shape, BF16)

        @pl.when(i >= nctx)
        def _():
            d = dm_ref[...]
            sa, sb = _sigmoid(ga_ref[...]), _sigmoid(gb_ref[...])
            da_ref[...] = (d * sa).astype(BF16)
            db_ref[...] = (d * sb).astype(BF16)
            dg_ref[:, 0:D] = (d * ya_ref[...] * sa * (1.0 - sa)).astype(BF16)
            dg_ref[:, D:2 * D] = (d * yb_ref[...] * sb * (1.0 - sb)).astype(BF16)

    row = pl.BlockSpec((TB, D), lambda i: (lat(i), 0))
    return pl.pallas_call(
        kern, name="merge_bwd", grid=(dm.nblk,),
        in_specs=[row, row, row, pl.BlockSpec((TB, D), lambda i: (i, gcol)),
                  pl.BlockSpec((TB, D), lambda i: (i, gcol + 1))],
        out_specs=[row, row, pl.BlockSpec((TB, 2 * D), lambda i: (i, 0))],
        out_shape=[jax.ShapeDtypeStruct((dm.S, D), BF16), jax.ShapeDtypeStruct((dm.S, D), BF16),
                   jax.ShapeDtypeStruct((dm.T, 2 * D), BF16)],
        compiler_params=_cparams(("arbitrary",)),
    )(dM, ya, yb, P, P)


def _branch_bwd(dm, dA, dB, Hf, Hb, P, O, mhw):
    TB, nctx, G = dm.TB, dm.nctx, dm.G
    lat = lambda i: jnp.maximum(i - nctx, 0)

    def kern(da_ref, db_ref, hf_ref, hb_ref, oa_ref, za_ref, zb_ref, o_ref, w_ref,
             doa_ref, dza_ref, dzb_ref, dh_ref, do_ref, del_ref, gw_ref):
        i = pl.program_id(0)

        @pl.when(i == 0)
        def _():
            gw_ref[...] = jnp.zeros(gw_ref.shape, F32)

        @pl.when(i < nctx)
        def _():
            doa_ref[...] = jnp.zeros(doa_ref.shape, BF16)
            dza_ref[...] = jnp.zeros(dza_ref.shape, BF16)
            dzb_ref[...] = jnp.zeros(dzb_ref.shape, BF16)
            dh_ref[...] = jnp.zeros(dh_ref.shape, F32)

        @pl.when(i >= nctx)
        def _():
            for h in range(NH_A):
                cols = slice(h * DV_A, (h + 1) * DV_A)
                hs = hf_ref[:, cols] + hb_ref[:, cols]
                rstd = lax.rsqrt(jnp.mean(hs * hs, axis=1, keepdims=True) + EPS)
                xn = hs * rstd
                w = w_ref[0:1, cols]
                hn = xn * w
                so, za = _sigmoid(oa_ref[:, cols]), za_ref[:, cols]
                sz = _sigmoid(za)
                silu = za * sz
                da = da_ref[:, cols]
                doa_ref[:, cols] = (da * hn * silu * so * (1.0 - so)).astype(BF16)
                dza_ref[:, cols] = (da * hn * so * sz * (1.0 + za * (1.0 - sz))).astype(BF16)
                dhn = da * so * silu
                gw_ref[0:1, cols] += jnp.sum(dhn * xn, axis=0, keepdims=True)
                dxn = dhn * w
                dh_ref[:, cols] = rstd * (dxn - xn * jnp.mean(dxn * xn, axis=1, keepdims=True))
            for h in range(NH_B):
                cols = slice(h * HD_B, (h + 1) * HD_B)
                zb = zb_ref[:, cols]
                sz = _sigmoid(zb)
                db, o = db_ref[:, cols], o_ref[h]
                do = db * (zb * sz)
                do_ref[h] = do.astype(BF16)
                dzb_ref[:, cols] = (db * o * sz * (1.0 + zb * (1.0 - sz))).astype(BF16)
                del_ref[h // G, :, (h % G):(h % G) + 1] = jnp.sum(do * o, axis=1, keepdims=True)

    vlat = pl.BlockSpec((TB, dm.V), lambda i: (lat(i), 0))
    qlat = pl.BlockSpec((TB, dm.QB), lambda i: (lat(i), 0))
    hlat = pl.BlockSpec((NH_B, TB, HD_B), lambda i: (0, lat(i), 0))
    vrow = pl.BlockSpec((TB, dm.V), lambda i: (i, 0))
    qrow = pl.BlockSpec((TB, dm.QB), lambda i: (i, 0))
    pv = lambda n: pl.BlockSpec((TB, dm.V), lambda i: (i, dm.off[n] // dm.V))
    return pl.pallas_call(
        kern, name="branch_bwd", grid=(dm.nblk,),
        in_specs=[vlat, qlat, vrow, vrow, pv("oa"), pv("za"),
                  pl.BlockSpec((TB, dm.QB), lambda i: (i, dm.off["zb"] // dm.QB)), hlat,
                  pl.BlockSpec((SUBLANE, dm.V), lambda i: (0, 0))],
        out_specs=[vrow, vrow, qrow, vrow, hlat,
                   pl.BlockSpec((NKV_B, TB, G), lambda i: (0, lat(i), 0)),
                   pl.BlockSpec((SUBLANE, dm.V), lambda i: (0, 0))],
        out_shape=[jax.ShapeDtypeStruct((dm.T, dm.V), BF16), jax.ShapeDtypeStruct((dm.T, dm.V), BF16),
                   jax.ShapeDtypeStruct((dm.T, dm.QB), BF16), jax.ShapeDtypeStruct((dm.T, dm.V), F32),
                   jax.ShapeDtypeStruct((NH_B, dm.S, HD_B), BF16), jax.ShapeDtypeStruct((NKV_B, dm.S, G), F32),
                   jax.ShapeDtypeStruct((SUBLANE, dm.V), F32)],
        compiler_params=_cparams(("arbitrary",)),
    )(dA, dB, Hf, Hb, P, P, P, O, mhw)


def _conv_bwd(dm, dq, dk, P, convp):
    W = 2 * dm.QK
    col = dm.off["qk"] // W
    kscale = DK_A ** -0.5
    TB = dm.TB

    def kern1(dq_ref, dk_ref, x_ref, p_ref, n_ref, c_ref, dz_ref):
        x = x_ref[...]
        xp, xn = _shifted(dm, x, p_ref, n_ref)
        z = c_ref[3:4, :] + xp * c_ref[0:1, :] + x * c_ref[1:2, :] + xn * c_ref[2:3, :]
        sz = _sigmoid(z)
        dact = jnp.concatenate([dq_ref[...], dk_ref[...] * kscale], axis=1)
        dz_ref[...] = dact * sz * (1.0 + z * (1.0 - sz))

    half = pl.BlockSpec((TB, dm.QK), lambda i: (i, 0))
    par = pl.BlockSpec((SUBLANE, W), lambda i: (0, 0))
    dz = pl.pallas_call(
        kern1, name="conv_bwd_act", grid=(dm.nblk,),
        in_specs=[half, half] + _conv_specs(dm, W, col) + [par],
        out_specs=pl.BlockSpec((TB, W), lambda i: (i, 0)),
        out_shape=jax.ShapeDtypeStruct((dm.T, W), F32),
        compiler_params=_cparams(("parallel",)),
    )(dq, dk, P, P, P, convp)

    def kern2(z_ref, zp_ref, zn_ref, x_ref, p_ref, n_ref, c_ref, dx_ref, cs_ref):
        @pl.when(pl.program_id(0) == 0)
        def _():
            cs_ref[...] = jnp.zeros(cs_ref.shape, F32)

        dz = z_ref[...]
        dzp, dzn = _shifted(dm, dz, zp_ref, zn_ref)
        dx_ref[...] = (dzn * c_ref[0:1, :] + dz * c_ref[1:2, :] + dzp * c_ref[2:3, :]).astype(BF16)
        x = x_ref[...]
        xp, xn = _shifted(dm, x, p_ref, n_ref)
        cs_ref[0:1, :] += jnp.sum(dz * xp, axis=0, keepdims=True)
        cs_ref[1:2, :] += jnp.sum(dz * x, axis=0, keepdims=True)
        cs_ref[2:3, :] += jnp.sum(dz * xn, axis=0, keepdims=True)
        cs_ref[3:4, :] += jnp.sum(dz, axis=0, keepdims=True)

    return pl.pallas_call(
        kern2, name="conv_bwd_taps", grid=(dm.nblk,),
        in_specs=_conv_specs(dm, W, 0) + _conv_specs(dm, W, col) + [par],
        out_specs=[pl.BlockSpec((TB, W), lambda i: (i, 0)), par],
        out_shape=[jax.ShapeDtypeStruct((dm.T, W), BF16), jax.ShapeDtypeStruct((SUBLANE, W), F32)],
        compiler_params=_cparams(("arbitrary",)),
    )(dz, dz, dz, P, P, P, convp)


def _gates_bwd(dm, dG, G):
    TB = dm.TB

    def kern(d_ref, g_ref, o_ref, cs_ref):
        @pl.when(pl.program_id(0) == 0)
        def _():
            cs_ref[...] = jnp.zeros(cs_ref.shape, F32)

        lane = lax.broadcasted_iota(jnp.int32, (TB, LANE), 1)
        is_f = ((lane // NH_A) % 2) == 1
        d = d_ref[...]
        dpre = jnp.where(lane < dm.NIF, jnp.where(is_f, d * (1.0 - jnp.exp(g_ref[...])), d), 0.0)
        cs_ref[0:1, :] += jnp.sum(dpre, axis=0, keepdims=True)
        if dm.IFP > LANE:
            o_ref[:, LANE:] = jnp.zeros((TB, dm.IFP - LANE), BF16)
        o_ref[:, 0:LANE] = dpre.astype(BF16)

    row = pl.BlockSpec((TB, LANE), lambda i: (i, 0))
    return pl.pallas_call(
        kern, name="gates_bwd", grid=(dm.nblk,),
        in_specs=[row, row],
        out_specs=[pl.BlockSpec((TB, dm.IFP), lambda i: (i, 0)), pl.BlockSpec((SUBLANE, LANE), lambda i: (0, 0))],
        out_shape=[jax.ShapeDtypeStruct((dm.T, dm.IFP), BF16), jax.ShapeDtypeStruct((SUBLANE, LANE), F32)],
        compiler_params=_cparams(("arbitrary",)),
    )(dG, G)


def _ln_mod_bwd(dm, dU, ctx2, x2, modv, dr_a):
    TB, D, nctx = dm.TB, dm.D, dm.nctx
    lat = lambda i: jnp.maximum(i - nctx, 0)

    def kern(du_ref, c_ref, x_ref, m_ref, dr_ref, gx_ref, cs_ref):
        i = pl.program_id(0)
        is_ctx = i < nctx

        @pl.when(i == 0)
        def _():
            cs_ref[...] = jnp.zeros(cs_ref.shape, F32)

        xh, rstd = _ln_stats(jnp.where(is_ctx, c_ref[...], x_ref[...]))
        du = du_ref[...]
        s_shift = jnp.sum(du, axis=0, keepdims=True)
        s_scale = jnp.sum(du * xh, axis=0, keepdims=True)
        cs_ref[0:1, :] += jnp.where(is_ctx, 0.0, s_shift)
        cs_ref[1:2, :] += jnp.where(is_ctx, 0.0, s_scale)
        cs_ref[2:3, :] += jnp.where(is_ctx, s_shift, 0.0)
        cs_ref[3:4, :] += jnp.where(is_ctx, s_scale, 0.0)

        @pl.when(i >= nctx)
        def _():
            dxh = du * (1.0 + m_ref[1:2, :])
            gx_ref[...] = dr_ref[...] + rstd * (dxh - jnp.mean(dxh, axis=1, keepdims=True)
                                                - xh * jnp.mean(dxh * xh, axis=1, keepdims=True))

    row = pl.BlockSpec((TB, D), lambda i: (i, 0))
    lrow = pl.BlockSpec((TB, D), lambda i: (lat(i), 0))
    par = pl.BlockSpec((SUBLANE, D), lambda i: (0, 0))
    return pl.pallas_call(
        kern, name="ln_mod_bwd", grid=(dm.nblk,),
        in_specs=[row] + _row_specs(dm) + [par, lrow],
        out_specs=[lrow, par],
        out_shape=[jax.ShapeDtypeStruct((dm.S, D), F32), jax.ShapeDtypeStruct((SUBLANE, D), F32)],
        compiler_params=_cparams(("arbitrary",)),
    )(dU, ctx2, x2, modv, dr_a)


def _mod_fwd(craw, w_loc, b_loc):
    R, D = craw.shape
    n = w_loc.shape[1]

    def kern(c_ref, w_ref, b_ref, o_ref):
        c = c_ref[...]
        o_ref[...] = _dot((c * _sigmoid(c)).astype(BF16), w_ref[...].astype(BF16)) + b_ref[0:1, :]

    return pl.pallas_call(
        kern, name="mod_fwd", out_shape=jax.ShapeDtypeStruct((R, n), F32),
        compiler_params=pltpu.CompilerParams(vmem_limit_bytes=VMEM_LIMIT),
    )(craw, w_loc, b_loc)


def _mod_bwd(crawT, dmod, w_loc):
    D, R = crawT.shape
    n = w_loc.shape[1]

    def kern(c_ref, d_ref, w_ref, gw_ref, dc_ref):
        c = c_ref[...]
        d = d_ref[...].astype(BF16)
        gw_ref[...] = _dot((c * _sigmoid(c)).astype(BF16), d)
        dc_ref[...] = _dot_nt(d, w_ref[...].astype(BF16))

    return pl.pallas_call(
        kern, name="mod_bwd",
        out_shape=[jax.ShapeDtypeStruct((D, n), F32), jax.ShapeDtypeStruct((R, D), F32)],
        compiler_params=pltpu.CompilerParams(vmem_limit_bytes=VMEM_LIMIT),
    )(crawT, dmod, w_loc)


def _cctx_grad(dsilu, c_ctx):
    def kern(p_ref, c_ref, o_ref):
        c = c_ref[...]
        sc = _sigmoid(c)
        o_ref[...] = p_ref[...] * (sc * (1.0 + c * (1.0 - sc)))

    return pl.pallas_call(kern, name="cctx_grad", out_shape=jax.ShapeDtypeStruct(c_ctx.shape, F32))(dsilu, c_ctx)


def _adamw(w, g, m, v, name):
    lead, (R, C) = w.shape[:-2], w.shape[-2:]
    assert all(d == 1 for d in lead)
    tb = _tile(R, max(SUBLANE, (1 << 19) // (4 * C) // SUBLANE * SUBLANE), SUBLANE)
    c1 = 1.0 / (1.0 - ADAM_B1 ** ADAM_STEP)
    c2 = 1.0 / (1.0 - ADAM_B2 ** ADAM_STEP)

    def kern(w_ref, g_ref, m_ref, v_ref, d_ref, nm_ref, nv_ref):
        g = g_ref[...]
        nm = ADAM_B1 * m_ref[...] + (1.0 - ADAM_B1) * g
        nv = ADAM_B2 * v_ref[...] + (1.0 - ADAM_B2) * (g * g)
        nm_ref[...] = nm
        nv_ref[...] = nv
        d_ref[...] = -ADAM_LR * ((nm * c1) / (jnp.sqrt(nv * c2) + ADAM_EPS) + ADAM_WD * w_ref[...])

    spec = pl.BlockSpec(lead + (tb, C), lambda i: (0,) * len(lead) + (i, 0))
    return pl.pallas_call(
        kern, name=name, grid=(R // tb,), in_specs=[spec] * 4, out_specs=[spec] * 3,
        out_shape=[jax.ShapeDtypeStruct(w.shape, F32)] * 3,
        compiler_params=_cparams(("parallel",)),
    )(w, g, m, v)


def _rope_tables(dm):
    half = HD_B // 2
    rows_n = dm.S // GRID_W
    row = jnp.repeat(jnp.arange(rows_n), GRID_W).astype(F32)
    col = jnp.tile(jnp.arange(GRID_W), rows_n).astype(F32)
    inv = ROPE_THETA ** (-jnp.arange(0, half, 2, dtype=F32) / half)
    ar, ac = row[:, None] * inv[None], col[:, None] * inv[None]
    cos = jnp.concatenate([jnp.cos(ar), jnp.cos(ar), jnp.cos(ac), jnp.cos(ac)], axis=1)
    zr = jnp.zeros_like(ar)
    sa = jnp.concatenate([-jnp.sin(ar), zr, -jnp.sin(ac), zr], axis=1)
    sb = jnp.concatenate([zr, jnp.sin(ar), zr, jnp.sin(ac)], axis=1)
    ctx = jnp.stack([jnp.ones((dm.Tc, HD_B), F32), jnp.zeros((dm.Tc, HD_B), F32), jnp.zeros((dm.Tc, HD_B), F32)])
    return jnp.concatenate([ctx, jnp.stack([cos, sa, sb])], axis=1)


def _rows8(*rows, width):
    out = [jnp.pad(r.reshape(-1).astype(F32), (0, width - r.size)) for r in rows]
    n = -(-len(out) // SUBLANE) * SUBLANE
    out += [jnp.zeros((width,), F32)] * (n - len(out))
    return jnp.stack(out)


def _ref_starts(dm):
    starts, o = {}, 0
    for n, wd in zip(dm.ref_names, dm.ref_widths):
        starts[n] = o
        o += wd
    return starts


def _to_padded(dm, shards):
    n = shards.shape[2]
    starts, wref = _ref_starts(dm), dict(zip(dm.ref_names, dm.ref_widths))
    pieces = []
    for name in dm.order:
        lo, hi = starts[name], starts[name] + wref[name]
        for j in range(N_DEV):
            a, b = max(lo, j * n), min(hi, (j + 1) * n)
            if a < b:
                pieces.append(shards[j][:, a - j * n:b - j * n])
        if dm.w[name] > wref[name]:
            pieces.append(jnp.zeros((shards.shape[1], dm.w[name] - wref[name]), shards.dtype))
    return jnp.concatenate(pieces, axis=1)


def _from_padded(dm, w_pad, n):
    starts = _ref_starts(dm)
    slabs = []
    for j in range(N_DEV):
        pieces = []
        for name, wd in zip(dm.ref_names, dm.ref_widths):
            a, b = max(starts[name], j * n), min(starts[name] + wd, (j + 1) * n)
            if a < b:
                o = dm.off[name] - starts[name]
                pieces.append(w_pad[:, a + o:b + o])
        slabs.append(jnp.concatenate(pieces, axis=1))
    return jnp.stack(slabs)


def kernel(x, c, ctx, c_ctx, w_mod, b_mod, w_in, b_if, conv_w, conv_b, mh_norm_w, q_norm_w, k_norm_w, w_branch_a, w_branch_b, w_out, ln_w, ln_b, loss_target, m_c_ctx, m_w_mod, m_b_mod, m_w_in, m_b_if, m_conv_w, m_conv_b, m_mh_norm_w, m_q_norm_w, m_k_norm_w, m_w_branch_a, m_w_branch_b, m_w_out, m_ln_w, m_ln_b, v_c_ctx, v_w_mod, v_b_mod, v_w_in, v_b_if, v_conv_w, v_conv_b, v_mh_norm_w, v_q_norm_w, v_k_norm_w, v_w_branch_a, v_w_branch_b, v_w_out, v_ln_w, v_ln_b):
    S, D = x.shape[1], x.shape[2]
    Tc = ctx.shape[1]
    dm = Dims(S, Tc, D)
    T, QK2 = dm.T, 2 * dm.QK
    me = 4 * lax.axis_index("x") + 2 * lax.axis_index("y") + lax.axis_index("c")
    n_mod = w_mod.shape[2]
    n_in = w_in.shape[2]
    n_cv = conv_w.shape[2]
    rb = w_out.shape[1]

    pack0 = _all_gather(_rows8(c[0], conv_w[0, 0], conv_w[0, 1], conv_w[0, 2], width=D), "ag_cond", False)
    c_all = pack0[:, 0, :]
    conv_full = jnp.transpose(pack0[:, 1:4, :n_cv], (1, 0, 2)).reshape(CONV_W, QK2)
    convp = _rows8(conv_full[0], conv_full[1], conv_full[2], conv_b[0], width=QK2)

    w_in_all = _all_gather(w_in[0].astype(BF16), "ag_w_in", True)
    Wp = _to_padded(dm, w_in_all)
    wsq = jnp.concatenate([w_branch_a[0], w_branch_b[0], w_out[0]], axis=0).astype(BF16)

    craw = _rows8(*[c_all[j] for j in range(N_DEV)], c_ctx, width=D)
    b_loc = _rows8(lax.dynamic_slice(b_mod[0], (me * n_mod,), (n_mod,)), width=n_mod)
    mod_all = _all_gather(_mod_fwd(craw, w_mod[0], b_loc), "ag_mod", False)
    mod_rows = jnp.transpose(mod_all, (1, 0, 2)).reshape(2 * SUBLANE, 3 * D)
    mod_me = lax.dynamic_slice(mod_rows, (me, 0), (1, 3 * D))[0]
    mod_cx = mod_rows[N_DEV]
    modv = _rows8(mod_me[0:D], mod_me[D:2 * D], mod_cx[0:D], mod_cx[D:2 * D], mod_me[2 * D:3 * D], width=D)

    U, UT = _ln_mod_fwd(dm, ctx[0], x[0], modv)
    P, wsq_all = _mm(U, Wp, "mm_in_proj", tn=896, exchange=("ag", wsq))
    Wba = wsq_all[:, 0:rb, :].reshape(dm.V, D)
    Wbb = wsq_all[:, rb:2 * rb, :].reshape(dm.QB, D)
    Wout = wsq_all[:, 2 * rb:3 * rb, :].reshape(D, D)
    qk_act = _conv_silu_fwd(dm, P, convp)
    G = _gates_fwd(dm, P, _rows8(b_if[0], width=LANE))
    Hf, Cf, NMf = _mlstm_fwd(dm, qk_act, P, G, False)
    Hb, Cb, NMb = _mlstm_fwd(dm, qk_act, P, G, True)
    rope = _rope_tables(dm)
    qn, kn = _rows8(q_norm_w[0], width=HD_B), _rows8(k_norm_w[0], width=HD_B)
    Qr, Kr, Vb = _qk_prep(dm, P, rope, qn, kn)
    O, LSE = _attn_fwd(dm, Qr, Kr, Vb)
    mhw = _rows8(mh_norm_w[0], width=dm.V)
    A_in, B_in, A_inT, B_inT = _merge_prep(dm, Hf, Hb, P, O, mhw)
    ya = _mm(A_in, Wba, "mm_branch_a")
    yb = _mm(B_in, Wbb, "mm_branch_b")
    M_in, M_inT = _gate_merge(dm, ya, yb, P)
    out = _mm(M_in, Wout, "mm_out")
    lnp = _rows8(ln_w[0], ln_b[0], width=D)
    dr_a, d_out, cs_fin, loss_p = _final(dm, x[0], out, loss_target[0], modv, lnp)
    loss = lax.psum(loss_p[0, 0], ("x", "y", "c"))

    dM = _mm(d_out, Wout, "mm_d_merge", nt=True)
    gWout = _mm(M_inT, d_out, "mm_g_w_out", tk=2048, out_dtype=BF16)
    d_ya, d_yb, d_g = _merge_bwd(dm, dM, ya, yb, P)
    dA = _mm(d_ya, Wba, "mm_d_a", nt=True)
    gWba = _mm(A_inT, d_ya, "mm_g_w_ba", tk=2048, out_dtype=BF16)
    dB = _mm(d_yb, Wbb, "mm_d_b", nt=True)
    gWbb = _mm(B_inT, d_yb, "mm_g_w_bb", tk=2048, out_dtype=BF16)
    d_oa, d_za, d_zb, dH, dO, DEL, gmh = _branch_bwd(dm, dA, dB, Hf, Hb, P, O, mhw)
    dQr, dKr, d_vb = _attn_bwd(dm, Qr, Kr, Vb, dO, jnp.transpose(LSE, (0, 2, 1)), jnp.transpose(DEL, (0, 2, 1)))
    d_qb, d_kb, gqn, gkn = _qk_bwd(dm, dQr, dKr, P, rope, qn, kn)
    acc = _mlstm_bwd(dm, qk_act, P, G, Cf, NMf, Hf, dH, None, False)
    dq, dk, dv, dG = _mlstm_bwd(dm, qk_act, P, G, Cb, NMb, Hb, dH, acc, True)
    d_qk, cs_conv = _conv_bwd(dm, dq, dk, P, convp)
    d_if, gbif = _gates_bwd(dm, dG, G)
    parts = {"g": d_g, "qk": d_qk, "va": dv.astype(BF16), "oa": d_oa, "za": d_za, "qb": d_qb,
             "zb": d_zb, "kb": d_kb, "vb": d_vb, "if": d_if}
    dP = jnp.concatenate([parts[n] for n in dm.order], axis=1)
    gsq = jnp.concatenate([gWba.reshape(N_DEV, rb, D), gWbb.reshape(N_DEV, rb, D),
                           gWout.reshape(N_DEV, rb, D)], axis=1)
    gWp, gsq_all = _mm(UT, dP, "mm_g_w_in", tm=1024, tn=896, tk=2816, out_dtype=BF16,
                       exchange=("a2a", gsq))
    gW = _from_padded(dm, gWp, n_in)
    dU, gW_all = _mm(dP, Wp, "mm_d_u", nt=True, tk=1792, exchange=("a2a", gW))
    grad_x, cs_ln = _ln_mod_bwd(dm, dU, ctx[0], x[0], modv, dr_a)

    dmod_me = _rows8(jnp.concatenate([cs_ln[0], cs_ln[1], cs_fin[2]]),
                     jnp.concatenate([cs_ln[2], cs_ln[3], jnp.zeros((D,), F32)]), width=3 * D)
    dmod_all = _all_gather(dmod_me, "ag_dmod", False)
    dmod_loc = lax.dynamic_slice(dmod_all, (0, 0, me * n_mod), (N_DEV, 2, n_mod))
    dmod_rows = _rows8(*[dmod_loc[j, 0] for j in range(N_DEV)], jnp.sum(dmod_loc[:, 1, :], axis=0), width=n_mod)
    g_w_mod, dc_part = _mod_bwd(craw.T, dmod_rows, w_mod[0])

    PW = dm.PW
    small = _rows8(cs_fin[0], cs_fin[1], gmh[0], cs_conv[3], cs_conv[0], cs_conv[1], cs_conv[2],
                   dmod_me[0, 0:D], dmod_me[0, D:2 * D], dmod_me[0, 2 * D:3 * D],
                   dmod_me[1, 0:D], dmod_me[1, D:2 * D],
                   jnp.concatenate([gqn[0], gkn[0], gbif[0]]), dc_part[N_DEV], width=PW)
    tot = _sum_slots(_all_gather(small, "ag_small", False), "sum_small")
    g_ln_w, g_ln_b, g_mh, g_conv_b = tot[0, :D], tot[1, :D], tot[2, :dm.V], tot[3, :QK2]
    g_conv_full = tot[4:7, :QK2]
    g_b_mod = jnp.concatenate([tot[7, :D] + tot[10, :D], tot[8, :D] + tot[11, :D], tot[9, :D]])
    g_qn, g_kn, g_bif = tot[12, 0:HD_B], tot[12, HD_B:2 * HD_B], tot[12, 2 * HD_B:2 * HD_B + dm.NIF]
    g_c_ctx = _cctx_grad(tot[13:14, :D], c_ctx.reshape(1, D))[0]
    g_conv_w = lax.dynamic_slice(g_conv_full, (0, me * n_cv), (CONV_W, n_cv))

    g_w_in = _sum_slots(gW_all, "sum_g_w_in")
    g_sq = _sum_slots(gsq_all, "sum_g_w_sq")

    upd_in = [a[None] for a in _adamw(w_in[0], g_w_in, m_w_in[0], v_w_in[0], "adam_w_in")]
    g_w_in, g_w_mod = g_w_in[None], g_w_mod[None]
    g_ba, g_bb, g_out = g_sq[None, 0:rb], g_sq[None, rb:2 * rb], g_sq[None, 2 * rb:3 * rb]
    upd_md = _adamw(w_mod, g_w_mod, m_w_mod, v_w_mod, "adam_w_mod")
    upd_ba = _adamw(w_branch_a, g_ba, m_w_branch_a, v_w_branch_a, "adam_w_ba")
    upd_bb = _adamw(w_branch_b, g_bb, m_w_branch_b, v_w_branch_b, "adam_w_bb")
    upd_out = _adamw(w_out, g_out, m_w_out, v_w_out, "adam_w_out")
    names = ["c_ctx", "b_mod", "b_if", "conv_w", "conv_b", "mh", "qn", "kn", "ln_w", "ln_b"]
    ws = [c_ctx, b_mod, b_if, conv_w, conv_b, mh_norm_w, q_norm_w, k_norm_w, ln_w, ln_b]
    ms = [m_c_ctx, m_b_mod, m_b_if, m_conv_w, m_conv_b, m_mh_norm_w, m_q_norm_w, m_k_norm_w, m_ln_w, m_ln_b]
    vs = [v_c_ctx, v_b_mod, v_b_if, v_conv_w, v_conv_b, v_mh_norm_w, v_q_norm_w, v_k_norm_w, v_ln_w, v_ln_b]
    gs = [g_c_ctx, g_b_mod, g_bif, g_conv_w, g_conv_b, g_mh, g_qn, g_kn, g_ln_w, g_ln_b]
    sizes = [a.size for a in ws]
    tot_n = sum(sizes)
    padn = -(-tot_n // LANE) * LANE
    flat = lambda arrs: jnp.pad(jnp.concatenate([a.reshape(-1) for a in arrs]), (0, padn - tot_n)).reshape(1, padn)
    d_s, nm_s, nv_s = _adamw(flat(ws), flat(gs), flat(ms), flat(vs), "adam_small")

    def split(a):
        res, o = {}, 0
        for n, wv, sz in zip(names, ws, sizes):
            res[n] = a[0, o:o + sz].reshape(wv.shape)
            o += sz
        return res

    def assemble(s, big_in, big_md, big_ba, big_bb, big_out):
        return [s["c_ctx"], big_md, s["b_mod"], big_in, s["b_if"], s["conv_w"], s["conv_b"],
                s["mh"], s["qn"], s["kn"], big_ba, big_bb, big_out, s["ln_w"], s["ln_b"]]

    g_small = {n: g.reshape(wv.shape) for n, g, wv in zip(names, gs, ws)}
    grads = assemble(g_small, g_w_in, g_w_mod, g_ba, g_bb, g_out)
    deltas, new_m, new_v = [
        assemble(split(sm), upd_in[t], upd_md[t], upd_ba[t], upd_bb[t], upd_out[t])
        for t, sm in enumerate((d_s, nm_s, nv_s))]
    return (loss, grad_x[None], *grads, *deltas, *new_m, *new_v)
```

```python
import jax
import jax.numpy as jnp
from jax import lax
from jax.experimental import pallas as pl
from jax.experimental.pallas import tpu as pltpu

F32 = jnp.float32
BF16 = jnp.bfloat16
MESH = pl.DeviceIdType.MESH
N_DEV = 8

GRID_W = 64
NH_A = 8
DK_A = 128
DV_A = 256
CONV_W = 3
CHUNK = 64
M_INIT = -1e30
NH_B = 16
NKV_B = 4
HD_B = 128
ROPE_THETA = 10000.0
EPS = 1e-6
DEPTH = 1
ALPHA = (2 * DEPTH) ** 0.25
ADAM_LR = 0.001
ADAM_B1 = 0.9
ADAM_B2 = 0.999
ADAM_EPS = 1e-08
ADAM_WD = 0.01
ADAM_STEP = 10

LANE = 128
SUBLANE = 8
VMEM_LIMIT = 56 << 20


def _tile(n, target, align):
    best = None
    t = align
    while t <= min(n, target):
        if n % t == 0:
            best = t
        t += align
    return best if best is not None else n


class Dims:
    def __init__(self, S, Tc, D):
        self.S, self.Tc, self.D = S, Tc, D
        self.T = S + Tc
        self.QK = NH_A * DK_A
        self.V = NH_A * DV_A
        self.QB = NH_B * HD_B
        self.KVB = NKV_B * HD_B
        self.G = NH_B // NKV_B
        self.NIF = 4 * NH_A
        self.IFP = 512 if self.KVB % 512 == 0 else LANE
        self.ref_widths = [2 * self.QK, self.V, self.NIF, self.KVB, self.KVB,
                           self.V, self.V, self.QB, self.QB, 2 * D]
        self.ref_names = ["qk", "va", "if", "kb", "vb", "oa", "za", "qb", "zb", "g"]
        self.N_IN = sum(self.ref_widths)
        self.order = ["g", "qk", "va", "oa", "za", "qb", "zb", "kb", "vb", "if"]
        w = dict(zip(self.ref_names, self.ref_widths))
        w["if"] = self.IFP
        self.w = w
        self.off = {}
        o = 0
        for n in self.order:
            assert o % w[n] == 0, (n, o, w[n])
            self.off[n] = o
            o += w[n]
        self.NP = o
        self.TB = min(256, Tc)
        assert Tc % self.TB == 0 and S % self.TB == 0 and self.TB % CHUNK == 0
        self.nctx = Tc // self.TB
        self.nlat = S // self.TB
        self.nblk = self.nctx + self.nlat
        self.PW = max(D, self.V, 2 * self.QK, 3 * LANE)


def _cparams(sem):
    return pltpu.CompilerParams(dimension_semantics=sem, vmem_limit_bytes=VMEM_LIMIT)


def _sigmoid(x):
    return 1.0 / (1.0 + jnp.exp(-x))


def _my_pos():
    return lax.axis_index("x"), lax.axis_index("y"), lax.axis_index("c")


def _all_gather(x, name, big):
    R, C = x.shape
    space = pl.ANY if big else pltpu.VMEM

    def body(x_ref, out_ref, send_sems, recv_sems, local_sem):
        px, py, pc = _my_pos()
        me, sibling = (px, py, pc), (px, py, 1 - pc)
        chips = [(1 - px, py), (px, 1 - py), (1 - px, 1 - py)]

        def slot(bx, by, bc):
            return out_ref.at[4 * bx + 2 * by + bc]

        def copy(k, block, to, src=None):
            return pltpu.make_async_remote_copy(
                src_ref=slot(*block) if src is None else src, dst_ref=slot(*block),
                send_sem=send_sems.at[k], recv_sem=recv_sems.at[k],
                device_id=to, device_id_type=MESH)

        mine = pltpu.make_async_copy(x_ref, slot(*me), local_sem)
        mine.start()
        first = [copy(0, me, sibling, src=x_ref)]
        first += [copy(1 + j, me, (*chip, pc), src=x_ref) for j, chip in enumerate(chips)]
        for cp in first:
            cp.start()
        passed = [copy(4 + j, (*chip, pc), sibling) for j, chip in enumerate(chips)]
        for j, chip in enumerate(chips):
            copy(1 + j, (*chip, pc), me).wait_recv()
            passed[j].start()
        copy(0, sibling, me).wait_recv()
        for j, chip in enumerate(chips):
            copy(4 + j, (*chip, 1 - pc), me).wait_recv()
        for cp in first + passed:
            cp.wait_send()
        mine.wait()

    return pl.pallas_call(
        body, name=name,
        out_shape=jax.ShapeDtypeStruct((N_DEV, R, C), x.dtype),
        in_specs=[pl.BlockSpec(memory_space=space)],
        out_specs=pl.BlockSpec(memory_space=space),
        scratch_shapes=[pltpu.SemaphoreType.DMA((7,)), pltpu.SemaphoreType.DMA((7,)),
                        pltpu.SemaphoreType.DMA],
    )(x)


EXCHANGE_SEMS = [pltpu.SemaphoreType.DMA((N_DEV - 1,)), pltpu.SemaphoreType.DMA((N_DEV - 1,)),
                 pltpu.SemaphoreType.DMA]


def _exchange(kind, src_ref, land_ref, send_sems, recv_sems, local_sem):
    def copies():
        px, py, pc = _my_pos()
        me = 4 * px + 2 * py + pc
        own = src_ref if kind == "ag" else src_ref.at[me]
        local = pltpu.make_async_copy(own, land_ref.at[me], local_sem)
        sends, recvs = [], []
        for r in range(1, N_DEV):
            dx, dy, dc = (r >> 2) & 1, (r >> 1) & 1, r & 1
            qx = px if dx == 0 else 1 - px
            qy = py if dy == 0 else 1 - py
            qc = pc if dc == 0 else 1 - pc
            peer = 4 * qx + 2 * qy + qc
            sems = dict(send_sem=send_sems.at[r - 1], recv_sem=recv_sems.at[r - 1],
                        device_id=(qx, qy, qc), device_id_type=MESH)
            sends.append(pltpu.make_async_remote_copy(
                src_ref=src_ref if kind == "ag" else src_ref.at[peer], dst_ref=land_ref.at[me], **sems))
            recvs.append(pltpu.make_async_remote_copy(src_ref=own, dst_ref=land_ref.at[peer], **sems))
        return local, sends, recvs

    def start():
        local, sends, _ = copies()
        local.start()
        for cp in sends:
            cp.start()

    def wait():
        local, sends, recvs = copies()
        for cp in recvs:
            cp.wait_recv()
        for cp in sends:
            cp.wait_send()
        local.wait()

    return start, wait


def _land_shape(kind, src):
    return jax.ShapeDtypeStruct(src.shape if kind == "a2a" else (N_DEV,) + src.shape, src.dtype)


def _sum_slots(a, name):
    _, R, C = a.shape
    tb = _tile(R, max(SUBLANE, (1 << 20) // (4 * C) // SUBLANE * SUBLANE), SUBLANE)

    def kern(a_ref, o_ref):
        acc = a_ref[0].astype(F32)
        for j in range(1, N_DEV):
            acc = acc + a_ref[j].astype(F32)
        o_ref[...] = acc

    return pl.pallas_call(
        kern, name=name, grid=(R // tb,),
        in_specs=[pl.BlockSpec((N_DEV, tb, C), lambda i: (0, i, 0))],
        out_specs=pl.BlockSpec((tb, C), lambda i: (i, 0)),
        out_shape=jax.ShapeDtypeStruct((R, C), F32),
        compiler_params=_cparams(("parallel",)),
    )(a)


def _mm(a, b, name, nt=False, tm=768, tn=1024, tk=2048, out_dtype=F32, exchange=None):
    M, K = a.shape
    N = b.shape[0] if nt else b.shape[1]
    assert (b.shape[1] if nt else b.shape[0]) == K
    tm, tn, tk = _tile(M, tm, 16), _tile(N, tn, LANE), _tile(K, tk, LANE)
    ni, nj, nk = M // tm, N // tn, K // tk

    def dot(x, y):
        if nt:
            return lax.dot_general(x, y, (((1,), (1,)), ((), ())), preferred_element_type=F32)
        return jnp.dot(x, y, preferred_element_type=F32)

    def kern(a_ref, b_ref, *rest):
        if exchange is not None:
            src_ref, o_ref, land_ref, acc_ref, send_sems, recv_sems, local_sem = rest
            start, wait = _exchange(exchange[0], src_ref, land_ref, send_sems, recv_sems, local_sem)
            i, j, kk = pl.program_id(0), pl.program_id(1), pl.program_id(2)
            pl.when((i == 0) & (j == 0) & (kk == 0))(start)
        else:
            o_ref, acc_ref = rest
        k = pl.program_id(2)
        part = dot(a_ref[...], b_ref[...])
        if nk == 1:
            o_ref[...] = part.astype(o_ref.dtype)
        else:
            @pl.when(k == 0)
            def _():
                acc_ref[...] = part

            @pl.when(k > 0)
            def _():
                acc_ref[...] += part

            @pl.when(k == nk - 1)
            def _():
                o_ref[...] = acc_ref[...].astype(o_ref.dtype)

        if exchange is not None:
            pl.when((i == ni - 1) & (j == nj - 1) & (kk == nk - 1))(wait)

    b_spec = (pl.BlockSpec((tn, tk), lambda i, j, k: (j, k)) if nt
              else pl.BlockSpec((tk, tn), lambda i, j, k: (k, j)))
    in_specs = [pl.BlockSpec((tm, tk), lambda i, j, k: (i, k)), b_spec]
    out_specs = pl.BlockSpec((tm, tn), lambda i, j, k: (i, j))
    out_shape = jax.ShapeDtypeStruct((M, N), out_dtype)
    scratch = [pltpu.VMEM((tm, tn) if nk > 1 else (SUBLANE, LANE), F32)]
    args = (a, b)
    sem = ("parallel", "parallel", "arbitrary")
    if exchange is not None:
        in_specs = in_specs + [pl.BlockSpec(memory_space=pl.ANY)]
        out_specs = [out_specs, pl.BlockSpec(memory_space=pl.ANY)]
        out_shape = [out_shape, _land_shape(*exchange)]
        scratch = scratch + EXCHANGE_SEMS
        args = (a, b, exchange[1])
        sem = ("arbitrary", "arbitrary", "arbitrary")
    return pl.pallas_call(
        kern, name=name, grid=(ni, nj, nk), in_specs=in_specs, out_specs=out_specs,
        out_shape=out_shape, scratch_shapes=scratch, compiler_params=_cparams(sem),
    )(*args)


def _ln_stats(x):
    mu = jnp.mean(x, axis=-1, keepdims=True)
    xc = x - mu
    var = jnp.mean(xc * xc, axis=-1, keepdims=True)
    rstd = lax.rsqrt(var + EPS)
    return xc * rstd, rstd


def _row_specs(dm):
    TB, D, nctx = dm.TB, dm.D, dm.nctx
    return [pl.BlockSpec((TB, D), lambda i: (jnp.minimum(i, nctx - 1), 0)),
            pl.BlockSpec((TB, D), lambda i: (jnp.maximum(i - nctx, 0), 0))]


def _ln_mod_fwd(dm, ctx2, x2, modv):
    TB, D, nctx = dm.TB, dm.D, dm.nctx

    def kern(c_ref, x_ref, m_ref, u_ref, ut_ref):
        is_ctx = pl.program_id(0) < nctx
        xh, _ = _ln_stats(jnp.where(is_ctx, c_ref[...], x_ref[...]))
        shift = jnp.where(is_ctx, m_ref[2:3, :], m_ref[0:1, :])
        scale = jnp.where(is_ctx, m_ref[3:4, :], m_ref[1:2, :])
        u = xh * (1.0 + scale) + shift
        u_ref[...] = u.astype(BF16)
        ut_ref[...] = u.T.astype(BF16)

    return pl.pallas_call(
        kern, name="ln_mod_fwd", grid=(dm.nblk,),
        in_specs=_row_specs(dm) + [pl.BlockSpec((SUBLANE, D), lambda i: (0, 0))],
        out_specs=[pl.BlockSpec((TB, D), lambda i: (i, 0)), pl.BlockSpec((D, TB), lambda i: (0, i))],
        out_shape=[jax.ShapeDtypeStruct((dm.T, D), BF16), jax.ShapeDtypeStruct((D, dm.T), BF16)],
        compiler_params=_cparams(("parallel",)),
    )(ctx2, x2, modv)


P_HALO = 16


def _conv_specs(dm, W, col, hr=SUBLANE):
    TB, T = dm.TB, dm.T
    per = TB // hr
    last = T // hr - 1
    return [pl.BlockSpec((TB, W), lambda i: (i, col)),
            pl.BlockSpec((hr, W), lambda i: (jnp.maximum(i * per - 1, 0), col)),
            pl.BlockSpec((hr, W), lambda i: (jnp.minimum((i + 1) * per, last), col))]


def _shifted(dm, x, prev_ref, next_ref):
    TB, nctx, nblk = dm.TB, dm.nctx, dm.nblk
    i = pl.program_id(0)
    row = lax.broadcasted_iota(jnp.int32, x.shape, 0)
    zero_prev = (i == 0) | (i == nctx)
    zero_next = (i == nctx - 1) | (i == nblk - 1)
    pv, nv = prev_ref[...].astype(F32), next_ref[...].astype(F32)
    hrow = lax.broadcasted_iota(jnp.int32, pv.shape, 0)
    last_prev = jnp.sum(jnp.where(hrow == pv.shape[0] - 1, pv, 0.0), axis=0, keepdims=True)
    first_next = jnp.sum(jnp.where(hrow == 0, nv, 0.0), axis=0, keepdims=True)
    before = jnp.where(zero_prev, 0.0, last_prev)
    after = jnp.where(zero_next, 0.0, first_next)
    xp = jnp.where(row == 0, before, pltpu.roll(x, 1, 0))
    xn = jnp.where(row == TB - 1, after, pltpu.roll(x, TB - 1, 0))
    return xp, xn


def _conv_silu_fwd(dm, P, convp):
    W = 2 * dm.QK
    col = dm.off["qk"] // W
    kscale = DK_A ** -0.5

    def kern(x_ref, p_ref, n_ref, c_ref, o_ref):
        x = x_ref[...].astype(F32)
        xp, xn = _shifted(dm, x, p_ref, n_ref)
        z = c_ref[3:4, :] + xp * c_ref[0:1, :] + x * c_ref[1:2, :] + xn * c_ref[2:3, :]
        lane = lax.broadcasted_iota(jnp.int32, (1, W), 1)
        cs = jnp.where(lane >= dm.QK, kscale, 1.0)
        o_ref[...] = z * _sigmoid(z) * cs

    return pl.pallas_call(
        kern, name="conv_silu_fwd", grid=(dm.nblk,),
        in_specs=_conv_specs(dm, W, col, P_HALO) + [pl.BlockSpec((SUBLANE, W), lambda i: (0, 0))],
        out_specs=pl.BlockSpec((dm.TB, W), lambda i: (i, 0)),
        out_shape=jax.ShapeDtypeStruct((dm.T, W), F32),
        compiler_params=_cparams(("parallel",)),
    )(P, P, P, convp)


def _gates_fwd(dm, P, bif):
    col = 0

    def kern(x_ref, b_ref, o_ref):
        x = x_ref[...] + b_ref[0:1, :]
        lane = lax.broadcasted_iota(jnp.int32, x.shape, 1)
        is_f = ((lane // NH_A) % 2) == 1
        ls = jnp.minimum(x, 0.0) - jnp.log(1.0 + jnp.exp(-jnp.abs(x)))
        o_ref[...] = jnp.where(lane < dm.NIF, jnp.where(is_f, ls, x), 0.0)

    return pl.pallas_call(
        kern, name="gates_fwd", grid=(dm.nblk,),
        in_specs=[pl.BlockSpec((dm.TB, LANE), lambda i: (i, col)),
                  pl.BlockSpec((SUBLANE, LANE), lambda i: (0, 0))],
        out_specs=pl.BlockSpec((dm.TB, LANE), lambda i: (i, 0)),
        out_shape=jax.ShapeDtypeStruct((dm.T, LANE), F32),
        compiler_params=_cparams(("parallel",)),
    )(P, bif)


def _mlstm_order(dm, reverse, backward):
    nctx, nblk = dm.nctx, dm.nblk

    def idx(i):
        if backward:
            i = nblk - 1 - i
        if not reverse:
            return i
        return jnp.where(i < nctx, nctx - 1 - i, nblk - 1 - (i - nctx))

    return idx


def _chunk_gates(g, ci, cf, mask_f, maskT_f, eye_f):
    lane = lax.broadcasted_iota(jnp.int32, g.shape, 1)
    gi_c = jnp.sum(jnp.where(lane == ci, g, 0.0), axis=1, keepdims=True)
    gf_c = jnp.sum(jnp.where(lane == cf, g, 0.0), axis=1, keepdims=True)
    gi_r = jnp.sum(eye_f * gi_c, axis=0, keepdims=True)
    gf_r = jnp.sum(eye_f * gf_c, axis=0, keepdims=True)
    b_c = jnp.sum(mask_f * gf_r, axis=1, keepdims=True)
    b_r = jnp.sum(maskT_f * gf_c, axis=0, keepdims=True)
    return gi_c, gi_r, b_c, b_r


def _chunk_masks(reverse):
    L = CHUNK
    r = lax.broadcasted_iota(jnp.int32, (L, L), 0)
    c = lax.broadcasted_iota(jnp.int32, (L, L), 1)
    mask = (c >= r) if reverse else (c <= r)
    maskT = (r >= c) if reverse else (r <= c)
    return mask, mask.astype(F32), maskT.astype(F32), (r == c).astype(F32)


def _pick_row(x, e):
    r = lax.broadcasted_iota(jnp.int32, x.shape, 0)
    return jnp.sum(jnp.where(r == e, x, 0.0), axis=0, keepdims=True)


def _dot_nt(a, b):
    return lax.dot_general(a, b, (((1,), (1,)), ((), ())), preferred_element_type=F32)


def _dot(a, b):
    return jnp.dot(a, b, preferred_element_type=F32)


def _chunk_fwd_core(q, k, g, ci, cf, C0, n0, m0, masks, reverse):
    mask, mask_f, maskT_f, eye_f = masks
    gi_c, gi_r, b_c, b_r = _chunk_gates(g, ci, cf, mask_f, maskT_f, eye_f)
    d = jnp.where(mask, b_c - b_r + gi_r, -jnp.inf)
    m_c = jnp.maximum(b_c + m0, jnp.max(d, axis=1, keepdims=True))
    w = jnp.exp(d - m_c)
    a_c = jnp.exp(b_c + m0 - m_c)
    qb, kb = q.astype(BF16), k.astype(BF16)
    s = _dot_nt(qb, kb) * w
    den = a_c * jnp.sum(q * n0, axis=1, keepdims=True) + jnp.sum(s, axis=1, keepdims=True)
    e = 0 if reverse else CHUNK - 1
    m_end, b_end, a_end = _pick_row(m_c, e), _pick_row(b_c, e), _pick_row(a_c, e)
    w_end = jnp.exp(b_end - b_c + gi_c - m_end)
    return qb, kb, s, w, a_c, m_c, den, w_end, a_end, m_end


MLSTM_HEADS_PER_STEP = 4


def _heads_per_step():
    return MLSTM_HEADS_PER_STEP if NH_A % MLSTM_HEADS_PER_STEP == 0 else 1


def _mlstm_fwd(dm, qk_act, P, G, reverse):
    TB, T = dm.TB, dm.T
    NC = TB // CHUNK
    idx = _mlstm_order(dm, reverse, False)
    vcol = dm.off["va"] // DV_A
    base = 2 * NH_A if reverse else 0

    HP = _heads_per_step()
    hcols = lambda hh, w: slice(hh * w, (hh + 1) * w)

    def kern(q_ref, k_ref, v_ref, g_ref, h_ref, cst_ref, nm_ref, C_s, N_s):
        i, hp = pl.program_id(0), pl.program_id(1)
        row8 = lax.broadcasted_iota(jnp.int32, (SUBLANE, DK_A), 0)
        heads = [hp * HP + hh for hh in range(HP)]

        @pl.when(i == 0)
        def _():
            for h in heads:
                C_s[h] = jnp.zeros((DK_A, DV_A), F32)
                N_s[h] = jnp.where(row8 == 1, M_INIT, 0.0)

        masks = _chunk_masks(reverse)
        state = [(C_s[h], N_s.at[h][0:1, :], N_s.at[h][1:2, 0:1]) for h in heads]
        for c in (range(NC - 1, -1, -1) if reverse else range(NC)):
            rows = pl.ds(c * CHUNK, CHUNK)
            g = g_ref[rows, :]
            for hh, h in enumerate(heads):
                C0, n0, m0 = state[hh]
                q, k = q_ref[rows, hcols(hh, DK_A)], k_ref[rows, hcols(hh, DK_A)]
                v = v_ref[rows, hcols(hh, DV_A)].astype(F32)
                cst_ref[hh, c] = C0
                nm_ref[hh, c] = jnp.where(row8 == 0, n0, jnp.where(row8 == 1, m0, 0.0))
                qb, kb, s, w, a_c, m_c, den, w_end, a_end, m_end = _chunk_fwd_core(
                    q, k, g, base + h, base + NH_A + h, C0, n0, m0, masks, reverse)
                vb = v.astype(BF16)
                num = a_c * _dot(qb, C0.astype(BF16)) + _dot(s.astype(BF16), vb)
                h_ref[rows, hcols(hh, DV_A)] = num / jnp.maximum(jnp.abs(den), jnp.exp(-m_c))
                state[hh] = (a_end * C0 + _dot(k.T.astype(BF16), (w_end * v).astype(BF16)),
                             a_end * n0 + jnp.sum(w_end * k, axis=0, keepdims=True), m_end)
        for hh, h in enumerate(heads):
            C0, n0, m0 = state[hh]
            C_s[h] = C0
            N_s[h] = jnp.where(row8 == 0, n0, jnp.where(row8 == 1, m0, 0.0))

    nch = T // CHUNK
    NG = NH_A // HP
    return pl.pallas_call(
        kern, name="mlstm_fwd_rev" if reverse else "mlstm_fwd", grid=(dm.nblk, NG),
        in_specs=[pl.BlockSpec((TB, HP * DK_A), lambda i, h: (idx(i), h)),
                  pl.BlockSpec((TB, HP * DK_A), lambda i, h: (idx(i), NG + h)),
                  pl.BlockSpec((TB, HP * DV_A), lambda i, h: (idx(i), vcol // HP + h)),
                  pl.BlockSpec((TB, LANE), lambda i, h: (idx(i), 0))],
        out_specs=[pl.BlockSpec((TB, HP * DV_A), lambda i, h: (idx(i), h)),
                   pl.BlockSpec((HP, NC, DK_A, DV_A), lambda i, h: (h, idx(i), 0, 0)),
                   pl.BlockSpec((HP, NC, SUBLANE, DK_A), lambda i, h: (h, idx(i), 0, 0))],
        out_shape=[jax.ShapeDtypeStruct((T, dm.V), F32),
                   jax.ShapeDtypeStruct((NH_A, nch, DK_A, DV_A), F32),
                   jax.ShapeDtypeStruct((NH_A, nch, SUBLANE, DK_A), F32)],
        scratch_shapes=[pltpu.VMEM((NH_A, DK_A, DV_A), F32), pltpu.VMEM((NH_A, SUBLANE, DK_A), F32)],
        compiler_params=_cparams(("arbitrary", "arbitrary")),
    )(qk_act, qk_act, P, G)


def _mlstm_bwd(dm, qk_act, P, G, Cst, NM, H, dH, acc, reverse):
    TB, T = dm.TB, dm.T
    NC = TB // CHUNK
    idx = _mlstm_order(dm, reverse, True)
    vcol = dm.off["va"] // DV_A
    base = 2 * NH_A if reverse else 0
    has_acc = acc is not None
    HP = _heads_per_step()
    hcols = lambda hh, w: slice(hh * w, (hh + 1) * w)

    def kern(*refs):
        (q_ref, k_ref, v_ref, g_ref, cst_ref, nm_ref, hh_ref, dh_ref) = refs[:8]
        p = 8
        if has_acc:
            aq_ref, ak_ref, av_ref, ag_ref = refs[p:p + 4]
            p += 4
        dq_ref, dk_ref, dv_ref, dg_ref, R_s, Rn_s = refs[p:p + 6]
        i, hp = pl.program_id(0), pl.program_id(1)
        row8 = lax.broadcasted_iota(jnp.int32, (SUBLANE, DK_A), 0)
        heads = [hp * HP + hh for hh in range(HP)]

        @pl.when(i == 0)
        def _():
            for h in heads:
                R_s[h] = jnp.zeros((DK_A, DV_A), F32)
                Rn_s[h] = jnp.zeros((SUBLANE, DK_A), F32)

        @pl.when(hp == 0)
        def _():
            dg_ref[...] = ag_ref[...] if has_acc else jnp.zeros((TB, LANE), F32)

        masks = _chunk_masks(reverse)
        _, mask_f, maskT_f, eye_f = masks
        before_f = mask_f - eye_f
        state = [(R_s[h], Rn_s.at[h][0:1, :]) for h in heads]
        lane = lax.broadcasted_iota(jnp.int32, (CHUNK, LANE), 1)

        def as_row(col):
            return jnp.sum(eye_f * col, axis=0, keepdims=True)

        for c in (range(NC) if reverse else range(NC - 1, -1, -1)):
            rows = pl.ds(c * CHUNK, CHUNK)
            g = g_ref[rows, :]
            dg = jnp.zeros((CHUNK, LANE), F32)
            for hh, h in enumerate(heads):
                R, Rn = state[hh]
                qc, vc = hcols(hh, DK_A), hcols(hh, DV_A)
                q, k, v = q_ref[rows, qc], k_ref[rows, qc], v_ref[rows, vc].astype(F32)
                C0 = cst_ref[hh, c]
                n0, m0 = nm_ref.at[hh, c][0:1, :], nm_ref.at[hh, c][1:2, 0:1]
                qb, kb, s, w, a_c, m_c, den, w_end, a_end, _ = _chunk_fwd_core(
                    q, k, g, base + h, base + NH_A + h, C0, n0, m0, masks, reverse)
                vb = v.astype(BF16)
                e_m = jnp.exp(-m_c)
                r = 1.0 / jnp.maximum(jnp.abs(den), e_m)
                dh = dh_ref[rows, vc]
                dN = dh * r
                dD = jnp.where(jnp.abs(den) > e_m,
                               -jnp.sum(dh * hh_ref[rows, vc], axis=1, keepdims=True) * r * jnp.sign(den), 0.0)
                dNb = dN.astype(BF16)
                dS = _dot_nt(dNb, vb) + dD
                dqk = dS * w
                Cb, Rb = C0.astype(BF16), R.astype(BF16)
                dq_in = a_c * (_dot_nt(dNb, Cb) + dD * n0)
                dk_out = w_end * (_dot_nt(vb, Rb) + Rn)
                dq = _dot(dqk.astype(BF16), kb) + dq_in
                dk = _dot(dqk.T.astype(BF16), qb) + dk_out
                dv = _dot(s.T.astype(BF16), dNb) + w_end * _dot(kb, Rb)
                if has_acc:
                    dq_ref[rows, qc] = aq_ref[rows, qc] + dq
                    dk_ref[rows, qc] = ak_ref[rows, qc] + dk
                    dv_ref[rows, vc] = av_ref[rows, vc] + dv
                else:
                    dq_ref[rows, qc] = dq
                    dk_ref[rows, qc] = dk
                    dv_ref[rows, vc] = dv
                gm = dS * s
                g_row = jnp.sum(gm, axis=1, keepdims=True)
                g_col = jnp.sum(eye_f * jnp.sum(gm, axis=0, keepdims=True), axis=1, keepdims=True)
                q_in = jnp.sum(q * dq_in, axis=1, keepdims=True)
                k_out = jnp.sum(k * dk_out, axis=1, keepdims=True)
                through = a_end * (jnp.sum(jnp.sum(R * C0, axis=1, keepdims=True), axis=0, keepdims=True)
                                   + jnp.sum(Rn * n0, axis=1, keepdims=True))
                di = g_col + k_out
                df = (jnp.sum(maskT_f * as_row(g_row - g_col + q_in), axis=1, keepdims=True)
                      + jnp.sum(before_f * as_row(k_out), axis=1, keepdims=True) + through)
                dg = dg + jnp.where(lane == base + h, di, 0.0) + jnp.where(lane == base + NH_A + h, df, 0.0)
                aq = a_c * q
                state[hh] = (a_end * R + _dot(aq.T.astype(BF16), dNb),
                             a_end * Rn + jnp.sum(aq * dD, axis=0, keepdims=True))
            dg_ref[rows, :] += dg
        for hh, h in enumerate(heads):
            R, Rn = state[hh]
            R_s[h] = R
            Rn_s[h] = jnp.where(row8 == 0, Rn, 0.0)

    NG = NH_A // HP
    qspec = pl.BlockSpec((TB, HP * DK_A), lambda i, h: (idx(i), h))
    vspec = pl.BlockSpec((TB, HP * DV_A), lambda i, h: (idx(i), h))
    gspec = pl.BlockSpec((TB, LANE), lambda i, h: (idx(i), 0))
    in_specs = [qspec,
                pl.BlockSpec((TB, HP * DK_A), lambda i, h: (idx(i), NG + h)),
                pl.BlockSpec((TB, HP * DV_A), lambda i, h: (idx(i), vcol // HP + h)),
                gspec,
                pl.BlockSpec((HP, NC, DK_A, DV_A), lambda i, h: (h, idx(i), 0, 0)),
                pl.BlockSpec((HP, NC, SUBLANE, DK_A), lambda i, h: (h, idx(i), 0, 0)),
                vspec, vspec]
    args = [qk_act, qk_act, P, G, Cst, NM, H, dH]
    if has_acc:
        in_specs += [qspec, qspec, vspec, gspec]
        args += list(acc)
    return pl.pallas_call(
        kern, name="mlstm_bwd_rev" if reverse else "mlstm_bwd", grid=(dm.nblk, NG),
        in_specs=in_specs,
        out_specs=[qspec, qspec, vspec, gspec],
        out_shape=[jax.ShapeDtypeStruct((T, dm.QK), F32), jax.ShapeDtypeStruct((T, dm.QK), F32),
                   jax.ShapeDtypeStruct((T, dm.V), F32), jax.ShapeDtypeStruct((T, LANE), F32)],
        scratch_shapes=[pltpu.VMEM((NH_A, DK_A, DV_A), F32), pltpu.VMEM((NH_A, SUBLANE, DK_A), F32)],
        compiler_params=_cparams(("arbitrary", "arbitrary")),
    )(*args)


def _rms_heads(x, w_row, nh, hd):
    out = []
    for h in range(nh):
        xh = x[:, h * hd:(h + 1) * hd]
        rstd = lax.rsqrt(jnp.mean(xh * xh, axis=1, keepdims=True) + EPS)
        out.append((xh * rstd, rstd))
    return out


def _rope(x, cos, sa, sb):
    return x * cos + pltpu.roll(x, HD_B - HD_B // 4, 1) * sa + pltpu.roll(x, HD_B // 4, 1) * sb


def _rope_t(dy, cos, sa, sb):
    return dy * cos + pltpu.roll(dy * sa, HD_B // 4, 1) + pltpu.roll(dy * sb, HD_B - HD_B // 4, 1)


ATT_SCALE = HD_B ** -0.5
LOG2E = 1.4426950408889634
LN2 = 0.6931471805599453
QSCALE = ATT_SCALE * LOG2E


def _qk_prep(dm, P, rope, qn, kn):
    TB, nctx = dm.TB, dm.nctx
    qcol, kcol, vcol = dm.off["qb"] // dm.QB, dm.off["kb"] // dm.KVB, dm.off["vb"] // dm.KVB

    def kern_q(x_ref, t_ref, w_ref, o_ref):
        cos, sa, sb = t_ref[0], t_ref[1], t_ref[2]
        for h, (xn, _) in enumerate(_rms_heads(x_ref[...].astype(F32), None, NH_B, HD_B)):
            o_ref[h] = (_rope(xn * w_ref[0:1, :], cos, sa, sb) * QSCALE).astype(BF16)

    Qr = pl.pallas_call(
        kern_q, name="q_prep", grid=(dm.nlat,),
        in_specs=[pl.BlockSpec((TB, dm.QB), lambda i: (i + nctx, qcol)),
                  pl.BlockSpec((3, TB, HD_B), lambda i: (0, i + nctx, 0)),
                  pl.BlockSpec((SUBLANE, HD_B), lambda i: (0, 0))],
        out_specs=pl.BlockSpec((NH_B, TB, HD_B), lambda i: (0, i, 0)),
        out_shape=jax.ShapeDtypeStruct((NH_B, dm.S, HD_B), BF16),
        compiler_params=_cparams(("parallel",)),
    )(P, rope, qn)

    def kern_k(x_ref, v_ref, t_ref, w_ref, o_ref, vo_ref):
        cos, sa, sb = t_ref[0], t_ref[1], t_ref[2]
        for h, (xn, _) in enumerate(_rms_heads(x_ref[...].astype(F32), None, NKV_B, HD_B)):
            o_ref[:, h * HD_B:(h + 1) * HD_B] = _rope(xn * w_ref[0:1, :], cos, sa, sb).astype(BF16)
        vo_ref[...] = v_ref[...].astype(BF16)

    Kr, Vb = pl.pallas_call(
        kern_k, name="k_prep", grid=(dm.nblk,),
        in_specs=[pl.BlockSpec((TB, dm.KVB), lambda i: (i, kcol)),
                  pl.BlockSpec((TB, dm.KVB), lambda i: (i, vcol)),
                  pl.BlockSpec((3, TB, HD_B), lambda i: (0, i, 0)),
                  pl.BlockSpec((SUBLANE, HD_B), lambda i: (0, 0))],
        out_specs=[pl.BlockSpec((TB, dm.KVB), lambda i: (i, 0))] * 2,
        out_shape=[jax.ShapeDtypeStruct((dm.T, dm.KVB), BF16)] * 2,
        compiler_params=_cparams(("parallel",)),
    )(P, P, rope, kn)
    return Qr, Kr, Vb


def _attn_tiles(dm):
    return _tile(dm.S, 512, LANE), _tile(dm.T, 768, LANE)


def _attn_fwd(dm, Qr, Kr, Vb):
    S, T, G = dm.S, dm.T, dm.G
    tq, tk = _tile(S, 256, LANE), T
    nk = T // tk

    def kern(q_ref, k_ref, v_ref, o_ref, l_ref, m_s, l_s, a_s):
        j = pl.program_id(2)
        k, v = k_ref[...], v_ref[...]
        if nk == 1:
            for h in range(G):
                s = _dot_nt(q_ref[h], k)
                m = jnp.max(s, axis=1, keepdims=True)
                p = jnp.exp2(s - m)
                l = jnp.sum(p, axis=1, keepdims=True)
                o_ref[h] = _dot(p.astype(BF16), v) / l
                l_ref[0, :, h:h + 1] = m + jnp.log(l) * LOG2E
            return

        @pl.when(j == 0)
        def _():
            m_s[...] = jnp.full(m_s.shape, -jnp.inf, F32)
            l_s[...] = jnp.zeros(l_s.shape, F32)
            a_s[...] = jnp.zeros(a_s.shape, F32)

        for h in range(G):
            s = _dot_nt(q_ref[h], k)
            m_old = m_s[h]
            m_new = jnp.maximum(m_old, jnp.max(s, axis=1, keepdims=True))
            p = jnp.exp2(s - m_new)
            corr = jnp.exp2(m_old - m_new)
            l_s[h] = corr * l_s[h] + jnp.sum(p, axis=1, keepdims=True)
            m_s[h] = m_new
            a_s[h] = corr * a_s[h] + _dot(p.astype(BF16), v)

        @pl.when(j == nk - 1)
        def _():
            for h in range(G):
                o_ref[h] = a_s[h] / l_s[h]
                l_ref[0, :, h:h + 1] = m_s[h] + jnp.log(l_s[h]) * LOG2E

    qspec = pl.BlockSpec((G, tq, HD_B), lambda g, i, j: (g, i, 0))
    return pl.pallas_call(
        kern, name="attn_fwd", grid=(NKV_B, S // tq, nk),
        in_specs=[qspec,
                  pl.BlockSpec((tk, HD_B), lambda g, i, j: (j, g)),
                  pl.BlockSpec((tk, HD_B), lambda g, i, j: (j, g))],
        out_specs=[qspec, pl.BlockSpec((1, tq, G), lambda g, i, j: (g, i, 0))],
        out_shape=[jax.ShapeDtypeStruct((NH_B, S, HD_B), F32), jax.ShapeDtypeStruct((NKV_B, S, G), F32)],
        scratch_shapes=[pltpu.VMEM((G, tq, 1), F32), pltpu.VMEM((G, tq, 1), F32),
                        pltpu.VMEM((G, tq, HD_B), F32)],
        compiler_params=_cparams(("parallel", "parallel", "arbitrary")),
    )(Qr, Kr, Vb)


def _attn_bwd(dm, Qr, Kr, Vb, dO, LSE_T, DEL_T):
    S, T, G = dm.S, dm.T, dm.G
    tq, tk = _attn_tiles(dm)
    nq, nkt = S // tq, T // tk

    def kern(q_ref, k_ref, v_ref, do_ref, l_ref, d_ref, dq_hbm, dk_ref, dv_ref, ak_s, av_s, dq_s, stage, sem):
        g, j, i = pl.program_id(0), pl.program_id(1), pl.program_id(2)
        rows = pl.ds(pl.multiple_of(i * tq, tq), tq)

        @pl.when(i == 0)
        def _():
            ak_s[...] = jnp.zeros(ak_s.shape, F32)
            av_s[...] = jnp.zeros(av_s.shape, F32)

        @pl.when(j == 0)
        def _():
            dq_s[:, rows, :] = jnp.zeros((G, tq, HD_B), F32)

        k, v = k_ref[...], v_ref[...]
        for h in range(G):
            q, do = q_ref[h], do_ref[h]
            pT = jnp.exp2(_dot_nt(k, q) - l_ref[0, h:h + 1, :])
            dpT = _dot_nt(v, do)
            dsT = (pT * (dpT - d_ref[0, h:h + 1, :])).astype(BF16)
            av_s[...] += _dot(pT.astype(BF16), do)
            ak_s[...] += _dot(dsT, q)
            dq_s[h, rows, :] += lax.dot_general(dsT, k, (((0,), (0,)), ((), ())),
                                                preferred_element_type=F32)

        @pl.when(i == nq - 1)
        def _():
            dk_ref[...] = ak_s[...] * LN2
            dv_ref[...] = av_s[...].astype(BF16)

        @pl.when(j == nkt - 1)
        def _():
            stage[...] = dq_s[:, rows, :] * ATT_SCALE
            out = pltpu.make_async_copy(stage, dq_hbm.at[pl.ds(g * G, G), rows, :], sem)
            out.start()
            out.wait()

    qspec = pl.BlockSpec((G, tq, HD_B), lambda g, j, i: (g, i, 0))
    kspec = pl.BlockSpec((tk, HD_B), lambda g, j, i: (j, g))
    lspec = pl.BlockSpec((1, G, tq), lambda g, j, i: (g, 0, i))
    return pl.pallas_call(
        kern, name="attn_bwd", grid=(NKV_B, nkt, nq),
        in_specs=[qspec, kspec, kspec, qspec, lspec, lspec],
        out_specs=[pl.BlockSpec(memory_space=pl.ANY), kspec, kspec],
        out_shape=[jax.ShapeDtypeStruct((NH_B, S, HD_B), F32), jax.ShapeDtypeStruct((T, dm.KVB), F32),
                   jax.ShapeDtypeStruct((T, dm.KVB), BF16)],
        scratch_shapes=[pltpu.VMEM((tk, HD_B), F32), pltpu.VMEM((tk, HD_B), F32),
                        pltpu.VMEM((G, S, HD_B), F32), pltpu.VMEM((G, tq, HD_B), F32),
                        pltpu.SemaphoreType.DMA],
        compiler_params=_cparams(("arbitrary", "arbitrary", "arbitrary")),
    )(Qr, Kr, Vb, dO, LSE_T, DEL_T)


def _qk_bwd(dm, dQr, dKr, P, rope, qn, kn):
    TB, nctx = dm.TB, dm.nctx
    qcol, kcol = dm.off["qb"] // dm.QB, dm.off["kb"] // dm.KVB

    def head_bwd(dyr, x, w_row, cos, sa, sb):
        rstd = lax.rsqrt(jnp.mean(x * x, axis=1, keepdims=True) + EPS)
        xn = x * rstd
        dy = _rope_t(dyr, cos, sa, sb)
        dw = jnp.sum(dy * xn, axis=0, keepdims=True)
        dxn = dy * w_row
        dx = rstd * (dxn - xn * jnp.mean(dxn * xn, axis=1, keepdims=True))
        return dx, dw

    def make(nh, ctx_zero):
        def kern(d_ref, x_ref, t_ref, w_ref, o_ref, gw_ref):
            i = pl.program_id(0)

            @pl.when(i == 0)
            def _():
                gw_ref[...] = jnp.zeros(gw_ref.shape, F32)

            def live():
                cos, sa, sb = t_ref[0], t_ref[1], t_ref[2]
                tot = jnp.zeros((1, HD_B), F32)
                for h in range(nh):
                    cols = slice(h * HD_B, (h + 1) * HD_B)
                    dyr = d_ref[h] if ctx_zero else d_ref[:, cols]
                    dx, dw = head_bwd(dyr, x_ref[:, cols].astype(F32), w_ref[0:1, :], cos, sa, sb)
                    o_ref[:, cols] = dx.astype(BF16)
                    tot = tot + dw
                gw_ref[0:1, :] += tot

            if ctx_zero:
                @pl.when(i < nctx)
                def _():
                    o_ref[...] = jnp.zeros(o_ref.shape, BF16)

                pl.when(i >= nctx)(live)
            else:
                live()
        return kern

    lat = lambda i: jnp.maximum(i - nctx, 0)
    d_qb, gqn = pl.pallas_call(
        make(NH_B, True), name="q_bwd", grid=(dm.nblk,),
        in_specs=[pl.BlockSpec((NH_B, TB, HD_B), lambda i: (0, lat(i), 0)),
                  pl.BlockSpec((TB, dm.QB), lambda i: (i, qcol)),
                  pl.BlockSpec((3, TB, HD_B), lambda i: (0, i, 0)),
                  pl.BlockSpec((SUBLANE, HD_B), lambda i: (0, 0))],
        out_specs=[pl.BlockSpec((TB, dm.QB), lambda i: (i, 0)), pl.BlockSpec((SUBLANE, HD_B), lambda i: (0, 0))],
        out_shape=[jax.ShapeDtypeStruct((dm.T, dm.QB), BF16), jax.ShapeDtypeStruct((SUBLANE, HD_B), F32)],
        compiler_params=_cparams(("arbitrary",)),
    )(dQr, P, rope, qn)
    d_kb, gkn = pl.pallas_call(
        make(NKV_B, False), name="k_bwd", grid=(dm.nblk,),
        in_specs=[pl.BlockSpec((TB, dm.KVB), lambda i: (i, 0)),
                  pl.BlockSpec((TB, dm.KVB), lambda i: (i, kcol)),
                  pl.BlockSpec((3, TB, HD_B), lambda i: (0, i, 0)),
                  pl.BlockSpec((SUBLANE, HD_B), lambda i: (0, 0))],
        out_specs=[pl.BlockSpec((TB, dm.KVB), lambda i: (i, 0)), pl.BlockSpec((SUBLANE, HD_B), lambda i: (0, 0))],
        out_shape=[jax.ShapeDtypeStruct((dm.T, dm.KVB), BF16), jax.ShapeDtypeStruct((SUBLANE, HD_B), F32)],
        compiler_params=_cparams(("arbitrary",)),
    )(dKr, P, rope, kn)
    return d_qb, d_kb, gqn, gkn


def _merge_prep(dm, Hf, Hb, P, O, mhw):
    TB, nctx = dm.TB, dm.nctx
    lat = lambda c: (lambda i: (i + nctx, c))

    def kern(hf_ref, hb_ref, oa_ref, za_ref, zb_ref, o_ref, w_ref, a_ref, b_ref, at_ref, bt_ref):
        for h in range(NH_A):
            cols = slice(h * DV_A, (h + 1) * DV_A)
            hs = hf_ref[:, cols] + hb_ref[:, cols]
            rstd = lax.rsqrt(jnp.mean(hs * hs, axis=1, keepdims=True) + EPS)
            za = za_ref[:, cols].astype(F32)
            a = _sigmoid(oa_ref[:, cols].astype(F32)) * (hs * rstd * w_ref[0:1, cols]) * (za * _sigmoid(za))
            a_ref[:, cols] = a.astype(BF16)
            at_ref[cols, :] = a.T.astype(BF16)
        for h in range(NH_B):
            cols = slice(h * HD_B, (h + 1) * HD_B)
            zb = zb_ref[:, cols].astype(F32)
            b = o_ref[h] * (zb * _sigmoid(zb))
            b_ref[:, cols] = b.astype(BF16)
            bt_ref[cols, :] = b.T.astype(BF16)

    return pl.pallas_call(
        kern, name="merge_prep", grid=(dm.nlat,),
        in_specs=[pl.BlockSpec((TB, dm.V), lat(0)), pl.BlockSpec((TB, dm.V), lat(0)),
                  pl.BlockSpec((TB, dm.V), lat(dm.off["oa"] // dm.V)),
                  pl.BlockSpec((TB, dm.V), lat(dm.off["za"] // dm.V)),
                  pl.BlockSpec((TB, dm.QB), lat(dm.off["zb"] // dm.QB)),
                  pl.BlockSpec((NH_B, TB, HD_B), lambda i: (0, i, 0)),
                  pl.BlockSpec((SUBLANE, dm.V), lambda i: (0, 0))],
        out_specs=[pl.BlockSpec((TB, dm.V), lambda i: (i, 0)), pl.BlockSpec((TB, dm.QB), lambda i: (i, 0)),
                   pl.BlockSpec((dm.V, TB), lambda i: (0, i)), pl.BlockSpec((dm.QB, TB), lambda i: (0, i))],
        out_shape=[jax.ShapeDtypeStruct((dm.S, dm.V), BF16), jax.ShapeDtypeStruct((dm.S, dm.QB), BF16),
                   jax.ShapeDtypeStruct((dm.V, dm.S), BF16), jax.ShapeDtypeStruct((dm.QB, dm.S), BF16)],
        compiler_params=_cparams(("parallel",)),
    )(Hf, Hb, P, P, P, O, mhw)


def _gate_merge(dm, ya, yb, P):
    TB, D, nctx = dm.TB, dm.D, dm.nctx
    gcol = dm.off["g"] // D

    def kern(ya_ref, yb_ref, ga_ref, gb_ref, o_ref, ot_ref):
        m = (_sigmoid(ga_ref[...].astype(F32)) * ya_ref[...]
             + _sigmoid(gb_ref[...].astype(F32)) * yb_ref[...])
        o_ref[...] = m.astype(BF16)
        ot_ref[...] = m.T.astype(BF16)

    row = pl.BlockSpec((TB, D), lambda i: (i, 0))
    return pl.pallas_call(
        kern, name="gate_merge", grid=(dm.nlat,),
        in_specs=[row, row, pl.BlockSpec((TB, D), lambda i: (i + nctx, gcol)),
                  pl.BlockSpec((TB, D), lambda i: (i + nctx, gcol + 1))],
        out_specs=[row, pl.BlockSpec((D, TB), lambda i: (0, i))],
        out_shape=[jax.ShapeDtypeStruct((dm.S, D), BF16), jax.ShapeDtypeStruct((D, dm.S), BF16)],
        compiler_params=_cparams(("parallel",)),
    )(ya, yb, P, P)


def _final(dm, x, out, tgt, modv, lnp):
    TB, D = dm.TB, dm.D

    def kern(x_ref, o_ref, t_ref, m_ref, p_ref, dr_ref, do_ref, cs_ref, ls_ref):
        i = pl.program_id(0)

        @pl.when(i == 0)
        def _():
            cs_ref[...] = jnp.zeros(cs_ref.shape, F32)
            ls_ref[...] = jnp.zeros(ls_ref.shape, F32)

        gate, lnw, lnb = m_ref[4:5, :], p_ref[0:1, :], p_ref[1:2, :]
        out = o_ref[...]
        xh, rstd = _ln_stats(ALPHA * x_ref[...] + gate * out)
        e = xh * lnw + lnb - t_ref[...]
        ls_ref[...] += 0.5 * jnp.sum(jnp.sum(e * e, axis=1, keepdims=True), axis=0, keepdims=True) / D
        dy = e * (1.0 / D)
        dxh = dy * lnw
        dr = rstd * (dxh - jnp.mean(dxh, axis=1, keepdims=True)
                     - xh * jnp.mean(dxh * xh, axis=1, keepdims=True))
        cs_ref[0:1, :] += jnp.sum(dy * xh, axis=0, keepdims=True)
        cs_ref[1:2, :] += jnp.sum(dy, axis=0, keepdims=True)
        cs_ref[2:3, :] += jnp.sum(dr * out, axis=0, keepdims=True)
        dr_ref[...] = ALPHA * dr
        do_ref[...] = (dr * gate).astype(BF16)

    row = pl.BlockSpec((TB, D), lambda i: (i, 0))
    par = pl.BlockSpec((SUBLANE, D), lambda i: (0, 0))
    return pl.pallas_call(
        kern, name="final_norm_loss", grid=(dm.nlat,),
        in_specs=[row, row, row, par, par],
        out_specs=[row, row, par, pl.BlockSpec((SUBLANE, LANE), lambda i: (0, 0))],
        out_shape=[jax.ShapeDtypeStruct((dm.S, D), F32), jax.ShapeDtypeStruct((dm.S, D), BF16),
                   jax.ShapeDtypeStruct((SUBLANE, D), F32), jax.ShapeDtypeStruct((SUBLANE, LANE), F32)],
        compiler_params=_cparams(("arbitrary",)),
    )(x, out, tgt, modv, lnp)


def _merge_bwd(dm, dM, ya, yb, P):
    TB, D, nctx = dm.TB, dm.D, dm.nctx
    gcol = dm.off["g"] // D
    lat = lambda i: jnp.maximum(i - nctx, 0)

    def kern(dm_ref, ya_ref, yb_ref, ga_ref, gb_ref, da_ref, db_ref, dg_ref):
        i = pl.program_id(0)

        @pl.when(i < nctx)
        def _():
            dg_ref[...] = jnp.zeros(dg_ref.shape, BF16)

        @pl.when(i >= nctx)
        def _():
            d = dm_ref[...]
            sa, sb = _sigmoid(ga_ref[...].astype(F32)), _sigmoid(gb_ref[...].astype(F32))
            da_ref[...] = (d * sa).astype(BF16)
            db_ref[...] = (d * sb).astype(BF16)
            dg_ref[:, 0:D] = (d * ya_ref[...] * sa * (1.0 - sa)).astype(BF16)
            dg_ref[:, D:2 * D] = (d * yb_ref[...] * sb * (1.0 - sb)).astype(BF16)

    row = pl.BlockSpec((TB, D), lambda i: (lat(i), 0))
    return pl.pallas_call(
        kern, name="merge_bwd", grid=(dm.nblk,),
        in_specs=[row, row, row, pl.BlockSpec((TB, D), lambda i: (i, gcol)),
                  pl.BlockSpec((TB, D), lambda i: (i, gcol + 1))],
        out_specs=[row, row, pl.BlockSpec((TB, 2 * D), lambda i: (i, 0))],
        out_shape=[jax.ShapeDtypeStruct((dm.S, D), BF16), jax.ShapeDtypeStruct((dm.S, D), BF16),
                   jax.ShapeDtypeStruct((dm.T, 2 * D), BF16)],
        compiler_params=_cparams(("arbitrary",)),
    )(dM, ya, yb, P, P)


def _branch_bwd(dm, dA, dB, Hf, Hb, P, O, mhw):
    TB, nctx, G = dm.TB, dm.nctx, dm.G
    lat = lambda i: jnp.maximum(i - nctx, 0)

    def kern(da_ref, db_ref, hf_ref, hb_ref, oa_ref, za_ref, zb_ref, o_ref, w_ref,
             doa_ref, dza_ref, dzb_ref, dh_ref, do_ref, del_ref, gw_ref):
        i = pl.program_id(0)

        @pl.when(i == 0)
        def _():
            gw_ref[...] = jnp.zeros(gw_ref.shape, F32)

        @pl.when(i < nctx)
        def _():
            doa_ref[...] = jnp.zeros(doa_ref.shape, BF16)
            dza_ref[...] = jnp.zeros(dza_ref.shape, BF16)
            dzb_ref[...] = jnp.zeros(dzb_ref.shape, BF16)
            dh_ref[...] = jnp.zeros(dh_ref.shape, F32)

        @pl.when(i >= nctx)
        def _():
            for h in range(NH_A):
                cols = slice(h * DV_A, (h + 1) * DV_A)
                hs = hf_ref[:, cols] + hb_ref[:, cols]
                rstd = lax.rsqrt(jnp.mean(hs * hs, axis=1, keepdims=True) + EPS)
                xn = hs * rstd
                w = w_ref[0:1, cols]
                hn = xn * w
                so, za = _sigmoid(oa_ref[:, cols].astype(F32)), za_ref[:, cols].astype(F32)
                sz = _sigmoid(za)
                silu = za * sz
                da = da_ref[:, cols]
                doa_ref[:, cols] = (da * hn * silu * so * (1.0 - so)).astype(BF16)
                dza_ref[:, cols] = (da * hn * so * sz * (1.0 + za * (1.0 - sz))).astype(BF16)
                dhn = da * so * silu
                gw_ref[0:1, cols] += jnp.sum(dhn * xn, axis=0, keepdims=True)
                dxn = dhn * w
                dh_ref[:, cols] = rstd * (dxn - xn * jnp.mean(dxn * xn, axis=1, keepdims=True))
            for h in range(NH_B):
                cols = slice(h * HD_B, (h + 1) * HD_B)
                zb = zb_ref[:, cols].astype(F32)
                sz = _sigmoid(zb)
                db, o = db_ref[:, cols], o_ref[h]
                do = db * (zb * sz)
                do_ref[h] = do.astype(BF16)
                dzb_ref[:, cols] = (db * o * sz * (1.0 + zb * (1.0 - sz))).astype(BF16)
                del_ref[h // G, :, (h % G):(h % G) + 1] = jnp.sum(do * o, axis=1, keepdims=True)

    vlat = pl.BlockSpec((TB, dm.V), lambda i: (lat(i), 0))
    qlat = pl.BlockSpec((TB, dm.QB), lambda i: (lat(i), 0))
    hlat = pl.BlockSpec((NH_B, TB, HD_B), lambda i: (0, lat(i), 0))
    vrow = pl.BlockSpec((TB, dm.V), lambda i: (i, 0))
    qrow = pl.BlockSpec((TB, dm.QB), lambda i: (i, 0))
    pv = lambda n: pl.BlockSpec((TB, dm.V), lambda i: (i, dm.off[n] // dm.V))
    return pl.pallas_call(
        kern, name="branch_bwd", grid=(dm.nblk,),
        in_specs=[vlat, qlat, vrow, vrow, pv("oa"), pv("za"),
                  pl.BlockSpec((TB, dm.QB), lambda i: (i, dm.off["zb"] // dm.QB)), hlat,
                  pl.BlockSpec((SUBLANE, dm.V), lambda i: (0, 0))],
        out_specs=[vrow, vrow, qrow, vrow, hlat,
                   pl.BlockSpec((NKV_B, TB, G), lambda i: (0, lat(i), 0)),
                   pl.BlockSpec((SUBLANE, dm.V), lambda i: (0, 0))],
        out_shape=[jax.ShapeDtypeStruct((dm.T, dm.V), BF16), jax.ShapeDtypeStruct((dm.T, dm.V), BF16),
                   jax.ShapeDtypeStruct((dm.T, dm.QB), BF16), jax.ShapeDtypeStruct((dm.T, dm.V), F32),
                   jax.ShapeDtypeStruct((NH_B, dm.S, HD_B), BF16), jax.ShapeDtypeStruct((NKV_B, dm.S, G), F32),
                   jax.ShapeDtypeStruct((SUBLANE, dm.V), F32)],
        compiler_params=_cparams(("arbitrary",)),
    )(dA, dB, Hf, Hb, P, P, P, O, mhw)


def _conv_bwd(dm, dq, dk, P, convp):
    W = 2 * dm.QK
    col = dm.off["qk"] // W
    kscale = DK_A ** -0.5
    TB = dm.TB

    def kern1(dq_ref, dk_ref, x_ref, p_ref, n_ref, c_ref, dz_ref):
        x = x_ref[...].astype(F32)
        xp, xn = _shifted(dm, x, p_ref, n_ref)
        z = c_ref[3:4, :] + xp * c_ref[0:1, :] + x * c_ref[1:2, :] + xn * c_ref[2:3, :]
        sz = _sigmoid(z)
        dact = jnp.concatenate([dq_ref[...], dk_ref[...] * kscale], axis=1)
        dz_ref[...] = dact * sz * (1.0 + z * (1.0 - sz))

    half = pl.BlockSpec((TB, dm.QK), lambda i: (i, 0))
    par = pl.BlockSpec((SUBLANE, W), lambda i: (0, 0))
    dz = pl.pallas_call(
        kern1, name="conv_bwd_act", grid=(dm.nblk,),
        in_specs=[half, half] + _conv_specs(dm, W, col, P_HALO) + [par],
        out_specs=pl.BlockSpec((TB, W), lambda i: (i, 0)),
        out_shape=jax.ShapeDtypeStruct((dm.T, W), F32),
        compiler_params=_cparams(("parallel",)),
    )(dq, dk, P, P, P, convp)

    def kern2(z_ref, zp_ref, zn_ref, x_ref, p_ref, n_ref, c_ref, dx_ref, cs_ref):
        @pl.when(pl.program_id(0) == 0)
        def _():
            cs_ref[...] = jnp.zeros(cs_ref.shape, F32)

        dz = z_ref[...]
        dzp, dzn = _shifted(dm, dz, zp_ref, zn_ref)
        dx_ref[...] = (dzn * c_ref[0:1, :] + dz * c_ref[1:2, :] + dzp * c_ref[2:3, :]).astype(BF16)
        x = x_ref[...].astype(F32)
        xp, xn = _shifted(dm, x, p_ref, n_ref)
        cs_ref[0:1, :] += jnp.sum(dz * xp, axis=0, keepdims=True)
        cs_ref[1:2, :] += jnp.sum(dz * x, axis=0, keepdims=True)
        cs_ref[2:3, :] += jnp.sum(dz * xn, axis=0, keepdims=True)
        cs_ref[3:4, :] += jnp.sum(dz, axis=0, keepdims=True)

    return pl.pallas_call(
        kern2, name="conv_bwd_taps", grid=(dm.nblk,),
        in_specs=_conv_specs(dm, W, 0) + _conv_specs(dm, W, col, P_HALO) + [par],
        out_specs=[pl.BlockSpec((TB, W), lambda i: (i, 0)), par],
        out_shape=[jax.ShapeDtypeStruct((dm.T, W), BF16), jax.ShapeDtypeStruct((SUBLANE, W), F32)],
        compiler_params=_cparams(("arbitrary",)),
    )(dz, dz, dz, P, P, P, convp)


def _gates_bwd(dm, dG, G):
    TB = dm.TB

    def kern(d_ref, g_ref, o_ref, cs_ref):
        @pl.when(pl.program_id(0) == 0)
        def _():
            cs_ref[...] = jnp.zeros(cs_ref.shape, F32)

        lane = lax.broadcasted_iota(jnp.int32, (TB, LANE), 1)
        is_f = ((lane // NH_A) % 2) == 1
        d = d_ref[...]
        dpre = jnp.where(lane < dm.NIF, jnp.where(is_f, d * (1.0 - jnp.exp(g_ref[...])), d), 0.0)
        cs_ref[0:1, :] += jnp.sum(dpre, axis=0, keepdims=True)
        if dm.IFP > LANE:
            o_ref[:, LANE:] = jnp.zeros((TB, dm.IFP - LANE), BF16)
        o_ref[:, 0:LANE] = dpre.astype(BF16)

    row = pl.BlockSpec((TB, LANE), lambda i: (i, 0))
    return pl.pallas_call(
        kern, name="gates_bwd", grid=(dm.nblk,),
        in_specs=[row, row],
        out_specs=[pl.BlockSpec((TB, dm.IFP), lambda i: (i, 0)), pl.BlockSpec((SUBLANE, LANE), lambda i: (0, 0))],
        out_shape=[jax.ShapeDtypeStruct((dm.T, dm.IFP), BF16), jax.ShapeDtypeStruct((SUBLANE, LANE), F32)],
        compiler_params=_cparams(("arbitrary",)),
    )(dG, G)


def _ln_mod_bwd(dm, dU, ctx2, x2, modv, dr_a):
    TB, D, nctx = dm.TB, dm.D, dm.nctx
    lat = lambda i: jnp.maximum(i - nctx, 0)

    def kern(du_ref, c_ref, x_ref, m_ref, dr_ref, gx_ref, cs_ref):
        i = pl.program_id(0)
        is_ctx = i < nctx

        @pl.when(i == 0)
        def _():
            cs_ref[...] = jnp.zeros(cs_ref.shape, F32)

        xh, rstd = _ln_stats(jnp.where(is_ctx, c_ref[...], x_ref[...]))
        du = du_ref[...]
        s_shift = jnp.sum(du, axis=0, keepdims=True)
        s_scale = jnp.sum(du * xh, axis=0, keepdims=True)
        cs_ref[0:1, :] += jnp.where(is_ctx, 0.0, s_shift)
        cs_ref[1:2, :] += jnp.where(is_ctx, 0.0, s_scale)
        cs_ref[2:3, :] += jnp.where(is_ctx, s_shift, 0.0)
        cs_ref[3:4, :] += jnp.where(is_ctx, s_scale, 0.0)

        @pl.when(i >= nctx)
        def _():
            dxh = du * (1.0 + m_ref[1:2, :])
            gx_ref[...] = dr_ref[...] + rstd * (dxh - jnp.mean(dxh, axis=1, keepdims=True)
                                                - xh * jnp.mean(dxh * xh, axis=1, keepdims=True))

    row = pl.BlockSpec((TB, D), lambda i: (i, 0))
    lrow = pl.BlockSpec((TB, D), lambda i: (lat(i), 0))
    par = pl.BlockSpec((SUBLANE, D), lambda i: (0, 0))
    return pl.pallas_call(
        kern, name="ln_mod_bwd", grid=(dm.nblk,),
        in_specs=[row] + _row_specs(dm) + [par, lrow],
        out_specs=[lrow, par],
        out_shape=[jax.ShapeDtypeStruct((dm.S, D), F32), jax.ShapeDtypeStruct((SUBLANE, D), F32)],
        compiler_params=_cparams(("arbitrary",)),
    )(dU, ctx2, x2, modv, dr_a)


def _mod_fwd(craw, w_loc, b_loc):
    R, D = craw.shape
    n = w_loc.shape[1]

    def kern(c_ref, w_ref, b_ref, o_ref):
        c = c_ref[...]
        o_ref[...] = _dot((c * _sigmoid(c)).astype(BF16), w_ref[...].astype(BF16)) + b_ref[0:1, :]

    return pl.pallas_call(
        kern, name="mod_fwd", out_shape=jax.ShapeDtypeStruct((R, n), F32),
        compiler_params=pltpu.CompilerParams(vmem_limit_bytes=VMEM_LIMIT),
    )(craw, w_loc, b_loc)


def _mod_bwd(crawT, dmod, w_loc):
    D, R = crawT.shape
    n = w_loc.shape[1]

    def kern(c_ref, d_ref, w_ref, gw_ref, dc_ref):
        c = c_ref[...]
        d = d_ref[...].astype(BF16)
        gw_ref[...] = _dot((c * _sigmoid(c)).astype(BF16), d)
        dc_ref[...] = _dot_nt(d, w_ref[...].astype(BF16))

    return pl.pallas_call(
        kern, name="mod_bwd",
        out_shape=[jax.ShapeDtypeStruct((D, n), F32), jax.ShapeDtypeStruct((R, D), F32)],
        compiler_params=pltpu.CompilerParams(vmem_limit_bytes=VMEM_LIMIT),
    )(crawT, dmod, w_loc)


def _cctx_grad(dsilu, c_ctx):
    def kern(p_ref, c_ref, o_ref):
        c = c_ref[...]
        sc = _sigmoid(c)
        o_ref[...] = p_ref[...] * (sc * (1.0 + c * (1.0 - sc)))

    return pl.pallas_call(kern, name="cctx_grad", out_shape=jax.ShapeDtypeStruct(c_ctx.shape, F32))(dsilu, c_ctx)


def _adamw(w, g, m, v, name):
    lead, (R, C) = w.shape[:-2], w.shape[-2:]
    assert all(d == 1 for d in lead)
    tb = _tile(R, max(SUBLANE, (1 << 19) // (4 * C) // SUBLANE * SUBLANE), SUBLANE)
    c1 = 1.0 / (1.0 - ADAM_B1 ** ADAM_STEP)
    c2 = 1.0 / (1.0 - ADAM_B2 ** ADAM_STEP)

    def kern(w_ref, g_ref, m_ref, v_ref, d_ref, nm_ref, nv_ref):
        g = g_ref[...]
        nm = ADAM_B1 * m_ref[...] + (1.0 - ADAM_B1) * g
        nv = ADAM_B2 * v_ref[...] + (1.0 - ADAM_B2) * (g * g)
        nm_ref[...] = nm
        nv_ref[...] = nv
        d_ref[...] = -ADAM_LR * ((nm * c1) / (jnp.sqrt(nv * c2) + ADAM_EPS) + ADAM_WD * w_ref[...])

    spec = pl.BlockSpec(lead + (tb, C), lambda i: (0,) * len(lead) + (i, 0))
    return pl.pallas_call(
        kern, name=name, grid=(R // tb,), in_specs=[spec] * 4, out_specs=[spec] * 3,
        out_shape=[jax.ShapeDtypeStruct(w.shape, F32)] * 3,
        compiler_params=_cparams(("parallel",)),
    )(w, g, m, v)


def _rope_tables(dm):
    half = HD_B // 2
    rows_n = dm.S // GRID_W
    row = jnp.repeat(jnp.arange(rows_n), GRID_W).astype(F32)
    col = jnp.tile(jnp.arange(GRID_W), rows_n).astype(F32)
    inv = ROPE_THETA ** (-jnp.arange(0, half, 2, dtype=F32) / half)
    ar, ac = row[:, None] * inv[None], col[:, None] * inv[None]
    cos = jnp.concatenate([jnp.cos(ar), jnp.cos(ar), jnp.cos(ac), jnp.cos(ac)], axis=1)
    zr = jnp.zeros_like(ar)
    sa = jnp.concatenate([-jnp.sin(ar), zr, -jnp.sin(ac), zr], axis=1)
    sb = jnp.concatenate([zr, jnp.sin(ar), zr, jnp.sin(ac)], axis=1)
    ctx = jnp.stack([jnp.ones((dm.Tc, HD_B), F32), jnp.zeros((dm.Tc, HD_B), F32), jnp.zeros((dm.Tc, HD_B), F32)])
    return jnp.concatenate([ctx, jnp.stack([cos, sa, sb])], axis=1)


def _rows8(*rows, width):
    out = [jnp.pad(r.reshape(-1).astype(F32), (0, width - r.size)) for r in rows]
    n = -(-len(out) // SUBLANE) * SUBLANE
    out += [jnp.zeros((width,), F32)] * (n - len(out))
    return jnp.stack(out)


def _ref_starts(dm):
    starts, o = {}, 0
    for n, wd in zip(dm.ref_names, dm.ref_widths):
        starts[n] = o
        o += wd
    return starts


def _to_padded(dm, shards):
    n = shards.shape[2]
    starts, wref = _ref_starts(dm), dict(zip(dm.ref_names, dm.ref_widths))
    pieces = []
    for name in dm.order:
        lo, hi = starts[name], starts[name] + wref[name]
        for j in range(N_DEV):
            a, b = max(lo, j * n), min(hi, (j + 1) * n)
            if a < b:
                pieces.append(shards[j][:, a - j * n:b - j * n])
        if dm.w[name] > wref[name]:
            pieces.append(jnp.zeros((shards.shape[1], dm.w[name] - wref[name]), shards.dtype))
    return jnp.concatenate(pieces, axis=1)


def _from_padded(dm, w_pad, n):
    starts = _ref_starts(dm)
    slabs = []
    for j in range(N_DEV):
        pieces = []
        for name, wd in zip(dm.ref_names, dm.ref_widths):
            a, b = max(starts[name], j * n), min(starts[name] + wd, (j + 1) * n)
            if a < b:
                o = dm.off[name] - starts[name]
                pieces.append(w_pad[:, a + o:b + o])
        slabs.append(jnp.concatenate(pieces, axis=1))
    return jnp.stack(slabs)


def kernel(x, c, ctx, c_ctx, w_mod, b_mod, w_in, b_if, conv_w, conv_b, mh_norm_w, q_norm_w, k_norm_w, w_branch_a, w_branch_b, w_out, ln_w, ln_b, loss_target, m_c_ctx, m_w_mod, m_b_mod, m_w_in, m_b_if, m_conv_w, m_conv_b, m_mh_norm_w, m_q_norm_w, m_k_norm_w, m_w_branch_a, m_w_branch_b, m_w_out, m_ln_w, m_ln_b, v_c_ctx, v_w_mod, v_b_mod, v_w_in, v_b_if, v_conv_w, v_conv_b, v_mh_norm_w, v_q_norm_w, v_k_norm_w, v_w_branch_a, v_w_branch_b, v_w_out, v_ln_w, v_ln_b):
    S, D = x.shape[1], x.shape[2]
    Tc = ctx.shape[1]
    dm = Dims(S, Tc, D)
    T, QK2 = dm.T, 2 * dm.QK
    me = 4 * lax.axis_index("x") + 2 * lax.axis_index("y") + lax.axis_index("c")
    n_mod = w_mod.shape[2]
    n_in = w_in.shape[2]
    n_cv = conv_w.shape[2]
    rb = w_out.shape[1]

    pack0 = _all_gather(_rows8(c[0], conv_w[0, 0], conv_w[0, 1], conv_w[0, 2], width=D), "ag_cond", False)
    c_all = pack0[:, 0, :]
    conv_full = jnp.transpose(pack0[:, 1:4, :n_cv], (1, 0, 2)).reshape(CONV_W, QK2)
    convp = _rows8(conv_full[0], conv_full[1], conv_full[2], conv_b[0], width=QK2)

    w_in_all = _all_gather(w_in[0].astype(BF16), "ag_w_in", True)
    Wp = _to_padded(dm, w_in_all)
    wsq = jnp.concatenate([w_branch_a[0], w_branch_b[0], w_out[0]], axis=0).astype(BF16)

    craw = _rows8(*[c_all[j] for j in range(N_DEV)], c_ctx, width=D)
    b_loc = _rows8(lax.dynamic_slice(b_mod[0], (me * n_mod,), (n_mod,)), width=n_mod)
    mod_all = _all_gather(_mod_fwd(craw, w_mod[0], b_loc), "ag_mod", False)
    mod_rows = jnp.transpose(mod_all, (1, 0, 2)).reshape(2 * SUBLANE, 3 * D)
    mod_me = lax.dynamic_slice(mod_rows, (me, 0), (1, 3 * D))[0]
    mod_cx = mod_rows[N_DEV]
    modv = _rows8(mod_me[0:D], mod_me[D:2 * D], mod_cx[0:D], mod_cx[D:2 * D], mod_me[2 * D:3 * D], width=D)

    U, UT = _ln_mod_fwd(dm, ctx[0], x[0], modv)
    P, wsq_all = _mm(U, Wp, "mm_in_proj", tn=896, out_dtype=BF16, exchange=("ag", wsq))
    gate_pre = _mm(U, Wp[:, dm.off["if"]:dm.off["if"] + dm.IFP], "mm_gates", tn=dm.IFP)
    Wba = wsq_all[:, 0:rb, :].reshape(dm.V, D)
    Wbb = wsq_all[:, rb:2 * rb, :].reshape(dm.QB, D)
    Wout = wsq_all[:, 2 * rb:3 * rb, :].reshape(D, D)
    qk_act = _conv_silu_fwd(dm, P, convp)
    G = _gates_fwd(dm, gate_pre, _rows8(b_if[0], width=LANE))
    Hf, Cf, NMf = _mlstm_fwd(dm, qk_act, P, G, False)
    Hb, Cb, NMb = _mlstm_fwd(dm, qk_act, P, G, True)
    rope = _rope_tables(dm)
    qn, kn = _rows8(q_norm_w[0], width=HD_B), _rows8(k_norm_w[0], width=HD_B)
    Qr, Kr, Vb = _qk_prep(dm, P, rope, qn, kn)
    O, LSE = _attn_fwd(dm, Qr, Kr, Vb)
    mhw = _rows8(mh_norm_w[0], width=dm.V)
    A_in, B_in, A_inT, B_inT = _merge_prep(dm, Hf, Hb, P, O, mhw)
    ya = _mm(A_in, Wba, "mm_branch_a")
    yb = _mm(B_in, Wbb, "mm_branch_b")
    M_in, M_inT = _gate_merge(dm, ya, yb, P)
    out = _mm(M_in, Wout, "mm_out")
    lnp = _rows8(ln_w[0], ln_b[0], width=D)
    dr_a, d_out, cs_fin, loss_p = _final(dm, x[0], out, loss_target[0], modv, lnp)
    loss = lax.psum(loss_p[0, 0], ("x", "y", "c"))

    dM = _mm(d_out, Wout, "mm_d_merge", nt=True)
    gWout = _mm(M_inT, d_out, "mm_g_w_out", tk=2048, out_dtype=BF16)
    d_ya, d_yb, d_g = _merge_bwd(dm, dM, ya, yb, P)
    dA = _mm(d_ya, Wba, "mm_d_a", nt=True)
    gWba = _mm(A_inT, d_ya, "mm_g_w_ba", tk=2048, out_dtype=BF16)
    dB = _mm(d_yb, Wbb, "mm_d_b", nt=True)
    gWbb = _mm(B_inT, d_yb, "mm_g_w_bb", tk=2048, out_dtype=BF16)
    d_oa, d_za, d_zb, dH, dO, DEL, gmh = _branch_bwd(dm, dA, dB, Hf, Hb, P, O, mhw)
    dQr, dKr, d_vb = _attn_bwd(dm, Qr, Kr, Vb, dO, jnp.transpose(LSE, (0, 2, 1)), jnp.transpose(DEL, (0, 2, 1)))
    d_qb, d_kb, gqn, gkn = _qk_bwd(dm, dQr, dKr, P, rope, qn, kn)
    acc = _mlstm_bwd(dm, qk_act, P, G, Cf, NMf, Hf, dH, None, False)
    dq, dk, dv, dG = _mlstm_bwd(dm, qk_act, P, G, Cb, NMb, Hb, dH, acc, True)
    d_qk, cs_conv = _conv_bwd(dm, dq, dk, P, convp)
    d_if, gbif = _gates_bwd(dm, dG, G)
    parts = {"g": d_g, "qk": d_qk, "va": dv.astype(BF16), "oa": d_oa, "za": d_za, "qb": d_qb,
             "zb": d_zb, "kb": d_kb, "vb": d_vb, "if": d_if}
    dP = jnp.concatenate([parts[n] for n in dm.order], axis=1)
    gsq = jnp.concatenate([gWba.reshape(N_DEV, rb, D), gWbb.reshape(N_DEV, rb, D),
                           gWout.reshape(N_DEV, rb, D)], axis=1)
    gWp, gsq_all = _mm(UT, dP, "mm_g_w_in", tm=1024, tn=896, tk=2816, out_dtype=BF16,
                       exchange=("a2a", gsq))
    gW = _from_padded(dm, gWp, n_in)
    dU, gW_all = _mm(dP, Wp, "mm_d_u", nt=True, tk=1792, exchange=("a2a", gW))
    grad_x, cs_ln = _ln_mod_bwd(dm, dU, ctx[0], x[0], modv, dr_a)

    dmod_me = _rows8(jnp.concatenate([cs_ln[0], cs_ln[1], cs_fin[2]]),
                     jnp.concatenate([cs_ln[2], cs_ln[3], jnp.zeros((D,), F32)]), width=3 * D)
    dmod_all = _all_gather(dmod_me, "ag_dmod", False)
    dmod_loc = lax.dynamic_slice(dmod_all, (0, 0, me * n_mod), (N_DEV, 2, n_mod))
    dmod_rows = _rows8(*[dmod_loc[j, 0] for j in range(N_DEV)], jnp.sum(dmod_loc[:, 1, :], axis=0), width=n_mod)
    g_w_mod, dc_part = _mod_bwd(craw.T, dmod_rows, w_mod[0])

    PW = dm.PW
    small = _rows8(cs_fin[0], cs_fin[1], gmh[0], cs_conv[3], cs_conv[0], cs_conv[1], cs_conv[2],
                   dmod_me[0, 0:D], dmod_me[0, D:2 * D], dmod_me[0, 2 * D:3 * D],
                   dmod_me[1, 0:D], dmod_me[1, D:2 * D],
                   jnp.concatenate([gqn[0], gkn[0], gbif[0]]), dc_part[N_DEV], width=PW)
    tot = _sum_slots(_all_gather(small, "ag_small", False), "sum_small")
    g_ln_w, g_ln_b, g_mh, g_conv_b = tot[0, :D], tot[1, :D], tot[2, :dm.V], tot[3, :QK2]
    g_conv_full = tot[4:7, :QK2]
    g_b_mod = jnp.concatenate([tot[7, :D] + tot[10, :D], tot[8, :D] + tot[11, :D], tot[9, :D]])
    g_qn, g_kn, g_bif = tot[12, 0:HD_B], tot[12, HD_B:2 * HD_B], tot[12, 2 * HD_B:2 * HD_B + dm.NIF]
    g_c_ctx = _cctx_grad(tot[13:14, :D], c_ctx.reshape(1, D))[0]
    g_conv_w = lax.dynamic_slice(g_conv_full, (0, me * n_cv), (CONV_W, n_cv))

    g_w_in = _sum_slots(gW_all, "sum_g_w_in")
    g_sq = _sum_slots(gsq_all, "sum_g_w_sq")

    upd_in = [a[None] for a in _adamw(w_in[0], g_w_in, m_w_in[0], v_w_in[0], "adam_w_in")]
    g_w_in, g_w_mod = g_w_in[None], g_w_mod[None]
    g_ba, g_bb, g_out = g_sq[None, 0:rb], g_sq[None, rb:2 * rb], g_sq[None, 2 * rb:3 * rb]
    upd_md = _adamw(w_mod, g_w_mod, m_w_mod, v_w_mod, "adam_w_mod")
    upd_ba = _adamw(w_branch_a, g_ba, m_w_branch_a, v_w_branch_a, "adam_w_ba")
    upd_bb = _adamw(w_branch_b, g_bb, m_w_branch_b, v_w_branch_b, "adam_w_bb")
    upd_out = _adamw(w_out, g_out, m_w_out, v_w_out, "adam_w_out")
    names = ["c_ctx", "b_mod", "b_if", "conv_w", "conv_b", "mh", "qn", "kn", "ln_w", "ln_b"]
    ws = [c_ctx, b_mod, b_if, conv_w, conv_b, mh_norm_w, q_norm_w, k_norm_w, ln_w, ln_b]
    ms = [m_c_ctx, m_b_mod, m_b_if, m_conv_w, m_conv_b, m_mh_norm_w, m_q_norm_w, m_k_norm_w, m_ln_w, m_ln_b]
    vs = [v_c_ctx, v_b_mod, v_b_if, v_conv_w, v_conv_b, v_mh_norm_w, v_q_norm_w, v_k_norm_w, v_ln_w, v_ln_b]
    gs = [g_c_ctx, g_b_mod, g_bif, g_conv_w, g_conv_b, g_mh, g_qn, g_kn, g_ln_w, g_ln_b]
    sizes = [a.size for a in ws]
    tot_n = sum(sizes)
    padn = -(-tot_n // LANE) * LANE
    flat = lambda arrs: jnp.pad(jnp.concatenate([a.reshape(-1) for a in arrs]), (0, padn - tot_n)).reshape(1, padn)
    d_s, nm_s, nv_s = _adamw(flat(ws), flat(gs), flat(ms), flat(vs), "adam_small")

    def split(a):
        res, o = {}, 0
        for n, wv, sz in zip(names, ws, sizes):
            res[n] = a[0, o:o + sz].reshape(wv.shape)
            o += sz
        return res

    def assemble(s, big_in, big_md, big_ba, big_bb, big_out):
        return [s["c_ctx"], big_md, s["b_mod"], big_in, s["b_if"], s["conv_w"], s["conv_b"],
                s["mh"], s["qn"], s["kn"], big_ba, big_bb, big_out, s["ln_w"], s["ln_b"]]

    g_small = {n: g.reshape(wv.shape) for n, g, wv in zip(names, gs, ws)}
    grads = assemble(g_small, g_w_in, g_w_mod, g_ba, g_bb, g_out)
    deltas, new_m, new_v = [
        assemble(split(sm), upd_in[t], upd_md[t], upd_ba[t], upd_bb[t], upd_out[t])
        for t, sm in enumerate((d_s, nm_s, nv_s))]
    return (loss, grad_x[None], *grads, *deltas, *new_m, *new_v)
```

```python
import jax
import jax.numpy as jnp
from jax import lax
from jax.experimental import pallas as pl
from jax.experimental.pallas import tpu as pltpu

F32 = jnp.float32
BF16 = jnp.bfloat16
MESH = pl.DeviceIdType.MESH
N_DEV = 8

GRID_W = 64
NH_A = 8
DK_A = 128
DV_A = 256
CONV_W = 3
CHUNK = 64
M_INIT = -1e30
NH_B = 16
NKV_B = 4
HD_B = 128
ROPE_THETA = 10000.0
EPS = 1e-6
DEPTH = 1
ALPHA = (2 * DEPTH) ** 0.25
ADAM_LR = 0.001
ADAM_B1 = 0.9
ADAM_B2 = 0.999
ADAM_EPS = 1e-08
ADAM_WD = 0.01
ADAM_STEP = 10

LANE = 128
SUBLANE = 8
VMEM_LIMIT = 56 << 20


def _tile(n, target, align):
    best = None
    t = align
    while t <= min(n, target):
        if n % t == 0:
            best = t
        t += align
    return best if best is not None else n


class Dims:
    def __init__(self, S, Tc, D):
        self.S, self.Tc, self.D = S, Tc, D
        self.T = S + Tc
        self.QK = NH_A * DK_A
        self.V = NH_A * DV_A
        self.QB = NH_B * HD_B
        self.KVB = NKV_B * HD_B
        self.G = NH_B // NKV_B
        self.NIF = 4 * NH_A
        self.IFP = 512 if self.KVB % 512 == 0 else LANE
        self.ref_widths = [2 * self.QK, self.V, self.NIF, self.KVB, self.KVB,
                           self.V, self.V, self.QB, self.QB, 2 * D]
        self.ref_names = ["qk", "va", "if", "kb", "vb", "oa", "za", "qb", "zb", "g"]
        self.N_IN = sum(self.ref_widths)
        self.order = ["g", "qk", "va", "oa", "za", "qb", "zb", "kb", "vb", "if"]
        w = dict(zip(self.ref_names, self.ref_widths))
        w["if"] = self.IFP
        self.w = w
        self.off = {}
        o = 0
        for n in self.order:
            assert o % w[n] == 0, (n, o, w[n])
            self.off[n] = o
            o += w[n]
        self.NP = o
        self.TB = min(256, Tc)
        assert Tc % self.TB == 0 and S % self.TB == 0 and self.TB % CHUNK == 0
        self.nctx = Tc // self.TB
        self.nlat = S // self.TB
        self.nblk = self.nctx + self.nlat
        self.PW = max(D, self.V, 2 * self.QK, 3 * LANE)


def _cparams(sem):
    return pltpu.CompilerParams(dimension_semantics=sem, vmem_limit_bytes=VMEM_LIMIT)


def _sigmoid(x):
    return 1.0 / (1.0 + jnp.exp(-x))


def _my_pos():
    return lax.axis_index("x"), lax.axis_index("y"), lax.axis_index("c")


def _all_gather(x, name, big):
    R, C = x.shape
    space = pl.ANY if big else pltpu.VMEM

    def body(x_ref, out_ref, send_sems, recv_sems, local_sem):
        px, py, pc = _my_pos()
        me, sibling = (px, py, pc), (px, py, 1 - pc)
        chips = [(1 - px, py), (px, 1 - py), (1 - px, 1 - py)]

        def slot(bx, by, bc):
            return out_ref.at[4 * bx + 2 * by + bc]

        def copy(k, block, to, src=None):
            return pltpu.make_async_remote_copy(
                src_ref=slot(*block) if src is None else src, dst_ref=slot(*block),
                send_sem=send_sems.at[k], recv_sem=recv_sems.at[k],
                device_id=to, device_id_type=MESH)

        mine = pltpu.make_async_copy(x_ref, slot(*me), local_sem)
        mine.start()
        first = [copy(0, me, sibling, src=x_ref)]
        first += [copy(1 + j, me, (*chip, pc), src=x_ref) for j, chip in enumerate(chips)]
        for cp in first:
            cp.start()
        passed = [copy(4 + j, (*chip, pc), sibling) for j, chip in enumerate(chips)]
        for j, chip in enumerate(chips):
            copy(1 + j, (*chip, pc), me).wait_recv()
            passed[j].start()
        copy(0, sibling, me).wait_recv()
        for j, chip in enumerate(chips):
            copy(4 + j, (*chip, 1 - pc), me).wait_recv()
        for cp in first + passed:
            cp.wait_send()
        mine.wait()

    return pl.pallas_call(
        body, name=name,
        out_shape=jax.ShapeDtypeStruct((N_DEV, R, C), x.dtype),
        in_specs=[pl.BlockSpec(memory_space=space)],
        out_specs=pl.BlockSpec(memory_space=space),
        scratch_shapes=[pltpu.SemaphoreType.DMA((7,)), pltpu.SemaphoreType.DMA((7,)),
                        pltpu.SemaphoreType.DMA],
    )(x)


EXCHANGE_SEMS = [pltpu.SemaphoreType.DMA((N_DEV - 1,)), pltpu.SemaphoreType.DMA((N_DEV - 1,)),
                 pltpu.SemaphoreType.DMA]


def _exchange(kind, src_ref, land_ref, send_sems, recv_sems, local_sem):
    def copies():
        px, py, pc = _my_pos()
        me = 4 * px + 2 * py + pc
        own = src_ref if kind == "ag" else src_ref.at[me]
        local = pltpu.make_async_copy(own, land_ref.at[me], local_sem)
        sends, recvs = [], []
        for r in range(1, N_DEV):
            dx, dy, dc = (r >> 2) & 1, (r >> 1) & 1, r & 1
            qx = px if dx == 0 else 1 - px
            qy = py if dy == 0 else 1 - py
            qc = pc if dc == 0 else 1 - pc
            peer = 4 * qx + 2 * qy + qc
            sems = dict(send_sem=send_sems.at[r - 1], recv_sem=recv_sems.at[r - 1],
                        device_id=(qx, qy, qc), device_id_type=MESH)
            sends.append(pltpu.make_async_remote_copy(
                src_ref=src_ref if kind == "ag" else src_ref.at[peer], dst_ref=land_ref.at[me], **sems))
            recvs.append(pltpu.make_async_remote_copy(src_ref=own, dst_ref=land_ref.at[peer], **sems))
        return local, sends, recvs

    def start():
        local, sends, _ = copies()
        local.start()
        for cp in sends:
            cp.start()

    def wait():
        local, sends, recvs = copies()
        for cp in recvs:
            cp.wait_recv()
        for cp in sends:
            cp.wait_send()
        local.wait()

    return start, wait


def _land_shape(kind, src):
    return jax.ShapeDtypeStruct(src.shape if kind == "a2a" else (N_DEV,) + src.shape, src.dtype)


def _sum_slots(a, name):
    _, R, C = a.shape
    tb = _tile(R, max(SUBLANE, (1 << 20) // (4 * C) // SUBLANE * SUBLANE), SUBLANE)

    def kern(a_ref, o_ref):
        acc = a_ref[0].astype(F32)
        for j in range(1, N_DEV):
            acc = acc + a_ref[j].astype(F32)
        o_ref[...] = acc

    return pl.pallas_call(
        kern, name=name, grid=(R // tb,),
        in_specs=[pl.BlockSpec((N_DEV, tb, C), lambda i: (0, i, 0))],
        out_specs=pl.BlockSpec((tb, C), lambda i: (i, 0)),
        out_shape=jax.ShapeDtypeStruct((R, C), F32),
        compiler_params=_cparams(("parallel",)),
    )(a)


def _mm(a, b, name, nt=False, tm=768, tn=1024, tk=2048, out_dtype=F32, exchange=None):
    M, K = a.shape
    N = b.shape[0] if nt else b.shape[1]
    assert (b.shape[1] if nt else b.shape[0]) == K
    tm, tn, tk = _tile(M, tm, 16), _tile(N, tn, LANE), _tile(K, tk, LANE)
    ni, nj, nk = M // tm, N // tn, K // tk

    def dot(x, y):
        if nt:
            return lax.dot_general(x, y, (((1,), (1,)), ((), ())), preferred_element_type=F32)
        return jnp.dot(x, y, preferred_element_type=F32)

    def kern(a_ref, b_ref, *rest):
        if exchange is not None:
            src_ref, o_ref, land_ref, acc_ref, send_sems, recv_sems, local_sem = rest
            start, wait = _exchange(exchange[0], src_ref, land_ref, send_sems, recv_sems, local_sem)
            i, j, kk = pl.program_id(0), pl.program_id(1), pl.program_id(2)
            pl.when((i == 0) & (j == 0) & (kk == 0))(start)
        else:
            o_ref, acc_ref = rest
        k = pl.program_id(2)
        part = dot(a_ref[...], b_ref[...])
        if nk == 1:
            o_ref[...] = part.astype(o_ref.dtype)
        else:
            @pl.when(k == 0)
            def _():
                acc_ref[...] = part

            @pl.when(k > 0)
            def _():
                acc_ref[...] += part

            @pl.when(k == nk - 1)
            def _():
                o_ref[...] = acc_ref[...].astype(o_ref.dtype)

        if exchange is not None:
            pl.when((i == ni - 1) & (j == nj - 1) & (kk == nk - 1))(wait)

    b_spec = (pl.BlockSpec((tn, tk), lambda i, j, k: (j, k)) if nt
              else pl.BlockSpec((tk, tn), lambda i, j, k: (k, j)))
    in_specs = [pl.BlockSpec((tm, tk), lambda i, j, k: (i, k)), b_spec]
    out_specs = pl.BlockSpec((tm, tn), lambda i, j, k: (i, j))
    out_shape = jax.ShapeDtypeStruct((M, N), out_dtype)
    scratch = [pltpu.VMEM((tm, tn) if nk > 1 else (SUBLANE, LANE), F32)]
    args = (a, b)
    sem = ("parallel", "parallel", "arbitrary")
    if exchange is not None:
        in_specs = in_specs + [pl.BlockSpec(memory_space=pl.ANY)]
        out_specs = [out_specs, pl.BlockSpec(memory_space=pl.ANY)]
        out_shape = [out_shape, _land_shape(*exchange)]
        scratch = scratch + EXCHANGE_SEMS
        args = (a, b, exchange[1])
        sem = ("arbitrary", "arbitrary", "arbitrary")
    return pl.pallas_call(
        kern, name=name, grid=(ni, nj, nk), in_specs=in_specs, out_specs=out_specs,
        out_shape=out_shape, scratch_shapes=scratch, compiler_params=_cparams(sem),
    )(*args)


def _ln_stats(x):
    mu = jnp.mean(x, axis=-1, keepdims=True)
    xc = x - mu
    var = jnp.mean(xc * xc, axis=-1, keepdims=True)
    rstd = lax.rsqrt(var + EPS)
    return xc * rstd, rstd


def _row_specs(dm):
    TB, D, nctx = dm.TB, dm.D, dm.nctx
    return [pl.BlockSpec((TB, D), lambda i: (jnp.minimum(i, nctx - 1), 0)),
            pl.BlockSpec((TB, D), lambda i: (jnp.maximum(i - nctx, 0), 0))]


def _ln_mod_fwd(dm, ctx2, x2, modv):
    TB, D, nctx = dm.TB, dm.D, dm.nctx

    def kern(c_ref, x_ref, m_ref, u_ref, ut_ref):
        is_ctx = pl.program_id(0) < nctx
        xh, _ = _ln_stats(jnp.where(is_ctx, c_ref[...], x_ref[...]))
        shift = jnp.where(is_ctx, m_ref[2:3, :], m_ref[0:1, :])
        scale = jnp.where(is_ctx, m_ref[3:4, :], m_ref[1:2, :])
        u = xh * (1.0 + scale) + shift
        u_ref[...] = u.astype(BF16)
        ut_ref[...] = u.T.astype(BF16)

    return pl.pallas_call(
        kern, name="ln_mod_fwd", grid=(dm.nblk,),
        in_specs=_row_specs(dm) + [pl.BlockSpec((SUBLANE, D), lambda i: (0, 0))],
        out_specs=[pl.BlockSpec((TB, D), lambda i: (i, 0)), pl.BlockSpec((D, TB), lambda i: (0, i))],
        out_shape=[jax.ShapeDtypeStruct((dm.T, D), BF16), jax.ShapeDtypeStruct((D, dm.T), BF16)],
        compiler_params=_cparams(("parallel",)),
    )(ctx2, x2, modv)


P_HALO = 16


def _conv_specs(dm, W, col, hr=SUBLANE):
    TB, T = dm.TB, dm.T
    per = TB // hr
    last = T // hr - 1
    return [pl.BlockSpec((TB, W), lambda i: (i, col)),
            pl.BlockSpec((hr, W), lambda i: (jnp.maximum(i * per - 1, 0), col)),
            pl.BlockSpec((hr, W), lambda i: (jnp.minimum((i + 1) * per, last), col))]


def _shifted(dm, x, prev_ref, next_ref):
    TB, nctx, nblk = dm.TB, dm.nctx, dm.nblk
    i = pl.program_id(0)
    row = lax.broadcasted_iota(jnp.int32, x.shape, 0)
    zero_prev = (i == 0) | (i == nctx)
    zero_next = (i == nctx - 1) | (i == nblk - 1)
    pv, nv = prev_ref[...].astype(F32), next_ref[...].astype(F32)
    hrow = lax.broadcasted_iota(jnp.int32, pv.shape, 0)
    last_prev = jnp.sum(jnp.where(hrow == pv.shape[0] - 1, pv, 0.0), axis=0, keepdims=True)
    first_next = jnp.sum(jnp.where(hrow == 0, nv, 0.0), axis=0, keepdims=True)
    before = jnp.where(zero_prev, 0.0, last_prev)
    after = jnp.where(zero_next, 0.0, first_next)
    xp = jnp.where(row == 0, before, pltpu.roll(x, 1, 0))
    xn = jnp.where(row == TB - 1, after, pltpu.roll(x, TB - 1, 0))
    return xp, xn


def _conv_silu_fwd(dm, P, convp):
    W = 2 * dm.QK
    col = dm.off["qk"] // W
    kscale = DK_A ** -0.5

    def kern(x_ref, p_ref, n_ref, c_ref, o_ref):
        x = x_ref[...].astype(F32)
        xp, xn = _shifted(dm, x, p_ref, n_ref)
        z = c_ref[3:4, :] + xp * c_ref[0:1, :] + x * c_ref[1:2, :] + xn * c_ref[2:3, :]
        lane = lax.broadcasted_iota(jnp.int32, (1, W), 1)
        cs = jnp.where(lane >= dm.QK, kscale, 1.0)
        o_ref[...] = z * _sigmoid(z) * cs

    return pl.pallas_call(
        kern, name="conv_silu_fwd", grid=(dm.nblk,),
        in_specs=_conv_specs(dm, W, col, P_HALO) + [pl.BlockSpec((SUBLANE, W), lambda i: (0, 0))],
        out_specs=pl.BlockSpec((dm.TB, W), lambda i: (i, 0)),
        out_shape=jax.ShapeDtypeStruct((dm.T, W), F32),
        compiler_params=_cparams(("parallel",)),
    )(P, P, P, convp)


def _gates_fwd(dm, P, bif):
    col = 0

    def kern(x_ref, b_ref, o_ref):
        x = x_ref[...] + b_ref[0:1, :]
        lane = lax.broadcasted_iota(jnp.int32, x.shape, 1)
        is_f = ((lane // NH_A) % 2) == 1
        ls = jnp.minimum(x, 0.0) - jnp.log(1.0 + jnp.exp(-jnp.abs(x)))
        o_ref[...] = jnp.where(lane < dm.NIF, jnp.where(is_f, ls, x), 0.0)

    return pl.pallas_call(
        kern, name="gates_fwd", grid=(dm.nblk,),
        in_specs=[pl.BlockSpec((dm.TB, LANE), lambda i: (i, col)),
                  pl.BlockSpec((SUBLANE, LANE), lambda i: (0, 0))],
        out_specs=pl.BlockSpec((dm.TB, LANE), lambda i: (i, 0)),
        out_shape=jax.ShapeDtypeStruct((dm.T, LANE), F32),
        compiler_params=_cparams(("parallel",)),
    )(P, bif)


def _mlstm_order(dm, reverse, backward):
    nctx, nblk = dm.nctx, dm.nblk

    def idx(i):
        if backward:
            i = nblk - 1 - i
        if not reverse:
            return i
        return jnp.where(i < nctx, nctx - 1 - i, nblk - 1 - (i - nctx))

    return idx


def _chunk_gates(g, ci, cf, mask_f, maskT_f, eye_f):
    lane = lax.broadcasted_iota(jnp.int32, g.shape, 1)
    gi_c = jnp.sum(jnp.where(lane == ci, g, 0.0), axis=1, keepdims=True)
    gf_c = jnp.sum(jnp.where(lane == cf, g, 0.0), axis=1, keepdims=True)
    gi_r = jnp.sum(eye_f * gi_c, axis=0, keepdims=True)
    gf_r = jnp.sum(eye_f * gf_c, axis=0, keepdims=True)
    b_c = jnp.sum(mask_f * gf_r, axis=1, keepdims=True)
    b_r = jnp.sum(maskT_f * gf_c, axis=0, keepdims=True)
    return gi_c, gi_r, b_c, b_r


def _chunk_masks(reverse):
    L = CHUNK
    r = lax.broadcasted_iota(jnp.int32, (L, L), 0)
    c = lax.broadcasted_iota(jnp.int32, (L, L), 1)
    mask = (c >= r) if reverse else (c <= r)
    maskT = (r >= c) if reverse else (r <= c)
    return mask, mask.astype(F32), maskT.astype(F32), (r == c).astype(F32)


def _pick_row(x, e):
    r = lax.broadcasted_iota(jnp.int32, x.shape, 0)
    return jnp.sum(jnp.where(r == e, x, 0.0), axis=0, keepdims=True)


def _dot_nt(a, b):
    return lax.dot_general(a, b, (((1,), (1,)), ((), ())), preferred_element_type=F32)


def _dot(a, b):
    return jnp.dot(a, b, preferred_element_type=F32)


def _chunk_fwd_core(q, k, g, ci, cf, C0, n0, m0, masks, reverse):
    mask, mask_f, maskT_f, eye_f = masks
    gi_c, gi_r, b_c, b_r = _chunk_gates(g, ci, cf, mask_f, maskT_f, eye_f)
    d = jnp.where(mask, b_c - b_r + gi_r, -jnp.inf)
    m_c = jnp.maximum(b_c + m0, jnp.max(d, axis=1, keepdims=True))
    w = jnp.exp(d - m_c)
    a_c = jnp.exp(b_c + m0 - m_c)
    qb, kb = q.astype(BF16), k.astype(BF16)
    s = _dot_nt(qb, kb) * w
    den = a_c * jnp.sum(q * n0, axis=1, keepdims=True) + jnp.sum(s, axis=1, keepdims=True)
    e = 0 if reverse else CHUNK - 1
    m_end, b_end, a_end = _pick_row(m_c, e), _pick_row(b_c, e), _pick_row(a_c, e)
    w_end = jnp.exp(b_end - b_c + gi_c - m_end)
    return qb, kb, s, w, a_c, m_c, den, w_end, a_end, m_end


MLSTM_HEADS_PER_STEP = 4


def _heads_per_step():
    return MLSTM_HEADS_PER_STEP if NH_A % MLSTM_HEADS_PER_STEP == 0 else 1


def _mlstm_fwd(dm, qk_act, P, G, reverse):
    TB, T = dm.TB, dm.T
    NC = TB // CHUNK
    idx = _mlstm_order(dm, reverse, False)
    vcol = dm.off["va"] // DV_A
    base = 2 * NH_A if reverse else 0

    HP = _heads_per_step()
    hcols = lambda hh, w: slice(hh * w, (hh + 1) * w)

    def kern(q_ref, k_ref, v_ref, g_ref, h_ref, cst_ref, nm_ref, C_s, N_s):
        i, hp = pl.program_id(0), pl.program_id(1)
        row8 = lax.broadcasted_iota(jnp.int32, (SUBLANE, DK_A), 0)
        heads = [hp * HP + hh for hh in range(HP)]

        @pl.when(i == 0)
        def _():
            for h in heads:
                C_s[h] = jnp.zeros((DK_A, DV_A), F32)
                N_s[h] = jnp.where(row8 == 1, M_INIT, 0.0)

        masks = _chunk_masks(reverse)
        state = [(C_s[h], N_s.at[h][0:1, :], N_s.at[h][1:2, 0:1]) for h in heads]
        for c in (range(NC - 1, -1, -1) if reverse else range(NC)):
            rows = pl.ds(c * CHUNK, CHUNK)
            g = g_ref[rows, :]
            for hh, h in enumerate(heads):
                C0, n0, m0 = state[hh]
                q, k = q_ref[rows, hcols(hh, DK_A)], k_ref[rows, hcols(hh, DK_A)]
                v = v_ref[rows, hcols(hh, DV_A)].astype(F32)
                cst_ref[hh, c] = C0
                nm_ref[hh, c] = jnp.where(row8 == 0, n0, jnp.where(row8 == 1, m0, 0.0))
                qb, kb, s, w, a_c, m_c, den, w_end, a_end, m_end = _chunk_fwd_core(
                    q, k, g, base + h, base + NH_A + h, C0, n0, m0, masks, reverse)
                vb = v.astype(BF16)
                num = a_c * _dot(qb, C0.astype(BF16)) + _dot(s.astype(BF16), vb)
                h_ref[rows, hcols(hh, DV_A)] = num / jnp.maximum(jnp.abs(den), jnp.exp(-m_c))
                state[hh] = (a_end * C0 + _dot(k.T.astype(BF16), (w_end * v).astype(BF16)),
                             a_end * n0 + jnp.sum(w_end * k, axis=0, keepdims=True), m_end)
        for hh, h in enumerate(heads):
            C0, n0, m0 = state[hh]
            C_s[h] = C0
            N_s[h] = jnp.where(row8 == 0, n0, jnp.where(row8 == 1, m0, 0.0))

    nch = T // CHUNK
    NG = NH_A // HP
    return pl.pallas_call(
        kern, name="mlstm_fwd_rev" if reverse else "mlstm_fwd", grid=(dm.nblk, NG),
        in_specs=[pl.BlockSpec((TB, HP * DK_A), lambda i, h: (idx(i), h)),
                  pl.BlockSpec((TB, HP * DK_A), lambda i, h: (idx(i), NG + h)),
                  pl.BlockSpec((TB, HP * DV_A), lambda i, h: (idx(i), vcol // HP + h)),
                  pl.BlockSpec((TB, LANE), lambda i, h: (idx(i), 0))],
        out_specs=[pl.BlockSpec((TB, HP * DV_A), lambda i, h: (idx(i), h)),
                   pl.BlockSpec((HP, NC, DK_A, DV_A), lambda i, h: (h, idx(i), 0, 0)),
                   pl.BlockSpec((HP, NC, SUBLANE, DK_A), lambda i, h: (h, idx(i), 0, 0))],
        out_shape=[jax.ShapeDtypeStruct((T, dm.V), F32),
                   jax.ShapeDtypeStruct((NH_A, nch, DK_A, DV_A), F32),
                   jax.ShapeDtypeStruct((NH_A, nch, SUBLANE, DK_A), F32)],
        scratch_shapes=[pltpu.VMEM((NH_A, DK_A, DV_A), F32), pltpu.VMEM((NH_A, SUBLANE, DK_A), F32)],
        compiler_params=_cparams(("arbitrary", "arbitrary")),
    )(qk_act, qk_act, P, G)


def _mlstm_bwd(dm, qk_act, P, G, Cst, NM, H, dH, acc, reverse):
    TB, T = dm.TB, dm.T
    NC = TB // CHUNK
    idx = _mlstm_order(dm, reverse, True)
    vcol = dm.off["va"] // DV_A
    base = 2 * NH_A if reverse else 0
    has_acc = acc is not None
    HP = _heads_per_step()
    hcols = lambda hh, w: slice(hh * w, (hh + 1) * w)

    def kern(*refs):
        (q_ref, k_ref, v_ref, g_ref, cst_ref, nm_ref, hh_ref, dh_ref) = refs[:8]
        p = 8
        if has_acc:
            aq_ref, ak_ref, av_ref, ag_ref = refs[p:p + 4]
            p += 4
        dq_ref, dk_ref, dv_ref, dg_ref, R_s, Rn_s = refs[p:p + 6]
        i, hp = pl.program_id(0), pl.program_id(1)
        row8 = lax.broadcasted_iota(jnp.int32, (SUBLANE, DK_A), 0)
        heads = [hp * HP + hh for hh in range(HP)]

        @pl.when(i == 0)
        def _():
            for h in heads:
                R_s[h] = jnp.zeros((DK_A, DV_A), F32)
                Rn_s[h] = jnp.zeros((SUBLANE, DK_A), F32)

        @pl.when(hp == 0)
        def _():
            dg_ref[...] = ag_ref[...] if has_acc else jnp.zeros((TB, LANE), F32)

        masks = _chunk_masks(reverse)
        _, mask_f, maskT_f, eye_f = masks
        before_f = mask_f - eye_f
        state = [(R_s[h], Rn_s.at[h][0:1, :]) for h in heads]
        lane = lax.broadcasted_iota(jnp.int32, (CHUNK, LANE), 1)

        def as_row(col):
            return jnp.sum(eye_f * col, axis=0, keepdims=True)

        for c in (range(NC) if reverse else range(NC - 1, -1, -1)):
            rows = pl.ds(c * CHUNK, CHUNK)
            g = g_ref[rows, :]
            dg = jnp.zeros((CHUNK, LANE), F32)
            for hh, h in enumerate(heads):
                R, Rn = state[hh]
                qc, vc = hcols(hh, DK_A), hcols(hh, DV_A)
                q, k, v = q_ref[rows, qc], k_ref[rows, qc], v_ref[rows, vc].astype(F32)
                C0 = cst_ref[hh, c]
                n0, m0 = nm_ref.at[hh, c][0:1, :], nm_ref.at[hh, c][1:2, 0:1]
                qb, kb, s, w, a_c, m_c, den, w_end, a_end, _ = _chunk_fwd_core(
                    q, k, g, base + h, base + NH_A + h, C0, n0, m0, masks, reverse)
                vb = v.astype(BF16)
                e_m = jnp.exp(-m_c)
                r = 1.0 / jnp.maximum(jnp.abs(den), e_m)
                dh = dh_ref[rows, vc]
                dN = dh * r
                dD = jnp.where(jnp.abs(den) > e_m,
                               -jnp.sum(dh * hh_ref[rows, vc], axis=1, keepdims=True) * r * jnp.sign(den), 0.0)
                dNb = dN.astype(BF16)
                dS = _dot_nt(dNb, vb) + dD
                dqk = dS * w
                Cb, Rb = C0.astype(BF16), R.astype(BF16)
                dq_in = a_c * (_dot_nt(dNb, Cb) + dD * n0)
                dk_out = w_end * (_dot_nt(vb, Rb) + Rn)
                dq = _dot(dqk.astype(BF16), kb) + dq_in
                dk = _dot(dqk.T.astype(BF16), qb) + dk_out
                dv = _dot(s.T.astype(BF16), dNb) + w_end * _dot(kb, Rb)
                if has_acc:
                    dq_ref[rows, qc] = aq_ref[rows, qc] + dq
                    dk_ref[rows, qc] = ak_ref[rows, qc] + dk
                    dv_ref[rows, vc] = av_ref[rows, vc] + dv
                else:
                    dq_ref[rows, qc] = dq
                    dk_ref[rows, qc] = dk
                    dv_ref[rows, vc] = dv
                gm = dS * s
                g_row = jnp.sum(gm, axis=1, keepdims=True)
                g_col = jnp.sum(eye_f * jnp.sum(gm, axis=0, keepdims=True), axis=1, keepdims=True)
                q_in = jnp.sum(q * dq_in, axis=1, keepdims=True)
                k_out = jnp.sum(k * dk_out, axis=1, keepdims=True)
                through = a_end * (jnp.sum(jnp.sum(R * C0, axis=1, keepdims=True), axis=0, keepdims=True)
                                   + jnp.sum(Rn * n0, axis=1, keepdims=True))
                di = g_col + k_out
                df = (jnp.sum(maskT_f * as_row(g_row - g_col + q_in), axis=1, keepdims=True)
                      + jnp.sum(before_f * as_row(k_out), axis=1, keepdims=True) + through)
                dg = dg + jnp.where(lane == base + h, di, 0.0) + jnp.where(lane == base + NH_A + h, df, 0.0)
                aq = a_c * q
                state[hh] = (a_end * R + _dot(aq.T.astype(BF16), dNb),
                             a_end * Rn + jnp.sum(aq * dD, axis=0, keepdims=True))
            dg_ref[rows, :] += dg
        for hh, h in enumerate(heads):
            R, Rn = state[hh]
            R_s[h] = R
            Rn_s[h] = jnp.where(row8 == 0, Rn, 0.0)

    NG = NH_A // HP
    qspec = pl.BlockSpec((TB, HP * DK_A), lambda i, h: (idx(i), h))
    vspec = pl.BlockSpec((TB, HP * DV_A), lambda i, h: (idx(i), h))
    gspec = pl.BlockSpec((TB, LANE), lambda i, h: (idx(i), 0))
    in_specs = [qspec,
                pl.BlockSpec((TB, HP * DK_A), lambda i, h: (idx(i), NG + h)),
                pl.BlockSpec((TB, HP * DV_A), lambda i, h: (idx(i), vcol // HP + h)),
                gspec,
                pl.BlockSpec((HP, NC, DK_A, DV_A), lambda i, h: (h, idx(i), 0, 0)),
                pl.BlockSpec((HP, NC, SUBLANE, DK_A), lambda i, h: (h, idx(i), 0, 0)),
                vspec, vspec]
    args = [qk_act, qk_act, P, G, Cst, NM, H, dH]
    if has_acc:
        in_specs += [qspec, qspec, vspec, gspec]
        args += list(acc)
    return pl.pallas_call(
        kern, name="mlstm_bwd_rev" if reverse else "mlstm_bwd", grid=(dm.nblk, NG),
        in_specs=in_specs,
        out_specs=[qspec, qspec, vspec, gspec],
        out_shape=[jax.ShapeDtypeStruct((T, dm.QK), F32), jax.ShapeDtypeStruct((T, dm.QK), F32),
                   jax.ShapeDtypeStruct((T, dm.V), F32), jax.ShapeDtypeStruct((T, LANE), F32)],
        scratch_shapes=[pltpu.VMEM((NH_A, DK_A, DV_A), F32), pltpu.VMEM((NH_A, SUBLANE, DK_A), F32)],
        compiler_params=_cparams(("arbitrary", "arbitrary")),
    )(*args)


def _rms_heads(x, w_row, nh, hd):
    out = []
    for h in range(nh):
        xh = x[:, h * hd:(h + 1) * hd]
        rstd = lax.rsqrt(jnp.mean(xh * xh, axis=1, keepdims=True) + EPS)
        out.append((xh * rstd, rstd))
    return out


def _rope(x, cos, sa, sb):
    return x * cos + pltpu.roll(x, HD_B - HD_B // 4, 1) * sa + pltpu.roll(x, HD_B // 4, 1) * sb


def _rope_t(dy, cos, sa, sb):
    return dy * cos + pltpu.roll(dy * sa, HD_B // 4, 1) + pltpu.roll(dy * sb, HD_B - HD_B // 4, 1)


ATT_SCALE = HD_B ** -0.5
LOG2E = 1.4426950408889634
LN2 = 0.6931471805599453
QSCALE = ATT_SCALE * LOG2E


def _qk_prep(dm, P, rope, qn, kn):
    TB, nctx = dm.TB, dm.nctx
    qcol, kcol, vcol = dm.off["qb"] // dm.QB, dm.off["kb"] // dm.KVB, dm.off["vb"] // dm.KVB

    def kern_q(x_ref, t_ref, w_ref, o_ref):
        cos, sa, sb = t_ref[0], t_ref[1], t_ref[2]
        for h, (xn, _) in enumerate(_rms_heads(x_ref[...].astype(F32), None, NH_B, HD_B)):
            o_ref[h] = (_rope(xn * w_ref[0:1, :], cos, sa, sb) * QSCALE).astype(BF16)

    Qr = pl.pallas_call(
        kern_q, name="q_prep", grid=(dm.nlat,),
        in_specs=[pl.BlockSpec((TB, dm.QB), lambda i: (i + nctx, qcol)),
                  pl.BlockSpec((3, TB, HD_B), lambda i: (0, i + nctx, 0)),
                  pl.BlockSpec((SUBLANE, HD_B), lambda i: (0, 0))],
        out_specs=pl.BlockSpec((NH_B, TB, HD_B), lambda i: (0, i, 0)),
        out_shape=jax.ShapeDtypeStruct((NH_B, dm.S, HD_B), BF16),
        compiler_params=_cparams(("parallel",)),
    )(P, rope, qn)

    def kern_k(x_ref, v_ref, t_ref, w_ref, o_ref, vo_ref):
        cos, sa, sb = t_ref[0], t_ref[1], t_ref[2]
        for h, (xn, _) in enumerate(_rms_heads(x_ref[...].astype(F32), None, NKV_B, HD_B)):
            o_ref[:, h * HD_B:(h + 1) * HD_B] = _rope(xn * w_ref[0:1, :], cos, sa, sb).astype(BF16)
        vo_ref[...] = v_ref[...].astype(BF16)

    Kr, Vb = pl.pallas_call(
        kern_k, name="k_prep", grid=(dm.nblk,),
        in_specs=[pl.BlockSpec((TB, dm.KVB), lambda i: (i, kcol)),
                  pl.BlockSpec((TB, dm.KVB), lambda i: (i, vcol)),
                  pl.BlockSpec((3, TB, HD_B), lambda i: (0, i, 0)),
                  pl.BlockSpec((SUBLANE, HD_B), lambda i: (0, 0))],
        out_specs=[pl.BlockSpec((TB, dm.KVB), lambda i: (i, 0))] * 2,
        out_shape=[jax.ShapeDtypeStruct((dm.T, dm.KVB), BF16)] * 2,
        compiler_params=_cparams(("parallel",)),
    )(P, P, rope, kn)
    return Qr, Kr, Vb


def _attn_tiles(dm):
    return _tile(dm.S, 512, LANE), _tile(dm.T, 1408, LANE)


def _attn_fwd(dm, Qr, Kr, Vb):
    S, T, G = dm.S, dm.T, dm.G
    tq, tk = _tile(S, 256, LANE), T
    nk = T // tk

    def kern(q_ref, k_ref, v_ref, o_ref, l_ref, m_s, l_s, a_s):
        j = pl.program_id(2)
        k, v = k_ref[...], v_ref[...]
        if nk == 1:
            for h in range(G):
                s = _dot_nt(q_ref[h], k)
                m = jnp.max(s, axis=1, keepdims=True)
                p = jnp.exp2(s - m)
                l = jnp.sum(p, axis=1, keepdims=True)
                o_ref[h] = _dot(p.astype(BF16), v) / l
                l_ref[0, :, h:h + 1] = m + jnp.log(l) * LOG2E
            return

        @pl.when(j == 0)
        def _():
            m_s[...] = jnp.full(m_s.shape, -jnp.inf, F32)
            l_s[...] = jnp.zeros(l_s.shape, F32)
            a_s[...] = jnp.zeros(a_s.shape, F32)

        for h in range(G):
            s = _dot_nt(q_ref[h], k)
            m_old = m_s[h]
            m_new = jnp.maximum(m_old, jnp.max(s, axis=1, keepdims=True))
            p = jnp.exp2(s - m_new)
            corr = jnp.exp2(m_old - m_new)
            l_s[h] = corr * l_s[h] + jnp.sum(p, axis=1, keepdims=True)
            m_s[h] = m_new
            a_s[h] = corr * a_s[h] + _dot(p.astype(BF16), v)

        @pl.when(j == nk - 1)
        def _():
            for h in range(G):
                o_ref[h] = a_s[h] / l_s[h]
                l_ref[0, :, h:h + 1] = m_s[h] + jnp.log(l_s[h]) * LOG2E

    qspec = pl.BlockSpec((G, tq, HD_B), lambda g, i, j: (g, i, 0))
    return pl.pallas_call(
        kern, name="attn_fwd", grid=(NKV_B, S // tq, nk),
        in_specs=[qspec,
                  pl.BlockSpec((tk, HD_B), lambda g, i, j: (j, g)),
                  pl.BlockSpec((tk, HD_B), lambda g, i, j: (j, g))],
        out_specs=[qspec, pl.BlockSpec((1, tq, G), lambda g, i, j: (g, i, 0))],
        out_shape=[jax.ShapeDtypeStruct((NH_B, S, HD_B), F32), jax.ShapeDtypeStruct((NKV_B, S, G), F32)],
        scratch_shapes=[pltpu.VMEM((G, tq, 1), F32), pltpu.VMEM((G, tq, 1), F32),
                        pltpu.VMEM((G, tq, HD_B), F32)],
        compiler_params=_cparams(("parallel", "parallel", "arbitrary")),
    )(Qr, Kr, Vb)


def _attn_bwd(dm, Qr, Kr, Vb, dO, LSE_T, DEL_T):
    S, T, G = dm.S, dm.T, dm.G
    tq, tk = _attn_tiles(dm)
    nq, nkt = S // tq, T // tk

    def kern(q_ref, k_ref, v_ref, do_ref, l_ref, d_ref, dq_hbm, dk_ref, dv_ref, ak_s, av_s, dq_s, stage, sem):
        g, j, i = pl.program_id(0), pl.program_id(1), pl.program_id(2)
        rows = pl.ds(pl.multiple_of(i * tq, tq), tq)

        @pl.when(i == 0)
        def _():
            ak_s[...] = jnp.zeros(ak_s.shape, F32)
            av_s[...] = jnp.zeros(av_s.shape, F32)

        @pl.when(j == 0)
        def _():
            dq_s[:, rows, :] = jnp.zeros((G, tq, HD_B), F32)

        k, v = k_ref[...], v_ref[...]
        for h in range(G):
            q, do = q_ref[h], do_ref[h]
            pT = jnp.exp2(_dot_nt(k, q) - l_ref[0, h:h + 1, :])
            dpT = _dot_nt(v, do)
            dsT = (pT * (dpT - d_ref[0, h:h + 1, :])).astype(BF16)
            av_s[...] += _dot(pT.astype(BF16), do)
            ak_s[...] += _dot(dsT, q)
            dq_s[h, rows, :] += lax.dot_general(dsT, k, (((0,), (0,)), ((), ())),
                                                preferred_element_type=F32)

        @pl.when(i == nq - 1)
        def _():
            dk_ref[...] = ak_s[...] * LN2
            dv_ref[...] = av_s[...].astype(BF16)

        @pl.when(j == nkt - 1)
        def _():
            stage[...] = dq_s[:, rows, :] * ATT_SCALE
            out = pltpu.make_async_copy(stage, dq_hbm.at[pl.ds(g * G, G), rows, :], sem)
            out.start()
            out.wait()

    qspec = pl.BlockSpec((G, tq, HD_B), lambda g, j, i: (g, i, 0))
    kspec = pl.BlockSpec((tk, HD_B), lambda g, j, i: (j, g))
    lspec = pl.BlockSpec((1, G, tq), lambda g, j, i: (g, 0, i))
    return pl.pallas_call(
        kern, name="attn_bwd", grid=(NKV_B, nkt, nq),
        in_specs=[qspec, kspec, kspec, qspec, lspec, lspec],
        out_specs=[pl.BlockSpec(memory_space=pl.ANY), kspec, kspec],
        out_shape=[jax.ShapeDtypeStruct((NH_B, S, HD_B), F32), jax.ShapeDtypeStruct((T, dm.KVB), F32),
                   jax.ShapeDtypeStruct((T, dm.KVB), BF16)],
        scratch_shapes=[pltpu.VMEM((tk, HD_B), F32), pltpu.VMEM((tk, HD_B), F32),
                        pltpu.VMEM((G, S, HD_B), F32), pltpu.VMEM((G, tq, HD_B), F32),
                        pltpu.SemaphoreType.DMA],
        compiler_params=_cparams(("arbitrary", "arbitrary", "arbitrary")),
    )(Qr, Kr, Vb, dO, LSE_T, DEL_T)


def _qk_bwd(dm, dQr, dKr, P, rope, qn, kn):
    TB, nctx = dm.TB, dm.nctx
    qcol, kcol = dm.off["qb"] // dm.QB, dm.off["kb"] // dm.KVB

    def head_bwd(dyr, x, w_row, cos, sa, sb):
        rstd = lax.rsqrt(jnp.mean(x * x, axis=1, keepdims=True) + EPS)
        xn = x * rstd
        dy = _rope_t(dyr, cos, sa, sb)
        dw = jnp.sum(dy * xn, axis=0, keepdims=True)
        dxn = dy * w_row
        dx = rstd * (dxn - xn * jnp.mean(dxn * xn, axis=1, keepdims=True))
        return dx, dw

    def make(nh, ctx_zero):
        def kern(d_ref, x_ref, t_ref, w_ref, o_ref, gw_ref):
            i = pl.program_id(0)

            @pl.when(i == 0)
            def _():
                gw_ref[...] = jnp.zeros(gw_ref.shape, F32)

            def live():
                cos, sa, sb = t_ref[0], t_ref[1], t_ref[2]
                tot = jnp.zeros((1, HD_B), F32)
                for h in range(nh):
                    cols = slice(h * HD_B, (h + 1) * HD_B)
                    dyr = d_ref[h] if ctx_zero else d_ref[:, cols]
                    dx, dw = head_bwd(dyr, x_ref[:, cols].astype(F32), w_ref[0:1, :], cos, sa, sb)
                    o_ref[:, cols] = dx.astype(BF16)
                    tot = tot + dw
                gw_ref[0:1, :] += tot

            if ctx_zero:
                @pl.when(i < nctx)
                def _():
                    o_ref[...] = jnp.zeros(o_ref.shape, BF16)

                pl.when(i >= nctx)(live)
            else:
                live()
        return kern

    lat = lambda i: jnp.maximum(i - nctx, 0)
    d_qb, gqn = pl.pallas_call(
        make(NH_B, True), name="q_bwd", grid=(dm.nblk,),
        in_specs=[pl.BlockSpec((NH_B, TB, HD_B), lambda i: (0, lat(i), 0)),
                  pl.BlockSpec((TB, dm.QB), lambda i: (i, qcol)),
                  pl.BlockSpec((3, TB, HD_B), lambda i: (0, i, 0)),
                  pl.BlockSpec((SUBLANE, HD_B), lambda i: (0, 0))],
        out_specs=[pl.BlockSpec((TB, dm.QB), lambda i: (i, 0)), pl.BlockSpec((SUBLANE, HD_B), lambda i: (0, 0))],
        out_shape=[jax.ShapeDtypeStruct((dm.T, dm.QB), BF16), jax.ShapeDtypeStruct((SUBLANE, HD_B), F32)],
        compiler_params=_cparams(("arbitrary",)),
    )(dQr, P, rope, qn)
    d_kb, gkn = pl.pallas_call(
        make(NKV_B, False), name="k_bwd", grid=(dm.nblk,),
        in_specs=[pl.BlockSpec((TB, dm.KVB), lambda i: (i, 0)),
                  pl.BlockSpec((TB, dm.KVB), lambda i: (i, kcol)),
                  pl.BlockSpec((3, TB, HD_B), lambda i: (0, i, 0)),
                  pl.BlockSpec((SUBLANE, HD_B), lambda i: (0, 0))],
        out_specs=[pl.BlockSpec((TB, dm.KVB), lambda i: (i, 0)), pl.BlockSpec((SUBLANE, HD_B), lambda i: (0, 0))],
        out_shape=[jax.ShapeDtypeStruct((dm.T, dm.KVB), BF16), jax.ShapeDtypeStruct((SUBLANE, HD_B), F32)],
        compiler_params=_cparams(("arbitrary",)),
    )(dKr, P, rope, kn)
    return d_qb, d_kb, gqn, gkn


def _merge_prep(dm, Hf, Hb, P, O, mhw):
    TB, nctx = dm.TB, dm.nctx
    lat = lambda c: (lambda i: (i + nctx, c))

    def kern(hf_ref, hb_ref, oa_ref, za_ref, zb_ref, o_ref, w_ref, a_ref, b_ref, at_ref, bt_ref):
        for h in range(NH_A):
            cols = slice(h * DV_A, (h + 1) * DV_A)
            hs = hf_ref[:, cols] + hb_ref[:, cols]
            rstd = lax.rsqrt(jnp.mean(hs * hs, axis=1, keepdims=True) + EPS)
            za = za_ref[:, cols].astype(F32)
            a = _sigmoid(oa_ref[:, cols].astype(F32)) * (hs * rstd * w_ref[0:1, cols]) * (za * _sigmoid(za))
            a_ref[:, cols] = a.astype(BF16)
            at_ref[cols, :] = a.T.astype(BF16)
        for h in range(NH_B):
            cols = slice(h * HD_B, (h + 1) * HD_B)
            zb = zb_ref[:, cols].astype(F32)
            b = o_ref[h] * (zb * _sigmoid(zb))
            b_ref[:, cols] = b.astype(BF16)
            bt_ref[cols, :] = b.T.astype(BF16)

    return pl.pallas_call(
        kern, name="merge_prep", grid=(dm.nlat,),
        in_specs=[pl.BlockSpec((TB, dm.V), lat(0)), pl.BlockSpec((TB, dm.V), lat(0)),
                  pl.BlockSpec((TB, dm.V), lat(dm.off["oa"] // dm.V)),
                  pl.BlockSpec((TB, dm.V), lat(dm.off["za"] // dm.V)),
                  pl.BlockSpec((TB, dm.QB), lat(dm.off["zb"] // dm.QB)),
                  pl.BlockSpec((NH_B, TB, HD_B), lambda i: (0, i, 0)),
                  pl.BlockSpec((SUBLANE, dm.V), lambda i: (0, 0))],
        out_specs=[pl.BlockSpec((TB, dm.V), lambda i: (i, 0)), pl.BlockSpec((TB, dm.QB), lambda i: (i, 0)),
                   pl.BlockSpec((dm.V, TB), lambda i: (0, i)), pl.BlockSpec((dm.QB, TB), lambda i: (0, i))],
        out_shape=[jax.ShapeDtypeStruct((dm.S, dm.V), BF16), jax.ShapeDtypeStruct((dm.S, dm.QB), BF16),
                   jax.ShapeDtypeStruct((dm.V, dm.S), BF16), jax.ShapeDtypeStruct((dm.QB, dm.S), BF16)],
        compiler_params=_cparams(("parallel",)),
    )(Hf, Hb, P, P, P, O, mhw)


def _gate_merge(dm, ya, yb, P):
    TB, D, nctx = dm.TB, dm.D, dm.nctx
    gcol = dm.off["g"] // D

    def kern(ya_ref, yb_ref, ga_ref, gb_ref, o_ref, ot_ref):
        m = (_sigmoid(ga_ref[...].astype(F32)) * ya_ref[...]
             + _sigmoid(gb_ref[...].astype(F32)) * yb_ref[...])
        o_ref[...] = m.astype(BF16)
        ot_ref[...] = m.T.astype(BF16)

    row = pl.BlockSpec((TB, D), lambda i: (i, 0))
    return pl.pallas_call(
        kern, name="gate_merge", grid=(dm.nlat,),
        in_specs=[row, row, pl.BlockSpec((TB, D), lambda i: (i + nctx, gcol)),
                  pl.BlockSpec((TB, D), lambda i: (i + nctx, gcol + 1))],
        out_specs=[row, pl.BlockSpec((D, TB), lambda i: (0, i))],
        out_shape=[jax.ShapeDtypeStruct((dm.S, D), BF16), jax.ShapeDtypeStruct((D, dm.S), BF16)],
        compiler_params=_cparams(("parallel",)),
    )(ya, yb, P, P)


def _final(dm, x, out, tgt, modv, lnp):
    TB, D = dm.TB, dm.D

    def kern(x_ref, o_ref, t_ref, m_ref, p_ref, dr_ref, do_ref, cs_ref, ls_ref):
        i = pl.program_id(0)

        @pl.when(i == 0)
        def _():
            cs_ref[...] = jnp.zeros(cs_ref.shape, F32)
            ls_ref[...] = jnp.zeros(ls_ref.shape, F32)

        gate, lnw, lnb = m_ref[4:5, :], p_ref[0:1, :], p_ref[1:2, :]
        out = o_ref[...]
        xh, rstd = _ln_stats(ALPHA * x_ref[...] + gate * out)
        e = xh * lnw + lnb - t_ref[...]
        ls_ref[...] += 0.5 * jnp.sum(jnp.sum(e * e, axis=1, keepdims=True), axis=0, keepdims=True) / D
        dy = e * (1.0 / D)
        dxh = dy * lnw
        dr = rstd * (dxh - jnp.mean(dxh, axis=1, keepdims=True)
                     - xh * jnp.mean(dxh * xh, axis=1, keepdims=True))
        cs_ref[0:1, :] += jnp.sum(dy * xh, axis=0, keepdims=True)
        cs_ref[1:2, :] += jnp.sum(dy, axis=0, keepdims=True)
        cs_ref[2:3, :] += jnp.sum(dr * out, axis=0, keepdims=True)
        dr_ref[...] = ALPHA * dr
        do_ref[...] = (dr * gate).astype(BF16)

    row = pl.BlockSpec((TB, D), lambda i: (i, 0))
    par = pl.BlockSpec((SUBLANE, D), lambda i: (0, 0))
    return pl.pallas_call(
        kern, name="final_norm_loss", grid=(dm.nlat,),
        in_specs=[row, row, row, par, par],
        out_specs=[row, row, par, pl.BlockSpec((SUBLANE, LANE), lambda i: (0, 0))],
        out_shape=[jax.ShapeDtypeStruct((dm.S, D), F32), jax.ShapeDtypeStruct((dm.S, D), BF16),
                   jax.ShapeDtypeStruct((SUBLANE, D), F32), jax.ShapeDtypeStruct((SUBLANE, LANE), F32)],
        compiler_params=_cparams(("arbitrary",)),
    )(x, out, tgt, modv, lnp)


def _merge_bwd(dm, dM, ya, yb, P):
    TB, D, nctx = dm.TB, dm.D, dm.nctx
    gcol = dm.off["g"] // D
    lat = lambda i: jnp.maximum(i - nctx, 0)

    def kern(dm_ref, ya_ref, yb_ref, ga_ref, gb_ref, da_ref, db_ref, dg_ref):
        i = pl.program_id(0)

        @pl.when(i < nctx)
        def _():
            dg_ref[...] = jnp.zeros(dg_ref.shape, BF16)

        @pl.when(i >= nctx)
        def _():
            d = dm_ref[...]
            sa, sb = _sigmoid(ga_ref[...].astype(F32)), _sigmoid(gb_ref[...].astype(F32))
            da_ref[...] = (d * sa).astype(BF16)
            db_ref[...] = (d * sb).astype(BF16)
            dg_ref[:, 0:D] = (d * ya_ref[...] * sa * (1.0 - sa)).astype(BF16)
            dg_ref[:, D:2 * D] = (d * yb_ref[...] * sb * (1.0 - sb)).astype(BF16)

    row = pl.BlockSpec((TB, D), lambda i: (lat(i), 0))
    return pl.pallas_call(
        kern, name="merge_bwd", grid=(dm.nblk,),
        in_specs=[row, row, row, pl.BlockSpec((TB, D), lambda i: (i, gcol)),
                  pl.BlockSpec((TB, D), lambda i: (i, gcol + 1))],
        out_specs=[row, row, pl.BlockSpec((TB, 2 * D), lambda i: (i, 0))],
        out_shape=[jax.ShapeDtypeStruct((dm.S, D), BF16), jax.ShapeDtypeStruct((dm.S, D), BF16),
                   jax.ShapeDtypeStruct((dm.T, 2 * D), BF16)],
        compiler_params=_cparams(("arbitrary",)),
    )(dM, ya, yb, P, P)


def _branch_bwd(dm, dA, dB, Hf, Hb, P, O, mhw):
    TB, nctx, G = dm.TB, dm.nctx, dm.G
    lat = lambda i: jnp.maximum(i - nctx, 0)

    def kern(da_ref, db_ref, hf_ref, hb_ref, oa_ref, za_ref, zb_ref, o_ref, w_ref,
             doa_ref, dza_ref, dzb_ref, dh_ref, do_ref, del_ref, gw_ref):
        i = pl.program_id(0)

        @pl.when(i == 0)
        def _():
            gw_ref[...] = jnp.zeros(gw_ref.shape, F32)

        @pl.when(i < nctx)
        def _():
            doa_ref[...] = jnp.zeros(doa_ref.shape, BF16)
            dza_ref[...] = jnp.zeros(dza_ref.shape, BF16)
            dzb_ref[...] = jnp.zeros(dzb_ref.shape, BF16)
            dh_ref[...] = jnp.zeros(dh_ref.shape, F32)

        @pl.when(i >= nctx)
        def _():
            for h in range(NH_A):
                cols = slice(h * DV_A, (h + 1) * DV_A)
                hs = hf_ref[:, cols] + hb_ref[:, cols]
                rstd = lax.rsqrt(jnp.mean(hs * hs, axis=1, keepdims=True) + EPS)
                xn = hs * rstd
                w = w_ref[0:1, cols]
                hn = xn * w
                so, za = _sigmoid(oa_ref[:, cols].astype(F32)), za_ref[:, cols].astype(F32)
                sz = _sigmoid(za)
                silu = za * sz
                da = da_ref[:, cols]
                doa_ref[:, cols] = (da * hn * silu * so * (1.0 - so)).astype(BF16)
                dza_ref[:, cols] = (da * hn * so * sz * (1.0 + za * (1.0 - sz))).astype(BF16)
                dhn = da * so * silu
                gw_ref[0:1, cols] += jnp.sum(dhn * xn, axis=0, keepdims=True)
                dxn = dhn * w
                dh_ref[:, cols] = rstd * (dxn - xn * jnp.mean(dxn * xn, axis=1, keepdims=True))
            for h in range(NH_B):
                cols = slice(h * HD_B, (h + 1) * HD_B)
                zb = zb_ref[:, cols].astype(F32)
                sz = _sigmoid(zb)
                db, o = db_ref[:, cols], o_ref[h]
                do = db * (zb * sz)
                do_ref[h] = do.astype(BF16)
                dzb_ref[:, cols] = (db * o * sz * (1.0 + zb * (1.0 - sz))).astype(BF16)
                del_ref[h // G, :, (h % G):(h % G) + 1] = jnp.sum(do * o, axis=1, keepdims=True)

    vlat = pl.BlockSpec((TB, dm.V), lambda i: (lat(i), 0))
    qlat = pl.BlockSpec((TB, dm.QB), lambda i: (lat(i), 0))
    hlat = pl.BlockSpec((NH_B, TB, HD_B), lambda i: (0, lat(i), 0))
    vrow = pl.BlockSpec((TB, dm.V), lambda i: (i, 0))
    qrow = pl.BlockSpec((TB, dm.QB), lambda i: (i, 0))
    pv = lambda n: pl.BlockSpec((TB, dm.V), lambda i: (i, dm.off[n] // dm.V))
    return pl.pallas_call(
        kern, name="branch_bwd", grid=(dm.nblk,),
        in_specs=[vlat, qlat, vrow, vrow, pv("oa"), pv("za"),
                  pl.BlockSpec((TB, dm.QB), lambda i: (i, dm.off["zb"] // dm.QB)), hlat,
                  pl.BlockSpec((SUBLANE, dm.V), lambda i: (0, 0))],
        out_specs=[vrow, vrow, qrow, vrow, hlat,
                   pl.BlockSpec((NKV_B, TB, G), lambda i: (0, lat(i), 0)),
                   pl.BlockSpec((SUBLANE, dm.V), lambda i: (0, 0))],
        out_shape=[jax.ShapeDtypeStruct((dm.T, dm.V), BF16), jax.ShapeDtypeStruct((dm.T, dm.V), BF16),
                   jax.ShapeDtypeStruct((dm.T, dm.QB), BF16), jax.ShapeDtypeStruct((dm.T, dm.V), F32),
                   jax.ShapeDtypeStruct((NH_B, dm.S, HD_B), BF16), jax.ShapeDtypeStruct((NKV_B, dm.S, G), F32),
                   jax.ShapeDtypeStruct((SUBLANE, dm.V), F32)],
        compiler_params=_cparams(("arbitrary",)),
    )(dA, dB, Hf, Hb, P, P, P, O, mhw)


def _conv_bwd(dm, dq, dk, P, convp):
    W = 2 * dm.QK
    col = dm.off["qk"] // W
    kscale = DK_A ** -0.5
    TB = dm.TB

    def kern1(dq_ref, dk_ref, x_ref, p_ref, n_ref, c_ref, dz_ref):
        x = x_ref[...].astype(F32)
        xp, xn = _shifted(dm, x, p_ref, n_ref)
        z = c_ref[3:4, :] + xp * c_ref[0:1, :] + x * c_ref[1:2, :] + xn * c_ref[2:3, :]
        sz = _sigmoid(z)
        dact = jnp.concatenate([dq_ref[...], dk_ref[...] * kscale], axis=1)
        dz_ref[...] = dact * sz * (1.0 + z * (1.0 - sz))

    half = pl.BlockSpec((TB, dm.QK), lambda i: (i, 0))
    par = pl.BlockSpec((SUBLANE, W), lambda i: (0, 0))
    dz = pl.pallas_call(
        kern1, name="conv_bwd_act", grid=(dm.nblk,),
        in_specs=[half, half] + _conv_specs(dm, W, col, P_HALO) + [par],
        out_specs=pl.BlockSpec((TB, W), lambda i: (i, 0)),
        out_shape=jax.ShapeDtypeStruct((dm.T, W), F32),
        compiler_params=_cparams(("parallel",)),
    )(dq, dk, P, P, P, convp)

    def kern2(z_ref, zp_ref, zn_ref, x_ref, p_ref, n_ref, c_ref, dx_ref, cs_ref):
        @pl.when(pl.program_id(0) == 0)
        def _():
            cs_ref[...] = jnp.zeros(cs_ref.shape, F32)

        dz = z_ref[...]
        dzp, dzn = _shifted(dm, dz, zp_ref, zn_ref)
        dx_ref[...] = (dzn * c_ref[0:1, :] + dz * c_ref[1:2, :] + dzp * c_ref[2:3, :]).astype(BF16)
        x = x_ref[...].astype(F32)
        xp, xn = _shifted(dm, x, p_ref, n_ref)
        cs_ref[0:1, :] += jnp.sum(dz * xp, axis=0, keepdims=True)
        cs_ref[1:2, :] += jnp.sum(dz * x, axis=0, keepdims=True)
        cs_ref[2:3, :] += jnp.sum(dz * xn, axis=0, keepdims=True)
        cs_ref[3:4, :] += jnp.sum(dz, axis=0, keepdims=True)

    return pl.pallas_call(
        kern2, name="conv_bwd_taps", grid=(dm.nblk,),
        in_specs=_conv_specs(dm, W, 0) + _conv_specs(dm, W, col, P_HALO) + [par],
        out_specs=[pl.BlockSpec((TB, W), lambda i: (i, 0)), par],
        out_shape=[jax.ShapeDtypeStruct((dm.T, W), BF16), jax.ShapeDtypeStruct((SUBLANE, W), F32)],
        compiler_params=_cparams(("arbitrary",)),
    )(dz, dz, dz, P, P, P, convp)


def _gates_bwd(dm, dG, G):
    TB = dm.TB

    def kern(d_ref, g_ref, o_ref, cs_ref):
        @pl.when(pl.program_id(0) == 0)
        def _():
            cs_ref[...] = jnp.zeros(cs_ref.shape, F32)

        lane = lax.broadcasted_iota(jnp.int32, (TB, LANE), 1)
        is_f = ((lane // NH_A) % 2) == 1
        d = d_ref[...]
        dpre = jnp.where(lane < dm.NIF, jnp.where(is_f, d * (1.0 - jnp.exp(g_ref[...])), d), 0.0)
        cs_ref[0:1, :] += jnp.sum(dpre, axis=0, keepdims=True)
        if dm.IFP > LANE:
            o_ref[:, LANE:] = jnp.zeros((TB, dm.IFP - LANE), BF16)
        o_ref[:, 0:LANE] = dpre.astype(BF16)

    row = pl.BlockSpec((TB, LANE), lambda i: (i, 0))
    return pl.pallas_call(
        kern, name="gates_bwd", grid=(dm.nblk,),
        in_specs=[row, row],
        out_specs=[pl.BlockSpec((TB, dm.IFP), lambda i: (i, 0)), pl.BlockSpec((SUBLANE, LANE), lambda i: (0, 0))],
        out_shape=[jax.ShapeDtypeStruct((dm.T, dm.IFP), BF16), jax.ShapeDtypeStruct((SUBLANE, LANE), F32)],
        compiler_params=_cparams(("arbitrary",)),
    )(dG, G)


def _ln_mod_bwd(dm, dU, ctx2, x2, modv, dr_a):
    TB, D, nctx = dm.TB, dm.D, dm.nctx
    lat = lambda i: jnp.maximum(i - nctx, 0)

    def kern(du_ref, c_ref, x_ref, m_ref, dr_ref, gx_ref, cs_ref):
        i = pl.program_id(0)
        is_ctx = i < nctx

        @pl.when(i == 0)
        def _():
            cs_ref[...] = jnp.zeros(cs_ref.shape, F32)

        xh, rstd = _ln_stats(jnp.where(is_ctx, c_ref[...], x_ref[...]))
        du = du_ref[...]
        s_shift = jnp.sum(du, axis=0, keepdims=True)
        s_scale = jnp.sum(du * xh, axis=0, keepdims=True)
        cs_ref[0:1, :] += jnp.where(is_ctx, 0.0, s_shift)
        cs_ref[1:2, :] += jnp.where(is_ctx, 0.0, s_scale)
        cs_ref[2:3, :] += jnp.where(is_ctx, s_shift, 0.0)
        cs_ref[3:4, :] += jnp.where(is_ctx, s_scale, 0.0)

        @pl.when(i >= nctx)
        def _():
            dxh = du * (1.0 + m_ref[1:2, :])
            gx_ref[...] = dr_ref[...] + rstd * (dxh - jnp.mean(dxh, axis=1, keepdims=True)
                                                - xh * jnp.mean(dxh * xh, axis=1, keepdims=True))

    row = pl.BlockSpec((TB, D), lambda i: (i, 0))
    lrow = pl.BlockSpec((TB, D), lambda i: (lat(i), 0))
    par = pl.BlockSpec((SUBLANE, D), lambda i: (0, 0))
    return pl.pallas_call(
        kern, name="ln_mod_bwd", grid=(dm.nblk,),
        in_specs=[row] + _row_specs(dm) + [par, lrow],
        out_specs=[lrow, par],
        out_shape=[jax.ShapeDtypeStruct((dm.S, D), F32), jax.ShapeDtypeStruct((SUBLANE, D), F32)],
        compiler_params=_cparams(("arbitrary",)),
    )(dU, ctx2, x2, modv, dr_a)


def _mod_fwd(craw, w_loc, b_loc):
    R, D = craw.shape
    n = w_loc.shape[1]

    def kern(c_ref, w_ref, b_ref, o_ref):
        c = c_ref[...]
        o_ref[...] = _dot((c * _sigmoid(c)).astype(BF16), w_ref[...].astype(BF16)) + b_ref[0:1, :]

    return pl.pallas_call(
        kern, name="mod_fwd", out_shape=jax.ShapeDtypeStruct((R, n), F32),
        compiler_params=pltpu.CompilerParams(vmem_limit_bytes=VMEM_LIMIT),
    )(craw, w_loc, b_loc)


def _mod_bwd(crawT, dmod, w_loc):
    D, R = crawT.shape
    n = w_loc.shape[1]

    def kern(c_ref, d_ref, w_ref, gw_ref, dc_ref):
        c = c_ref[...]
        d = d_ref[...].astype(BF16)
        gw_ref[...] = _dot((c * _sigmoid(c)).astype(BF16), d)
        dc_ref[...] = _dot_nt(d, w_ref[...].astype(BF16))

    return pl.pallas_call(
        kern, name="mod_bwd",
        out_shape=[jax.ShapeDtypeStruct((D, n), F32), jax.ShapeDtypeStruct((R, D), F32)],
        compiler_params=pltpu.CompilerParams(vmem_limit_bytes=VMEM_LIMIT),
    )(crawT, dmod, w_loc)


def _cctx_grad(dsilu, c_ctx):
    def kern(p_ref, c_ref, o_ref):
        c = c_ref[...]
        sc = _sigmoid(c)
        o_ref[...] = p_ref[...] * (sc * (1.0 + c * (1.0 - sc)))

    return pl.pallas_call(kern, name="cctx_grad", out_shape=jax.ShapeDtypeStruct(c_ctx.shape, F32))(dsilu, c_ctx)


def _adamw(w, g, m, v, name):
    lead, (R, C) = w.shape[:-2], w.shape[-2:]
    assert all(d == 1 for d in lead)
    tb = _tile(R, max(SUBLANE, (1 << 19) // (4 * C) // SUBLANE * SUBLANE), SUBLANE)
    c1 = 1.0 / (1.0 - ADAM_B1 ** ADAM_STEP)
    c2 = 1.0 / (1.0 - ADAM_B2 ** ADAM_STEP)

    def kern(w_ref, g_ref, m_ref, v_ref, d_ref, nm_ref, nv_ref):
        g = g_ref[...]
        nm = ADAM_B1 * m_ref[...] + (1.0 - ADAM_B1) * g
        nv = ADAM_B2 * v_ref[...] + (1.0 - ADAM_B2) * (g * g)
        nm_ref[...] = nm
        nv_ref[...] = nv
        d_ref[...] = -ADAM_LR * ((nm * c1) / (jnp.sqrt(nv * c2) + ADAM_EPS) + ADAM_WD * w_ref[...])

    spec = pl.BlockSpec(lead + (tb, C), lambda i: (0,) * len(lead) + (i, 0))
    return pl.pallas_call(
        kern, name=name, grid=(R // tb,), in_specs=[spec] * 4, out_specs=[spec] * 3,
        out_shape=[jax.ShapeDtypeStruct(w.shape, F32)] * 3,
        compiler_params=_cparams(("parallel",)),
    )(w, g, m, v)


def _rope_tables(dm):
    half = HD_B // 2
    rows_n = dm.S // GRID_W
    row = jnp.repeat(jnp.arange(rows_n), GRID_W).astype(F32)
    col = jnp.tile(jnp.arange(GRID_W), rows_n).astype(F32)
    inv = ROPE_THETA ** (-jnp.arange(0, half, 2, dtype=F32) / half)
    ar, ac = row[:, None] * inv[None], col[:, None] * inv[None]
    cos = jnp.concatenate([jnp.cos(ar), jnp.cos(ar), jnp.cos(ac), jnp.cos(ac)], axis=1)
    zr = jnp.zeros_like(ar)
    sa = jnp.concatenate([-jnp.sin(ar), zr, -jnp.sin(ac), zr], axis=1)
    sb = jnp.concatenate([zr, jnp.sin(ar), zr, jnp.sin(ac)], axis=1)
    ctx = jnp.stack([jnp.ones((dm.Tc, HD_B), F32), jnp.zeros((dm.Tc, HD_B), F32), jnp.zeros((dm.Tc, HD_B), F32)])
    return jnp.concatenate([ctx, jnp.stack([cos, sa, sb])], axis=1)


def _rows8(*rows, width):
    out = [jnp.pad(r.reshape(-1).astype(F32), (0, width - r.size)) for r in rows]
    n = -(-len(out) // SUBLANE) * SUBLANE
    out += [jnp.zeros((width,), F32)] * (n - len(out))
    return jnp.stack(out)


def _ref_starts(dm):
    starts, o = {}, 0
    for n, wd in zip(dm.ref_names, dm.ref_widths):
        starts[n] = o
        o += wd
    return starts


def _to_padded(dm, shards):
    n = shards.shape[2]
    starts, wref = _ref_starts(dm), dict(zip(dm.ref_names, dm.ref_widths))
    pieces = []
    for name in dm.order:
        lo, hi = starts[name], starts[name] + wref[name]
        for j in range(N_DEV):
            a, b = max(lo, j * n), min(hi, (j + 1) * n)
            if a < b:
                pieces.append(shards[j][:, a - j * n:b - j * n])
        if dm.w[name] > wref[name]:
            pieces.append(jnp.zeros((shards.shape[1], dm.w[name] - wref[name]), shards.dtype))
    return jnp.concatenate(pieces, axis=1)


def _from_padded(dm, w_pad, n):
    starts = _ref_starts(dm)
    slabs = []
    for j in range(N_DEV):
        pieces = []
        for name, wd in zip(dm.ref_names, dm.ref_widths):
            a, b = max(starts[name], j * n), min(starts[name] + wd, (j + 1) * n)
            if a < b:
                o = dm.off[name] - starts[name]
                pieces.append(w_pad[:, a + o:b + o])
        slabs.append(jnp.concatenate(pieces, axis=1))
    return jnp.stack(slabs)


def kernel(x, c, ctx, c_ctx, w_mod, b_mod, w_in, b_if, conv_w, conv_b, mh_norm_w, q_norm_w, k_norm_w, w_branch_a, w_branch_b, w_out, ln_w, ln_b, loss_target, m_c_ctx, m_w_mod, m_b_mod, m_w_in, m_b_if, m_conv_w, m_conv_b, m_mh_norm_w, m_q_norm_w, m_k_norm_w, m_w_branch_a, m_w_branch_b, m_w_out, m_ln_w, m_ln_b, v_c_ctx, v_w_mod, v_b_mod, v_w_in, v_b_if, v_conv_w, v_conv_b, v_mh_norm_w, v_q_norm_w, v_k_norm_w, v_w_branch_a, v_w_branch_b, v_w_out, v_ln_w, v_ln_b):
    S, D = x.shape[1], x.shape[2]
    Tc = ctx.shape[1]
    dm = Dims(S, Tc, D)
    T, QK2 = dm.T, 2 * dm.QK
    me = 4 * lax.axis_index("x") + 2 * lax.axis_index("y") + lax.axis_index("c")
    n_mod = w_mod.shape[2]
    n_in = w_in.shape[2]
    n_cv = conv_w.shape[2]
    rb = w_out.shape[1]

    pack0 = _all_gather(_rows8(c[0], conv_w[0, 0], conv_w[0, 1], conv_w[0, 2], width=D), "ag_cond", False)
    c_all = pack0[:, 0, :]
    conv_full = jnp.transpose(pack0[:, 1:4, :n_cv], (1, 0, 2)).reshape(CONV_W, QK2)
    convp = _rows8(conv_full[0], conv_full[1], conv_full[2], conv_b[0], width=QK2)

    w_in_all = _all_gather(w_in[0].astype(BF16), "ag_w_in", True)
    Wp = _to_padded(dm, w_in_all)
    wsq = jnp.concatenate([w_branch_a[0], w_branch_b[0], w_out[0]], axis=0).astype(BF16)

    craw = _rows8(*[c_all[j] for j in range(N_DEV)], c_ctx, width=D)
    b_loc = _rows8(lax.dynamic_slice(b_mod[0], (me * n_mod,), (n_mod,)), width=n_mod)
    mod_all = _all_gather(_mod_fwd(craw, w_mod[0], b_loc), "ag_mod", False)
    mod_rows = jnp.transpose(mod_all, (1, 0, 2)).reshape(2 * SUBLANE, 3 * D)
    mod_me = lax.dynamic_slice(mod_rows, (me, 0), (1, 3 * D))[0]
    mod_cx = mod_rows[N_DEV]
    modv = _rows8(mod_me[0:D], mod_me[D:2 * D], mod_cx[0:D], mod_cx[D:2 * D], mod_me[2 * D:3 * D], width=D)

    U, UT = _ln_mod_fwd(dm, ctx[0], x[0], modv)
    P, wsq_all = _mm(U, Wp, "mm_in_proj", tn=896, out_dtype=BF16, exchange=("ag", wsq))
    gate_pre = _mm(U, Wp[:, dm.off["if"]:dm.off["if"] + dm.IFP], "mm_gates", tn=dm.IFP)
    Wba = wsq_all[:, 0:rb, :].reshape(dm.V, D)
    Wbb = wsq_all[:, rb:2 * rb, :].reshape(dm.QB, D)
    Wout = wsq_all[:, 2 * rb:3 * rb, :].reshape(D, D)
    qk_act = _conv_silu_fwd(dm, P, convp)
    G = _gates_fwd(dm, gate_pre, _rows8(b_if[0], width=LANE))
    Hf, Cf, NMf = _mlstm_fwd(dm, qk_act, P, G, False)
    Hb, Cb, NMb = _mlstm_fwd(dm, qk_act, P, G, True)
    rope = _rope_tables(dm)
    qn, kn = _rows8(q_norm_w[0], width=HD_B), _rows8(k_norm_w[0], width=HD_B)
    Qr, Kr, Vb = _qk_prep(dm, P, rope, qn, kn)
    O, LSE = _attn_fwd(dm, Qr, Kr, Vb)
    mhw = _rows8(mh_norm_w[0], width=dm.V)
    A_in, B_in, A_inT, B_inT = _merge_prep(dm, Hf, Hb, P, O, mhw)
    ya = _mm(A_in, Wba, "mm_branch_a")
    yb = _mm(B_in, Wbb, "mm_branch_b")
    M_in, M_inT = _gate_merge(dm, ya, yb, P)
    out = _mm(M_in, Wout, "mm_out")
    lnp = _rows8(ln_w[0], ln_b[0], width=D)
    dr_a, d_out, cs_fin, loss_p = _final(dm, x[0], out, loss_target[0], modv, lnp)
    loss = lax.psum(loss_p[0, 0], ("x", "y", "c"))

    dM = _mm(d_out, Wout, "mm_d_merge", nt=True)
    gWout = _mm(M_inT, d_out, "mm_g_w_out", tk=2048, out_dtype=BF16)
    d_ya, d_yb, d_g = _merge_bwd(dm, dM, ya, yb, P)
    dA = _mm(d_ya, Wba, "mm_d_a", nt=True)
    gWba = _mm(A_inT, d_ya, "mm_g_w_ba", tk=2048, out_dtype=BF16)
    dB = _mm(d_yb, Wbb, "mm_d_b", nt=True)
    gWbb = _mm(B_inT, d_yb, "mm_g_w_bb", tk=2048, out_dtype=BF16)
    d_oa, d_za, d_zb, dH, dO, DEL, gmh = _branch_bwd(dm, dA, dB, Hf, Hb, P, O, mhw)
    dQr, dKr, d_vb = _attn_bwd(dm, Qr, Kr, Vb, dO, jnp.transpose(LSE, (0, 2, 1)), jnp.transpose(DEL, (0, 2, 1)))
    d_qb, d_kb, gqn, gkn = _qk_bwd(dm, dQr, dKr, P, rope, qn, kn)
    acc = _mlstm_bwd(dm, qk_act, P, G, Cf, NMf, Hf, dH, None, False)
    dq, dk, dv, dG = _mlstm_bwd(dm, qk_act, P, G, Cb, NMb, Hb, dH, acc, True)
    d_qk, cs_conv = _conv_bwd(dm, dq, dk, P, convp)
    d_if, gbif = _gates_bwd(dm, dG, G)
    parts = {"g": d_g, "qk": d_qk, "va": dv.astype(BF16), "oa": d_oa, "za": d_za, "qb": d_qb,
             "zb": d_zb, "kb": d_kb, "vb": d_vb, "if": d_if}
    dP = jnp.concatenate([parts[n] for n in dm.order], axis=1)
    gsq = jnp.concatenate([gWba.reshape(N_DEV, rb, D), gWbb.reshape(N_DEV, rb, D),
                           gWout.reshape(N_DEV, rb, D)], axis=1)
    gWp, gsq_all = _mm(UT, dP, "mm_g_w_in", tm=1024, tn=896, tk=2816, out_dtype=BF16,
                       exchange=("a2a", gsq))
    gW = _from_padded(dm, gWp, n_in)
    dU, gW_all = _mm(dP, Wp, "mm_d_u", nt=True, tk=1792, exchange=("a2a", gW))
    grad_x, cs_ln = _ln_mod_bwd(dm, dU, ctx[0], x[0], modv, dr_a)

    dmod_me = _rows8(jnp.concatenate([cs_ln[0], cs_ln[1], cs_fin[2]]),
                     jnp.concatenate([cs_ln[2], cs_ln[3], jnp.zeros((D,), F32)]), width=3 * D)
    dmod_all = _all_gather(dmod_me, "ag_dmod", False)
    dmod_loc = lax.dynamic_slice(dmod_all, (0, 0, me * n_mod), (N_DEV, 2, n_mod))
    dmod_rows = _rows8(*[dmod_loc[j, 0] for j in range(N_DEV)], jnp.sum(dmod_loc[:, 1, :], axis=0), width=n_mod)
    g_w_mod, dc_part = _mod_bwd(craw.T, dmod_rows, w_mod[0])

    PW = dm.PW
    small = _rows8(cs_fin[0], cs_fin[1], gmh[0], cs_conv[3], cs_conv[0], cs_conv[1], cs_conv[2],
                   dmod_me[0, 0:D], dmod_me[0, D:2 * D], dmod_me[0, 2 * D:3 * D],
                   dmod_me[1, 0:D], dmod_me[1, D:2 * D],
                   jnp.concatenate([gqn[0], gkn[0], gbif[0]]), dc_part[N_DEV], width=PW)
    tot = _sum_slots(_all_gather(small, "ag_small", False), "sum_small")
    g_ln_w, g_ln_b, g_mh, g_conv_b = tot[0, :D], tot[1, :D], tot[2, :dm.V], tot[3, :QK2]
    g_conv_full = tot[4:7, :QK2]
    g_b_mod = jnp.concatenate([tot[7, :D] + tot[10, :D], tot[8, :D] + tot[11, :D], tot[9, :D]])
    g_qn, g_kn, g_bif = tot[12, 0:HD_B], tot[12, HD_B:2 * HD_B], tot[12, 2 * HD_B:2 * HD_B + dm.NIF]
    g_c_ctx = _cctx_grad(tot[13:14, :D], c_ctx.reshape(1, D))[0]
    g_conv_w = lax.dynamic_slice(g_conv_full, (0, me * n_cv), (CONV_W, n_cv))

    g_w_in = _sum_slots(gW_all, "sum_g_w_in")
    g_sq = _sum_slots(gsq_all, "sum_g_w_sq")

    upd_in = [a[None] for a in _adamw(w_in[0], g_w_in, m_w_in[0], v_w_in[0], "adam_w_in")]
    g_w_in, g_w_mod = g_w_in[None], g_w_mod[None]
    g_ba, g_bb, g_out = g_sq[None, 0:rb], g_sq[None, rb:2 * rb], g_sq[None, 2 * rb:3 * rb]
    upd_md = _adamw(w_mod, g_w_mod, m_w_mod, v_w_mod, "adam_w_mod")
    upd_ba = _adamw(w_branch_a, g_ba, m_w_branch_a, v_w_branch_a, "adam_w_ba")
    upd_bb = _adamw(w_branch_b, g_bb, m_w_branch_b, v_w_branch_b, "adam_w_bb")
    upd_out = _adamw(w_out, g_out, m_w_out, v_w_out, "adam_w_out")
    names = ["c_ctx", "b_mod", "b_if", "conv_w", "conv_b", "mh", "qn", "kn", "ln_w", "ln_b"]
    ws = [c_ctx, b_mod, b_if, conv_w, conv_b, mh_norm_w, q_norm_w, k_norm_w, ln_w, ln_b]
    ms = [m_c_ctx, m_b_mod, m_b_if, m_conv_w, m_conv_b, m_mh_norm_w, m_q_norm_w, m_k_norm_w, m_ln_w, m_ln_b]
    vs = [v_c_ctx, v_b_mod, v_b_if, v_conv_w, v_conv_b, v_mh_norm_w, v_q_norm_w, v_k_norm_w, v_ln_w, v_ln_b]
    gs = [g_c_ctx, g_b_mod, g_bif, g_conv_w, g_conv_b, g_mh, g_qn, g_kn, g_ln_w, g_ln_b]
    sizes = [a.size for a in ws]
    tot_n = sum(sizes)
    padn = -(-tot_n // LANE) * LANE
    flat = lambda arrs: jnp.pad(jnp.concatenate([a.reshape(-1) for a in arrs]), (0, padn - tot_n)).reshape(1, padn)
    d_s, nm_s, nv_s = _adamw(flat(ws), flat(gs), flat(ms), flat(vs), "adam_small")

    def split(a):
        res, o = {}, 0
        for n, wv, sz in zip(names, ws, sizes):
            res[n] = a[0, o:o + sz].reshape(wv.shape)
            o += sz
        return res

    def assemble(s, big_in, big_md, big_ba, big_bb, big_out):
        return [s["c_ctx"], big_md, s["b_mod"], big_in, s["b_if"], s["conv_w"], s["conv_b"],
                s["mh"], s["qn"], s["kn"], big_ba, big_bb, big_out, s["ln_w"], s["ln_b"]]

    g_small = {n: g.reshape(wv.shape) for n, g, wv in zip(names, gs, ws)}
    grads = assemble(g_small, g_w_in, g_w_mod, g_ba, g_bb, g_out)
    deltas, new_m, new_v = [
        assemble(split(sm), upd_in[t], upd_md[t], upd_ba[t], upd_bb[t], upd_out[t])
        for t, sm in enumerate((d_s, nm_s, nv_s))]
    return (loss, grad_x[None], *grads, *deltas, *new_m, *new_v)
```
